```python
import math
import jax
import jax.numpy as jnp
from jax import lax
import numpy as np

D_MODEL = 1024
BATCH = 16
SEQ = 256
DEPTH = 2
DEC_BATCH = 2
DEC_SEQ = 1024
PAST_LEN = 256

GRID_W = 64
BRANCH_W = 256
N_BRANCH = 4
HEAD_A = 64
N_HEADS_A = BRANCH_W // HEAD_A
LORA_W = 64
LORA_A = 64
LORA_G = 128
DECAY_SCALE = math.exp(-0.5)
POOL_WINDOWS = (2, 4, 8, 16)
N_POOL_GROUPS = len(POOL_WINDOWS)
POOL_GROUP_W = BRANCH_W // N_POOL_GROUPS
CONV_W = 3
CHUNK = 128
N_SGU_GROUPS = 4
SGU_GROUP_W = BRANCH_W // N_SGU_GROUPS
N_EXPERTS = 16
EXPERT_FF = 1024
EC_FACTOR = 2
N_MOD = 6
NORM_EPS = 1e-6
GN_EPS = 64e-5
IN_SPLITS = (BRANCH_W, BRANCH_W, BRANCH_W, 2 * LORA_W, 2 * LORA_A, LORA_G,
             BRANCH_W, BRANCH_W, BRANCH_W, BRANCH_W, BRANCH_W, BRANCH_W, N_BRANCH * D_MODEL)
N_IN = sum(IN_SPLITS)
IN_OFFSETS = tuple(int(o) for o in np.cumsum(IN_SPLITS)[:-1])

kernel_name = 'hybrid_rwkv7_pool_conv_sgu_ecmoe_diffusion_step'


def rms_norm(x, g):
    xf = x.astype(jnp.float32)
    xf = xf * lax.rsqrt(jnp.mean(xf * xf, axis=-1, keepdims=True) + NORM_EPS)
    return (xf * g.astype(jnp.float32)).astype(x.dtype)


def rwkv7_bidir(r, k, v, xw, xa, xg, s0, w0, w_up, a0, a_up, g_up, k_k, k_a, r_k, lnx_g, lnx_b):
    f32 = jnp.float32
    B, L, _ = r.shape
    H, N = N_HEADS_A, HEAD_A
    dec = w0.astype(f32) + jnp.einsum('bldr,drc->bldc', jnp.tanh(xw.astype(f32)), w_up.astype(f32))
    decay = jnp.exp(-DECAY_SCALE * jax.nn.sigmoid(dec)).reshape(B, L, 2, H, N)
    a = jax.nn.sigmoid(a0.astype(f32) + jnp.einsum('bldr,drc->bldc', xa.astype(f32), a_up.astype(f32)))
    a = a.reshape(B, L, 2, H, N)
    g = jnp.einsum('blr,rc->blc', jax.nn.sigmoid(xg), g_up)
    rf = r.astype(f32).reshape(B, L, H, N)
    kf = k.astype(f32).reshape(B, L, H, N)
    vf = v.astype(f32).reshape(B, L, H, N)
    kk = kf * k_k.astype(f32).reshape(H, N)
    kk = kk / jnp.maximum(jnp.sqrt(jnp.sum(kk * kk, axis=-1, keepdims=True)), 1e-12)
    k_in = kf[:, :, None] * (1.0 + (a - 1.0) * k_a.astype(f32).reshape(H, N))

    def per_dir(t):
        t = jnp.transpose(t, (1, 2, 0, 3, 4))
        return jnp.stack([t[:, 0], t[::-1, 1]], axis=1)

    def shared(t):
        t = jnp.transpose(t, (1, 0, 2, 3))
        return jnp.stack([t, t[::-1]], axis=1)

    xs = (shared(rf), per_dir(decay), per_dir(k_in), shared(vf), shared(kk), per_dir(a))

    def step(s, inp):
        r_t, w_t, k_t, v_t, kk_t, a_t = inp
        s_kk = jnp.einsum('dbhvk,dbhk->dbhv', s, kk_t)
        s = (s * w_t[..., None, :] - s_kk[..., :, None] * (a_t * kk_t)[..., None, :]
             + v_t[..., :, None] * k_t[..., None, :])
        return s, jnp.einsum('dbhvk,dbhk->dbhv', s, r_t)

    s_fin, ys = lax.scan(step, s0.astype(f32), xs)
    y = jnp.transpose(ys[:, 0] + ys[::-1, 1], (1, 0, 2, 3))
    mu = jnp.mean(y, axis=-1, keepdims=True)
    var = jnp.mean(jnp.square(y - mu), axis=-1, keepdims=True)
    y = (y - mu) * lax.rsqrt(var + GN_EPS) * lnx_g.astype(f32).reshape(H, N) + lnx_b.astype(f32).reshape(H, N)
    bonus = jnp.sum(rf[:, :, None] * k_in * r_k.astype(f32), axis=-1, keepdims=True)
    y = y + jnp.sum(bonus, axis=2) * vf
    o = y.reshape(B, L, BRANCH_W).astype(r.dtype) * g
    return o, s_fin


def multiscale_pool(u, w_pool):
    f32 = jnp.float32
    L = u.shape[-3]
    uf = u.astype(f32)
    cs = jnp.cumsum(uf, axis=-3)
    cs = jnp.concatenate([jnp.zeros_like(cs[..., :1, :, :]), cs], axis=-3)
    half = jnp.array(POOL_WINDOWS, jnp.int32) // 2
    pos = jnp.arange(L, dtype=jnp.int32)[:, None]
    lo = jnp.clip(pos - half, 0, L)
    hi = jnp.clip(pos + half, 0, L)
    gi = jnp.arange(N_POOL_GROUPS, dtype=jnp.int32)[None, :]
    mean = (cs[..., hi, gi, :] - cs[..., lo, gi, :]) / (hi - lo).astype(f32)[:, :, None]
    out = jnp.einsum('...gc,gcd->...gd', mean - uf, w_pool.astype(f32))
    return out.astype(u.dtype)


def centred_conv(t, conv_w, conv_b):
    L = t.shape[-2]
    p = CONV_W // 2
    tp = jnp.pad(t, [(0, 0)] * (t.ndim - 2) + [(p, p), (0, 0)])
    out = conv_b + tp[..., 0:L, :] * conv_w[0]
    for j in range(1, CONV_W):
        out = out + tp[..., j:j + L, :] * conv_w[j]
    return out


def spatial_gating(u, v, sgu_g, w_s, b_s):
    f32 = jnp.float32
    B, L, _ = u.shape
    u = jax.nn.gelu(u)
    vf = jax.nn.gelu(v.astype(f32)).reshape(B, L // CHUNK, CHUNK, N_SGU_GROUPS, SGU_GROUP_W)
    mu = jnp.mean(vf, axis=-1, keepdims=True)
    var = jnp.mean(jnp.square(vf - mu), axis=-1, keepdims=True)
    vf = (vf - mu) * lax.rsqrt(var + NORM_EPS) * sgu_g.astype(f32).reshape(N_SGU_GROUPS, SGU_GROUP_W)
    s = jnp.einsum('gpq,bnqgc->bnpgc', w_s.astype(f32), vf) + b_s.astype(f32).T[:, :, None]
    return u * s.reshape(B, L, BRANCH_W).astype(u.dtype)


def token_mixer(h, s0, on_grid, p):
    B, L, _ = h.shape
    z = h @ p['w_in']
    r, k, v, xw, xa, xg, pin, cin, cb, cc, du, dv, gl = jnp.split(z, IN_OFFSETS, axis=-1)
    o_a, s_fin = rwkv7_bidir(r, k, v, xw.reshape(B, L, 2, LORA_W), xa.reshape(B, L, 2, LORA_A), xg, s0,
                             p['w0'], p['w_up'], p['a0'], p['a_up'], p['g_up'], p['k_k'], p['k_a'],
                             p['r_k'], p['lnx_g'], p['lnx_b'])
    if on_grid:
        rows = L // GRID_W
        lay = (B, rows, GRID_W)
    else:
        lay = (B, L)
    o_b = multiscale_pool(pin.reshape(lay + (N_POOL_GROUPS, POOL_GROUP_W)), p['w_pool'])
    o_b = o_b.reshape(B, L, BRANCH_W) * p['pool_scale']
    o_c = cb * centred_conv((cc * cin).reshape(lay + (BRANCH_W,)), p['conv_w'], p['conv_b']).reshape(B, L, BRANCH_W)
    o_d = spatial_gating(du, dv, p['sgu_g'], p['w_s'], p['b_s'])
    br = jnp.stack([o_a, o_b, o_c, o_d], axis=2)
    proj = jnp.einsum('blic,icd->blid', br, p['w_branch'])
    gate = jax.nn.sigmoid(gl.reshape(B, L, N_BRANCH, D_MODEL))
    merged = jnp.einsum('blid,blid->bld', gate, proj)
    return merged @ p['w_out'], s_fin


def expert_choice_ffn(h, w_router, w_e1, w_e3, w_e2):
    B, L, D = h.shape
    cap = EC_FACTOR * L // N_EXPERTS
    aff = jax.nn.softmax(jnp.einsum('bld,de->ble', h.astype(jnp.float32), w_router.astype(jnp.float32)), axis=-1)
    gate, idx = lax.top_k(jnp.swapaxes(aff, 1, 2), cap)
    xs = jax.vmap(lambda hb, ib: hb[ib])(h, idx)
    act = jax.nn.silu(jnp.einsum('becd,edf->becf', xs, w_e1)) * jnp.einsum('becd,edf->becf', xs, w_e3)
    out = jnp.einsum('becf,efd->becd', act, w_e2) * gate[..., None].astype(h.dtype)
    return jax.vmap(lambda ib, ob: jnp.zeros((L, D), ob.dtype).at[ib.reshape(-1)].add(ob.reshape(-1, D)))(idx, out)


def trunk_layer(x, cond, s0, on_grid, p):
    mod = (jax.nn.silu(cond) @ p['w_mod'] + p['b_mod']).reshape(cond.shape[0], 1, N_MOD, D_MODEL)
    shift1, scale1, gate1, shift2, scale2, gate2 = (mod[:, :, i] for i in range(N_MOD))
    h = rms_norm(x, p['norm1_g']) * (1.0 + scale1) + shift1
    mix, s_fin = token_mixer(h, s0, on_grid, p)
    x = x + gate1 * mix
    h = rms_norm(x, p['norm2_g']) * (1.0 + scale2) + shift2
    x = x + gate2 * expert_choice_ffn(h, p['w_router'], p['w_e1'], p['w_e3'], p['w_e2'])
    return x, s_fin


def setup_inputs(seed: int = 0) -> dict:
    key = jax.random.key(seed)
    keys = jax.random.split(key, 40)

    def nrm(i, shape, scale):
        return jax.random.normal(keys[i], shape, jnp.float32) * scale

    W, D, E, F = BRANCH_W, D_MODEL, N_EXPERTS, EXPERT_FF
    return {
        'x_prompt': nrm(0, (BATCH, SEQ, D), 1.0),
        'x_sample': nrm(1, (DEC_BATCH, DEC_SEQ, D), 1.0),
        'state_rwkv': nrm(2, (DEC_BATCH, DEPTH, 2, N_HEADS_A, HEAD_A, HEAD_A), 0.5),
        'c': nrm(3, (DEC_BATCH, D), 1.0),
        'c_ctx': nrm(4, (D,), 1.0),
        'norm1_g': 1.0 + nrm(5, (DEPTH, D), 0.02),
        'w_mod': nrm(6, (DEPTH, D, N_MOD * D), 0.5 * D ** -0.5),
        'b_mod': nrm(7, (DEPTH, N_MOD * D), 0.01),
        'w_in': nrm(8, (DEPTH, D, N_IN), D ** -0.5),
        'w0': nrm(9, (DEPTH, 2, W), 0.5),
        'w_up': nrm(10, (DEPTH, 2, LORA_W, W), 0.5 * LORA_W ** -0.5),
        'a0': nrm(11, (DEPTH, 2, W), 0.5),
        'a_up': nrm(12, (DEPTH, 2, LORA_A, W), 0.5 * LORA_A ** -0.5),
        'g_up': nrm(13, (DEPTH, LORA_G, W), LORA_G ** -0.5),
        'k_k': 0.85 + nrm(14, (DEPTH, W), 0.02),
        'k_a': 1.0 + nrm(15, (DEPTH, W), 0.02),
        'r_k': nrm(16, (DEPTH, N_HEADS_A, HEAD_A), 0.1),
        'lnx_g': 1.0 + nrm(17, (DEPTH, W), 0.02),
        'lnx_b': nrm(18, (DEPTH, W), 0.01),
        'w_pool': nrm(19, (DEPTH, N_POOL_GROUPS, POOL_GROUP_W, POOL_GROUP_W), POOL_GROUP_W ** -0.5),
        'pool_scale': 1.0 + nrm(20, (DEPTH, W), 0.02),
        'conv_w': nrm(21, (DEPTH, CONV_W, W), CONV_W ** -0.5),
        'conv_b': nrm(22, (DEPTH, W), 0.01),
        'sgu_g': 1.0 + nrm(23, (DEPTH, W), 0.02),
        'w_s': nrm(24, (DEPTH, N_SGU_GROUPS, CHUNK, CHUNK), CHUNK ** -0.5),
        'b_s': 1.0 + nrm(25, (DEPTH, N_SGU_GROUPS, CHUNK), 0.02),
        'w_branch': nrm(26, (DEPTH, N_BRANCH, W, D), W ** -0.5),
        'w_out': nrm(27, (DEPTH, D, D), D ** -0.5),
        'norm2_g': 1.0 + nrm(28, (DEPTH, D), 0.02),
        'w_router': nrm(29, (DEPTH, D, E), D ** -0.5),
        'w_e1': nrm(30, (DEPTH, E, D, F), D ** -0.5),
        'w_e3': nrm(31, (DEPTH, E, D, F), D ** -0.5),
        'w_e2': nrm(32, (DEPTH, E, F, D), F ** -0.5),
        'final_g': 1.0 + nrm(33, (D,), 0.02),
    }


def reference(x_prompt, x_sample, state_rwkv, c, c_ctx, norm1_g, w_mod, b_mod, w_in, w0, w_up, a0, a_up,
              g_up, k_k, k_a, r_k, lnx_g, lnx_b, w_pool, pool_scale, conv_w, conv_b, sgu_g, w_s, b_s,
              w_branch, w_out, norm2_g, w_router, w_e1, w_e3, w_e2, final_g):
    B = x_prompt.shape[0]
    x_ctx = x_prompt
    x_lat = x_sample
    ctx_states = []
    for l in range(DEPTH):
        p = {'norm1_g': norm1_g[l], 'w_mod': w_mod[l], 'b_mod': b_mod[l], 'w_in': w_in[l],
             'w0': w0[l], 'w_up': w_up[l], 'a0': a0[l], 'a_up': a_up[l], 'g_up': g_up[l],
             'k_k': k_k[l], 'k_a': k_a[l], 'r_k': r_k[l], 'lnx_g': lnx_g[l], 'lnx_b': lnx_b[l],
             'w_pool': w_pool[l], 'pool_scale': pool_scale[l], 'conv_w': conv_w[l], 'conv_b': conv_b[l],
             'sgu_g': sgu_g[l], 'w_s': w_s[l], 'b_s': b_s[l], 'w_branch': w_branch[l], 'w_out': w_out[l],
             'norm2_g': norm2_g[l], 'w_router': w_router[l], 'w_e1': w_e1[l], 'w_e3': w_e3[l], 'w_e2': w_e2[l]}
        s0_ctx = jnp.zeros((2, B, N_HEADS_A, HEAD_A, HEAD_A), jnp.float32)
        x_ctx, s_ctx = trunk_layer(x_ctx, c_ctx[None], s0_ctx, False, p)
        ctx_states.append(jnp.transpose(s_ctx, (1, 0, 2, 3, 4)).astype(x_prompt.dtype))
        s0_lat = jnp.transpose(state_rwkv[:, l], (1, 0, 2, 3, 4))
        x_lat, _ = trunk_layer(x_lat, c, s0_lat, True, p)
    y_prompt = rms_norm(x_ctx, final_g)
    y_sample = rms_norm(x_lat, final_g)
    new_state_rwkv = jnp.stack(ctx_states, axis=1)
    return (y_prompt, y_sample, new_state_rwkv)
```

```python
import functools
import math

import numpy as np
import jax
import jax.numpy as jnp
from jax import lax
from jax.experimental import pallas as pl
from jax.experimental.pallas import tpu as pltpu

F32 = jnp.float32
BF16 = jnp.bfloat16

D = 1024
N_CTX_SEQ, L_CTX = 16, 256
N_LAT_SEQ, L_LAT = 2, 1024
T_CTX = N_CTX_SEQ * L_CTX
T_LAT = N_LAT_SEQ * L_LAT
T_ALL = T_CTX + T_LAT
DEPTH = 2
TM = 256
N_BLK = T_ALL // TM
N_CTX_BLK = T_CTX // TM
LAT_BLK_PER_SEQ = L_LAT // TM
GRID_W = 64
W = 256
HEAD = 64
N_HEADS = W // HEAD
LORA = 64
LORA_G = 128
DECAY_SCALE = math.exp(-0.5)
POOL_WINDOWS = (2, 4, 8, 16)
CHUNK = 128
N_EXPERTS = 16
FF = 1024
CAP_CTX = 2 * L_CTX // N_EXPERTS
CAP_LAT = 2 * L_LAT // N_EXPERTS
SLOTS = N_CTX_SEQ * CAP_CTX + N_LAT_SEQ * CAP_LAT
N_MOD = 6
NORM_EPS = 1e-6
GN_EPS = 64e-5
N_SMALL = 3 * W + 2 * LORA + 2 * LORA + LORA_G + 6 * W
LANE = 128
SUBLANE = 8
VMEM_LIMIT = 56 * 1024 * 1024


def _cparams(n_axes):
    return pltpu.CompilerParams(dimension_semantics=("arbitrary",) * n_axes,
                                vmem_limit_bytes=VMEM_LIMIT)


def _split2(x):
    hi = x.astype(BF16)
    lo = (x - hi.astype(F32)).astype(BF16)
    return hi, lo


def _split3(x):
    hi = x.astype(BF16)
    r = x - hi.astype(F32)
    mid = r.astype(BF16)
    lo = (r - mid.astype(F32)).astype(BF16)
    return hi, mid, lo


def _dot(a, b):
    return jnp.dot(a, b, preferred_element_type=F32)


def _sel_dot_l(m_bf16, x, parts=2):
    ps = _split2(x) if parts == 2 else _split3(x)
    acc = _dot(m_bf16, ps[0])
    for p in ps[1:]:
        acc = acc + _dot(m_bf16, p)
    return acc


def _sel_dot_r(x, m_bf16, parts=2):
    ps = _split2(x) if parts == 2 else _split3(x)
    acc = _dot(ps[0], m_bf16)
    for p in ps[1:]:
        acc = acc + _dot(p, m_bf16)
    return acc


def _dot_hl(a, b):
    ah, al = _split2(a)
    bh, bl = _split2(b)
    return _dot(ah, bh) + _dot(al, bh) + _dot(ah, bl)


def _sigmoid(x):
    return 1.0 / (1.0 + jnp.exp(-x))


def _gelu_tanh(x):
    return 0.5 * x * (1.0 + jnp.tanh(math.sqrt(2.0 / math.pi) * (x + 0.044715 * (x * x * x))))


def _norm_mod(x, g, scale, shift):
    ms = jnp.mean(x * x, axis=-1, keepdims=True)
    return (x * lax.rsqrt(ms + NORM_EPS) * g) * (1.0 + scale) + shift


def _mod_row(i):
    return jnp.where(i < N_CTX_BLK, 0, (i - N_CTX_BLK) // LAT_BLK_PER_SEQ + 1)


def _mod_kernel(c_ref, w_ref, b_ref, o_ref):
    c = c_ref[...]
    s = c * _sigmoid(c)
    o_ref[...] = _dot(s.astype(BF16), w_ref[...].astype(BF16)) + b_ref[...]


def _modulation(cond8, w_mod, b_mod):
    tn = 1536
    n = N_MOD * D
    return pl.pallas_call(
        _mod_kernel,
        grid=(DEPTH, n // tn),
        in_specs=[pl.BlockSpec((SUBLANE, D), lambda l, j: (0, 0)),
                  pl.BlockSpec((None, D, tn), lambda l, j: (l, 0, j)),
                  pl.BlockSpec((None, 1, tn), lambda l, j: (l, 0, j))],
        out_specs=pl.BlockSpec((None, SUBLANE, tn), lambda l, j: (l, 0, j)),
        out_shape=jax.ShapeDtypeStruct((DEPTH, SUBLANE, n), F32),
        compiler_params=_cparams(2),
        name="modulation",
    )(cond8, w_mod, b_mod.reshape(DEPTH, 1, n))


def _inproj_kernel(x_ref, mod_ref, g_ref, w_ref, rkv_ref, lora_ref, bcd_ref):
    h = _norm_mod(x_ref[...], g_ref[...], mod_ref[0, 1:2, :], mod_ref[0, 0:1, :])
    z = _dot(h.astype(BF16), w_ref[...])
    rkv_ref[...] = z[:, :3 * W]
    lora_ref[...] = z[:, 3 * W:3 * W + 3 * LANE]
    bcd_ref[...] = z[:, 3 * W + 3 * LANE:]


def _in_projection(x, mod_l, g1, w_small):
    return pl.pallas_call(
        _inproj_kernel,
        grid=(N_BLK,),
        in_specs=[pl.BlockSpec((TM, D), lambda i: (i, 0)),
                  pl.BlockSpec((1, N_MOD, D), lambda i: (_mod_row(i), 0, 0)),
                  pl.BlockSpec((1, D), lambda i: (0, 0)),
                  pl.BlockSpec((D, N_SMALL), lambda i: (0, 0))],
        out_specs=[pl.BlockSpec((TM, 3 * W), lambda i: (i, 0)),
                   pl.BlockSpec((TM, 3 * LANE), lambda i: (i, 0)),
                   pl.BlockSpec((TM, 6 * W), lambda i: (i, 0))],
        out_shape=[jax.ShapeDtypeStruct((T_ALL, 3 * W), F32),
                   jax.ShapeDtypeStruct((T_ALL, 3 * LANE), F32),
                   jax.ShapeDtypeStruct((T_ALL, 6 * W), F32)],
        compiler_params=_cparams(1),
        name="in_projection",
    )(x, mod_l, g1, w_small)


def _prep_kernel(rkv_ref, lora_ref, wup_ref, aup_ref, gup_ref, w0_ref, a0_ref, kk_k_ref, k_a_ref, r_k_ref,
                 ones4_ref, w_ref, b_ref, kin_ref, kk_ref, g_ref, bonus_ref):
    r = rkv_ref[:, 0:W]
    k = rkv_ref[:, W:2 * W]
    v = rkv_ref[:, 2 * W:3 * W]
    xw = lora_ref[:, 0:LANE]
    xa = lora_ref[:, LANE:2 * LANE]
    xg = lora_ref[:, 2 * LANE:3 * LANE]
    ones4 = ones4_ref[...]
    dec = w0_ref[...] + _dot_hl(jnp.tanh(xw), wup_ref[...])
    w_ref[...] = jnp.exp(-DECAY_SCALE * _sigmoid(dec))
    a = _sigmoid(a0_ref[...] + _dot_hl(xa, aup_ref[...]))
    g_ref[...] = _dot_hl(_sigmoid(xg), gup_ref[...])
    kk = k * kk_k_ref[...]
    n2 = _sel_dot_r(kk * kk, ones4, parts=3)
    kk = kk / jnp.maximum(jnp.sqrt(n2), 1e-12)
    kk_ref[...] = kk
    k_a = k_a_ref[...]
    kin0 = k * (1.0 + (a[:, 0:W] - 1.0) * k_a)
    kin1 = k * (1.0 + (a[:, W:2 * W] - 1.0) * k_a)
    kin_ref[:, 0:W] = kin0
    kin_ref[:, W:2 * W] = kin1
    b_ref[:, 0:W] = a[:, 0:W] * kk
    b_ref[:, W:2 * W] = a[:, W:2 * W] * kk
    r_k = r_k_ref[...]
    bonus = _sel_dot_r(r * kin0 * r_k, ones4, parts=3) + _sel_dot_r(r * kin1 * r_k, ones4, parts=3)
    bonus_ref[...] = bonus * v


def _rwkv_prep(rkv, lora, wup_bd, aup_bd, g_up, w0, a0, k_k, k_a, r_k, ones4):
    tok = lambda c: pl.BlockSpec((TM, c), lambda i: (i, 0))
    full = lambda a: pl.BlockSpec(a.shape, lambda i: (0,) * a.ndim)
    params = (wup_bd, aup_bd, g_up, w0, a0, k_k, k_a, r_k, ones4)
    return pl.pallas_call(
        _prep_kernel,
        grid=(N_BLK,),
        in_specs=[tok(3 * W), tok(3 * LANE)] + [full(a) for a in params],
        out_specs=[tok(2 * W), tok(2 * W), tok(2 * W), tok(W), tok(W), tok(W)],
        out_shape=[jax.ShapeDtypeStruct((T_ALL, c), F32) for c in (2 * W, 2 * W, 2 * W, W, W, W)],
        compiler_params=_cparams(1),
        name="rwkv_prep",
    )(rkv, lora, *params)


SEQ_PER_SCAN_BLK = 2
N_CHAIN = SEQ_PER_SCAN_BLK * 2 * 2
T_STEP = SUBLANE


def _scan_kernel(seq_len, r_ref, v_ref, kk_ref, w_ref, b_ref, kin_ref, s0_ref, ones2_ref, eye2_ref,
                 y_ref, sfin_ref, y0_scr, y1_scr):
    ones2 = ones2_ref[...]
    eye2 = eye2_ref[...]
    y_scr = (y0_scr, y1_scr)
    chains = [(b, d, p) for b in range(SEQ_PER_SCAN_BLK) for d in range(2) for p in range(2)]

    def seg_sum(x):
        return _sel_dot_r(x, ones2, parts=2)

    def body(tb, state):
        state = list(state)
        for b in range(SEQ_PER_SCAN_BLK):
            for d in range(2):
                t0 = tb * T_STEP if d == 0 else seq_len - T_STEP - tb * T_STEP
                base = pl.multiple_of(b * seq_len + t0, T_STEP)
                rows = pl.ds(base, T_STEP)
                r8, v8, kk8 = r_ref[rows, :], v_ref[rows, :], kk_ref[rows, :]
                w8 = w_ref[rows, d * W:(d + 1) * W]
                b8 = b_ref[rows, d * W:(d + 1) * W]
                k8 = kin_ref[rows, d * W:(d + 1) * W]
                y_rows = [[None, None] for _ in range(T_STEP)]
                for j in range(T_STEP):
                    jj = j if d == 0 else T_STEP - 1 - j
                    for p in range(2):
                        c = chains.index((b, d, p))
                        lanes = slice(p * LANE, (p + 1) * LANE)
                        row = lambda t: t[jj:jj + 1, lanes]
                        s = state[c]
                        s_kk = seg_sum(s * row(kk8))
                        v_col = seg_sum(eye2 * row(v8))
                        s = s * row(w8) - s_kk * row(b8) + v_col * row(k8)
                        state[c] = s
                        y_col = seg_sum(s * row(r8))
                        y_rows[jj][p] = jnp.sum(eye2 * y_col, axis=0, keepdims=True)
                y_tile = jnp.concatenate([jnp.concatenate(yr, axis=1) for yr in y_rows], axis=0)
                y_scr[d][rows, :] = y_tile
        return tuple(state)

    init = tuple(s0_ref[b, d, p] for (b, d, p) in chains)
    fin = lax.fori_loop(0, seq_len // T_STEP, body, init)
    for c, (b, d, p) in enumerate(chains):
        sfin_ref[b, d, p] = fin[c]
    y_ref[...] = y0_scr[...] + y1_scr[...]


def _rwkv_scan(seq_len, n_blk, blk0, rkv, kk, wdec, bvec, kin, s0, ones2, eye2):
    rows = SEQ_PER_SCAN_BLK * seq_len
    tok = lambda c, j: pl.BlockSpec((rows, c), lambda i: (i + blk0, j))
    st = pl.BlockSpec((None, SEQ_PER_SCAN_BLK, 2, 2, HEAD, LANE), lambda i: (i, 0, 0, 0, 0, 0))
    full = lambda a: pl.BlockSpec(a.shape, lambda i: (0,) * a.ndim)
    return pl.pallas_call(
        functools.partial(_scan_kernel, seq_len),
        grid=(n_blk,),
        in_specs=[tok(W, 0), tok(W, 2), tok(W, 0), tok(2 * W, 0), tok(2 * W, 0), tok(2 * W, 0), st,
                  full(ones2), full(eye2)],
        out_specs=[pl.BlockSpec((rows, W), lambda i: (i, 0)), st],
        out_shape=[jax.ShapeDtypeStruct((n_blk * rows, W), F32),
                   jax.ShapeDtypeStruct((n_blk, SEQ_PER_SCAN_BLK, 2, 2, HEAD, LANE), F32)],
        scratch_shapes=[pltpu.VMEM((rows, W), F32), pltpu.VMEM((rows, W), F32)],
        compiler_params=_cparams(1),
        name=f"rwkv_scan_{seq_len}",
    )(rkv, rkv, kk, wdec, bvec, kin, s0, ones2, eye2)


def _branch_kernel(y_ref, g_ref, bonus_ref, bcd_ref, lnx_g_ref, lnx_b_ref, band_ref, invcnt_ref, shift_ref,
                   wpool_ref, pscale_ref, conv_w_ref, conv_b_ref, sgu_g_ref, ws_ref, bs_ref, ones4_ref, br_ref):
    ones4 = ones4_ref[...]
    group = lax.broadcasted_iota(jnp.int32, (TM, W), 1) // HEAD

    def seg_mean(x):
        return _sel_dot_r(x, ones4, parts=3) * (1.0 / HEAD)

    y = y_ref[...]
    mu = seg_mean(y)
    yc = y - mu
    var = seg_mean(yc * yc)
    ya = yc * lax.rsqrt(var + GN_EPS) * lnx_g_ref[...] + lnx_b_ref[...] + bonus_ref[...]
    br_ref[:, 0:W] = (ya * g_ref[...]).astype(BF16)

    u = bcd_ref[:, 0:W]
    u_parts = _split3(u)
    win = jnp.zeros((TM, W), F32)
    for gi in range(len(POOL_WINDOWS)):
        band = band_ref[0, gi]
        s = _dot(band, u_parts[0]) + _dot(band, u_parts[1]) + _dot(band, u_parts[2])
        win = jnp.where(group == gi, s, win)
    pooled = win * invcnt_ref[0] - u
    br_ref[:, W:2 * W] = (_dot_hl(pooled, wpool_ref[...]) * pscale_ref[...]).astype(BF16)

    cin = bcd_ref[:, W:2 * W]
    cb = bcd_ref[:, 2 * W:3 * W]
    cc = bcd_ref[:, 3 * W:4 * W]
    t = cc * cin
    t_prev = _sel_dot_l(shift_ref[0, 0], t, parts=3)
    t_next = _sel_dot_l(shift_ref[0, 1], t, parts=3)
    conv = conv_b_ref[...] + t_prev * conv_w_ref[0:1, :]
    conv = conv + t * conv_w_ref[1:2, :]
    conv = conv + t_next * conv_w_ref[2:3, :]
    br_ref[:, 2 * W:3 * W] = (cb * conv).astype(BF16)

    du = _gelu_tanh(bcd_ref[:, 4 * W:5 * W])
    dv = _gelu_tanh(bcd_ref[:, 5 * W:6 * W])
    mu = seg_mean(dv)
    dc = dv - mu
    var = seg_mean(dc * dc)
    vn = dc * lax.rsqrt(var + NORM_EPS) * sgu_g_ref[...]
    group_c = lax.broadcasted_iota(jnp.int32, (CHUNK, W), 1) // HEAD
    for ch in range(TM // CHUNK):
        rows = slice(ch * CHUNK, (ch + 1) * CHUNK)
        vh, vl = _split2(vn[rows, :])
        s = jnp.zeros((CHUNK, W), F32)
        for gi in range(N_HEADS):
            wh, wl = _split2(ws_ref[gi])
            sg = _dot(wh, vh) + _dot(wl, vh) + _dot(wh, vl)
            s = jnp.where(group_c == gi, sg, s)
        br_ref[rows, 3 * W:4 * W] = (du[rows, :] * (s + bs_ref[...])).astype(BF16)


def _branches(y, g, bonus, bcd, p):
    tok = lambda c: pl.BlockSpec((TM, c), lambda i: (i, 0))
    full = lambda a: pl.BlockSpec(a.shape, lambda i: (0,) * a.ndim)
    lay = lambda a: pl.BlockSpec((1,) + a.shape[1:], lambda i: (jnp.where(i < N_CTX_BLK, 0, 1),) + (0,) * (a.ndim - 1))
    return pl.pallas_call(
        _branch_kernel,
        grid=(N_BLK,),
        in_specs=[tok(W), tok(W), tok(W), tok(6 * W), full(p['lnx_g']), full(p['lnx_b']),
                  lay(p['band']), lay(p['invcnt']), lay(p['shift']),
                  full(p['wpool_bd']), full(p['pool_scale']), full(p['conv_w']), full(p['conv_b']),
                  full(p['sgu_g']), full(p['w_s']), full(p['bs_full']), full(p['ones4'])],
        out_specs=tok(4 * W),
        out_shape=jax.ShapeDtypeStruct((T_ALL, 4 * W), BF16),
        compiler_params=_cparams(1),
        name="branches",
    )(y, g, bonus, bcd, p['lnx_g'], p['lnx_b'], p['band'], p['invcnt'], p['shift'], p['wpool_bd'],
      p['pool_scale'], p['conv_w'], p['conv_b'], p['sgu_g'], p['w_s'], p['bs_full'], p['ones4'])


def _merge_kernel(x_ref, mod_ref, g_ref, br_ref, wgl_ref, wbr_ref, wout_ref, o_ref):
    x = x_ref[...]
    h = _norm_mod(x, g_ref[...], mod_ref[0, 1:2, :], mod_ref[0, 0:1, :]).astype(BF16)
    merged = jnp.zeros((TM, D), F32)
    for i in range(4):
        gl = _dot(h, wgl_ref[:, i * D:(i + 1) * D])
        proj = _dot(br_ref[:, i * W:(i + 1) * W], wbr_ref[i])
        merged = merged + _sigmoid(gl) * proj
    mix = _dot(merged.astype(BF16), wout_ref[...])
    o_ref[...] = x + mod_ref[0, 2:3, :] * mix


def _merge(x, mod_l, g1, br, w_gl, w_branch, w_out):
    full = lambda a: pl.BlockSpec(a.shape, lambda i: (0,) * a.ndim)
    return pl.pallas_call(
        _merge_kernel,
        grid=(N_BLK,),
        in_specs=[pl.BlockSpec((TM, D), lambda i: (i, 0)),
                  pl.BlockSpec((1, N_MOD, D), lambda i: (_mod_row(i), 0, 0)),
                  pl.BlockSpec((1, D), lambda i: (0, 0)),
                  pl.BlockSpec((TM, 4 * W), lambda i: (i, 0)),
                  full(w_gl), full(w_branch), full(w_out)],
        out_specs=pl.BlockSpec((TM, D), lambda i: (i, 0)),
        out_shape=jax.ShapeDtypeStruct((T_ALL, D), F32),
        compiler_params=_cparams(1),
        name="merge",
    )(x, mod_l, g1, br, w_gl, w_branch, w_out)


def _router_kernel(x_ref, mod_ref, g_ref, wr_ref, h_ref, aff_ref):
    h = _norm_mod(x_ref[...], g_ref[...], mod_ref[0, 4:5, :], mod_ref[0, 3:4, :])
    h_ref[...] = h.astype(BF16)
    logits = _dot_hl(h, wr_ref[...])
    lane = lax.broadcasted_iota(jnp.int32, (TM, LANE), 1)
    logits = jnp.where(lane < N_EXPERTS, logits, -1e30)
    m = jnp.max(logits, axis=-1, keepdims=True)
    e = jnp.where(lane < N_EXPERTS, jnp.exp(logits - m), 0.0)
    aff_ref[...] = e / jnp.sum(e, axis=-1, keepdims=True)


def _router(x, mod_l, g2, w_router_pad):
    return pl.pallas_call(
        _router_kernel,
        grid=(N_BLK,),
        in_specs=[pl.BlockSpec((TM, D), lambda i: (i, 0)),
                  pl.BlockSpec((1, N_MOD, D), lambda i: (_mod_row(i), 0, 0)),
                  pl.BlockSpec((1, D), lambda i: (0, 0)),
                  pl.BlockSpec((D, LANE), lambda i: (0, 0))],
        out_specs=[pl.BlockSpec((TM, D), lambda i: (i, 0)), pl.BlockSpec((TM, LANE), lambda i: (i, 0))],
        out_shape=[jax.ShapeDtypeStruct((T_ALL, D), BF16), jax.ShapeDtypeStruct((T_ALL, LANE), F32)],
        compiler_params=_cparams(1),
        name="router",
    )(x, mod_l, g2, w_router_pad)


def _rank_kernel(seq_len, aff_ref, afft_ref, ones_ref, rank_ref):
    row0 = pl.program_id(1) * TM
    t_self = lax.broadcasted_iota(jnp.int32, (TM, seq_len), 0) + row0
    t_other = lax.broadcasted_iota(jnp.int32, (TM, seq_len), 1)
    earlier = t_other < t_self
    lane = lax.broadcasted_iota(jnp.int32, (TM, LANE), 1)
    aff = aff_ref[...]
    rank = jnp.zeros((TM, LANE), F32)
    for e in range(N_EXPERTS):
        mine = aff[:, e:e+1]
        other = afft_ref[e:e+1, :]
        beats = jnp.where(other > mine, 1.0, jnp.where(earlier, jnp.where(other == mine, 1.0, 0.0), 0.0))
        cnt = _dot(beats.astype(BF16), ones_ref[...])
        rank = jnp.where(lane == e, cnt, rank)
    rank_ref[...] = rank


def _ranks(seq_len, n_seq, blk0, aff, aff_t):
    nb = seq_len // TM
    ones = jnp.ones((seq_len, LANE), BF16)
    return pl.pallas_call(
        functools.partial(_rank_kernel, seq_len),
        grid=(n_seq, nb),
        in_specs=[pl.BlockSpec((TM, LANE), lambda s, j: (blk0 + s * nb + j, 0)),
                  pl.BlockSpec((None, N_EXPERTS, seq_len), lambda s, j: (s, 0, 0)),
                  pl.BlockSpec((seq_len, LANE), lambda s, j: (0, 0))],
        out_specs=pl.BlockSpec((TM, LANE), lambda s, j: (s * nb + j, 0)),
        out_shape=jax.ShapeDtypeStruct((n_seq * seq_len, LANE), F32),
        compiler_params=_cparams(2),
        name=f"ranks_{seq_len}",
    )(aff, aff_t, ones)


def _expert_kernel(h_ref, rank_ref, w1_ref, w3_ref, w2_ref, o_ref, xs_scr):
    def gather(seq_len, cap, tok0, slot0):
        slot = lax.broadcasted_iota(jnp.int32, (cap, seq_len), 0).astype(F32)
        rk = rank_ref[:, tok0:tok0 + seq_len]
        onehot = jnp.where(rk == slot, 1.0, 0.0).astype(BF16)
        xs_scr[slot0:slot0 + cap, :] = _dot(onehot, h_ref[tok0:tok0 + seq_len, :]).astype(BF16)

    for s in range(N_CTX_SEQ):
        gather(L_CTX, CAP_CTX, s * L_CTX, s * CAP_CTX)
    for s in range(N_LAT_SEQ):
        gather(L_LAT, CAP_LAT, T_CTX + s * L_LAT, N_CTX_SEQ * CAP_CTX + s * CAP_LAT)
    xs = xs_scr[...]
    a = _dot(xs, w1_ref[...])
    act = (a * _sigmoid(a)) * _dot(xs, w3_ref[...])
    o_ref[...] = _dot(act.astype(BF16), w2_ref[...])


def _experts(h2, rank_t, w1, w3, w2):
    wspec = lambda: pl.BlockSpec((None, D, FF), lambda e: (e, 0, 0))
    return pl.pallas_call(
        _expert_kernel,
        grid=(N_EXPERTS,),
        in_specs=[pl.BlockSpec((T_ALL, D), lambda e: (0, 0)),
                  pl.BlockSpec((None, 1, T_ALL), lambda e: (e, 0, 0)),
                  wspec(), wspec(), pl.BlockSpec((None, FF, D), lambda e: (e, 0, 0))],
        out_specs=pl.BlockSpec((None, SLOTS, D), lambda e: (e, 0, 0)),
        out_shape=jax.ShapeDtypeStruct((N_EXPERTS, SLOTS, D), F32),
        scratch_shapes=[pltpu.VMEM((SLOTS, D), BF16)],
        compiler_params=_cparams(1),
        name="experts",
    )(h2, rank_t, w1, w3, w2)


def _combine_kernel(cap, final, x_ref, mod_ref, aff_ref, rank_ref, oe_ref, fg_ref, o_ref):
    slot = lax.broadcasted_iota(jnp.int32, (TM, cap), 1).astype(F32)
    aff = aff_ref[...]
    rank = rank_ref[...]
    acc = jnp.zeros((TM, D), F32)
    for e in range(N_EXPERTS):
        onehot = jnp.where(rank[:, e:e+1] == slot, 1.0, 0.0).astype(BF16)
        acc = acc + aff[:, e:e+1] * _sel_dot_l(onehot, oe_ref[e], parts=2)
    x = x_ref[...] + mod_ref[0, 5:6, :] * acc
    if final:
        ms = jnp.mean(x * x, axis=-1, keepdims=True)
        x = x * lax.rsqrt(ms + NORM_EPS) * fg_ref[...]
    o_ref[...] = x


def _combine(seq_len, n_seq, blk0, cap, slot_blk0, final, x, mod_l, aff, rank, oe, final_g):
    nb = seq_len // TM
    tok = lambda c: pl.BlockSpec((TM, c), lambda s, j: (blk0 + s * nb + j, 0))
    return pl.pallas_call(
        functools.partial(_combine_kernel, cap, final),
        grid=(n_seq, nb),
        in_specs=[tok(D),
                  pl.BlockSpec((1, N_MOD, D), lambda s, j: (_mod_row(blk0 + s * nb + j), 0, 0)),
                  tok(LANE),
                  pl.BlockSpec((TM, LANE), lambda s, j: (s * nb + j, 0)),
                  pl.BlockSpec((N_EXPERTS, cap, D), lambda s, j: (0, slot_blk0 + s, 0)),
                  pl.BlockSpec((1, D), lambda s, j: (0, 0))],
        out_specs=pl.BlockSpec((TM, D), lambda s, j: (s * nb + j, 0)),
        out_shape=jax.ShapeDtypeStruct((n_seq * seq_len, D), F32),
        compiler_params=_cparams(2),
        name=f"combine_{seq_len}",
    )(x, mod_l, aff, rank, oe, final_g)


def _row_structure(row_len):
    t = np.arange(TM)
    same_row = (t[:, None] // row_len) == (t[None, :] // row_len)
    delta = t[None, :] - t[:, None]
    band = np.stack([same_row & (delta >= -(w // 2)) & (delta < w // 2) for w in POOL_WINDOWS]).astype(np.float32)
    cnt = band.sum(-1)
    invcnt = np.repeat((1.0 / cnt).T, HEAD, axis=1).astype(np.float32)
    shift = np.stack([same_row & (delta == -1), same_row & (delta == 1)]).astype(np.float32)
    return band, invcnt, shift


def _constants():
    lane = np.arange(W)
    ones4 = (lane[:, None] // HEAD == lane[None, :] // HEAD).astype(np.float32)
    lane = np.arange(LANE)
    ones2 = (lane[:, None] // HEAD == lane[None, :] // HEAD).astype(np.float32)
    eye2 = (np.arange(HEAD)[:, None] == lane[None, :] % HEAD).astype(np.float32)
    structs = [_row_structure(L_CTX), _row_structure(GRID_W)]
    return dict(
        ones4=jnp.asarray(ones4, BF16), ones2=jnp.asarray(ones2, BF16), eye2=jnp.asarray(eye2, F32),
        band=jnp.asarray(np.stack([s[0] for s in structs]), BF16),
        invcnt=jnp.asarray(np.stack([s[1] for s in structs]), F32),
        shift=jnp.asarray(np.stack([s[2] for s in structs]), BF16))


def _block_diag(blocks):
    n = len(blocks)
    rows = []
    for i, b in enumerate(blocks):
        rows.append(jnp.concatenate([b if j == i else jnp.zeros((b.shape[0], blocks[j].shape[1]), b.dtype)
                                     for j in range(n)], axis=1))
    return jnp.concatenate(rows, axis=0)


def _pair_tiles(s):
    lead = s.shape[:-3]
    s = s.reshape(lead + (2, 2, HEAD, HEAD))
    s = jnp.moveaxis(s, -3, -2)
    return s.reshape(lead + (2, HEAD, 2 * HEAD))


def _unpair_tiles(s):
    lead = s.shape[:-3]
    s = s.reshape(lead + (2, HEAD, 2, HEAD))
    s = jnp.moveaxis(s, -2, -3)
    return s.reshape(lead + (N_HEADS, HEAD, HEAD))


def kernel(x_prompt, x_sample, state_rwkv, c, c_ctx, norm1_g, w_mod, b_mod, w_in, w0, w_up, a0, a_up, g_up, k_k, k_a, r_k, lnx_g, lnx_b, w_pool, pool_scale, conv_w, conv_b, sgu_g, w_s, b_s, w_branch, w_out, norm2_g, w_router, w_e1, w_e3, w_e2, final_g):
    const = _constants()
    x = jnp.concatenate([x_prompt.reshape(T_CTX, D), x_sample.reshape(T_LAT, D)], axis=0)
    cond8 = jnp.concatenate([c_ctx[None], c, jnp.zeros((SUBLANE - 1 - N_LAT_SEQ, D), F32)], axis=0)
    mod = _modulation(cond8, w_mod, b_mod).reshape(DEPTH, SUBLANE, N_MOD, D)
    final_g2 = final_g.reshape(1, D)
    ctx_states = []
    y_ctx = y_lat = None
    for l in range(DEPTH):
        mod_l = mod[l]
        g1 = norm1_g[l].reshape(1, D)
        rkv, lora, bcd = _in_projection(x, mod_l, g1, w_in[l, :, :N_SMALL].astype(BF16))
        wup_bd = _block_diag([w_up[l, 0], w_up[l, 1]])
        aup_bd = _block_diag([a_up[l, 0], a_up[l, 1]])
        wdec, bvec, kin, kk, gate_a, bonus = _rwkv_prep(
            rkv, lora, wup_bd, aup_bd, g_up[l], w0[l].reshape(1, 2 * W), a0[l].reshape(1, 2 * W),
            k_k[l].reshape(1, W), k_a[l].reshape(1, W), r_k[l].reshape(1, W), const['ones4'])

        s0_ctx = jnp.zeros((N_CTX_SEQ // SEQ_PER_SCAN_BLK, SEQ_PER_SCAN_BLK, 2, 2, HEAD, LANE), F32)
        y_c, s_ctx = _rwkv_scan(L_CTX, N_CTX_SEQ // SEQ_PER_SCAN_BLK, 0, rkv, kk, wdec, bvec, kin, s0_ctx,
                                const['ones2'], const['eye2'])
        s0_lat = _pair_tiles(state_rwkv[:, l])[None]
        y_l, _ = _rwkv_scan(L_LAT, N_LAT_SEQ // SEQ_PER_SCAN_BLK, T_CTX // (SEQ_PER_SCAN_BLK * L_LAT), rkv, kk,
                            wdec, bvec, kin, s0_lat, const['ones2'], const['eye2'])
        ctx_states.append(_unpair_tiles(s_ctx).reshape(N_CTX_SEQ, 2, N_HEADS, HEAD, HEAD))
        y_scan = jnp.concatenate([y_c, y_l], axis=0)

        bs_full = jnp.repeat(b_s[l].T, HEAD, axis=1)
        bp = dict(lnx_g=lnx_g[l].reshape(1, W), lnx_b=lnx_b[l].reshape(1, W), band=const['band'],
                  invcnt=const['invcnt'], shift=const['shift'],
                  wpool_bd=_block_diag([w_pool[l, i] for i in range(len(POOL_WINDOWS))]),
                  pool_scale=pool_scale[l].reshape(1, W), conv_w=conv_w[l], conv_b=conv_b[l].reshape(1, W),
                  sgu_g=sgu_g[l].reshape(1, W), w_s=w_s[l], bs_full=bs_full, ones4=const['ones4'])
        br = _branches(y_scan, gate_a, bonus, bcd, bp)
        x = _merge(x, mod_l, g1, br, w_in[l, :, N_SMALL:].astype(BF16), w_branch[l].astype(BF16),
                   w_out[l].astype(BF16))

        wr_pad = jnp.concatenate([w_router[l], jnp.zeros((D, LANE - N_EXPERTS), F32)], axis=1)
        h2, aff = _router(x, mod_l, norm2_g[l].reshape(1, D), wr_pad)
        aff_t_ctx = jnp.swapaxes(aff[:T_CTX, :N_EXPERTS].reshape(N_CTX_SEQ, L_CTX, N_EXPERTS), 1, 2)
        aff_t_lat = jnp.swapaxes(aff[T_CTX:, :N_EXPERTS].reshape(N_LAT_SEQ, L_LAT, N_EXPERTS), 1, 2)
        rank_ctx = _ranks(L_CTX, N_CTX_SEQ, 0, aff, aff_t_ctx)
        rank_lat = _ranks(L_LAT, N_LAT_SEQ, N_CTX_BLK, aff, aff_t_lat)
        rank_t = jnp.concatenate([rank_ctx[:, :N_EXPERTS], rank_lat[:, :N_EXPERTS]], axis=0).T
        oe = _experts(h2, rank_t.reshape(N_EXPERTS, 1, T_ALL), w_e1[l].astype(BF16), w_e3[l].astype(BF16),
                      w_e2[l].astype(BF16))
        final = l == DEPTH - 1
        y_ctx = _combine(L_CTX, N_CTX_SEQ, 0, CAP_CTX, 0, final, x, mod_l, aff, rank_ctx, oe, final_g2)
        y_lat = _combine(L_LAT, N_LAT_SEQ, N_CTX_BLK, CAP_LAT, N_CTX_SEQ * CAP_CTX // CAP_LAT, final, x, mod_l,
                         aff, rank_lat, oe, final_g2)
        if not final:
            x = jnp.concatenate([y_ctx, y_lat], axis=0)
    y_prompt = y_ctx.reshape(N_CTX_SEQ, L_CTX, D)
    y_sample = y_lat.reshape(N_LAT_SEQ, L_LAT, D)
    new_state = jnp.stack(ctx_states, axis=1)
    return (y_prompt, y_sample, new_state)
```

```python
import functools
import math

import numpy as np
import jax
import jax.numpy as jnp
from jax import lax
from jax.experimental import pallas as pl
from jax.experimental.pallas import tpu as pltpu

F32 = jnp.float32
BF16 = jnp.bfloat16

D = 1024
N_CTX_SEQ, L_CTX = 16, 256
N_LAT_SEQ, L_LAT = 2, 1024
T_CTX = N_CTX_SEQ * L_CTX
T_LAT = N_LAT_SEQ * L_LAT
T_ALL = T_CTX + T_LAT
DEPTH = 2
TM = 256
N_BLK = T_ALL // TM
N_CTX_BLK = T_CTX // TM
LAT_BLK_PER_SEQ = L_LAT // TM
GRID_W = 64
W = 256
HEAD = 64
N_HEADS = W // HEAD
LORA = 64
LORA_G = 128
DECAY_SCALE = math.exp(-0.5)
POOL_WINDOWS = (2, 4, 8, 16)
CHUNK = 128
N_EXPERTS = 16
FF = 1024
CAP_CTX = 2 * L_CTX // N_EXPERTS
CAP_LAT = 2 * L_LAT // N_EXPERTS
SLOTS = N_CTX_SEQ * CAP_CTX + N_LAT_SEQ * CAP_LAT
N_MOD = 6
NORM_EPS = 1e-6
GN_EPS = 64e-5
N_SMALL = 3 * W + 2 * LORA + 2 * LORA + LORA_G + 6 * W
LANE = 128
SUBLANE = 8
VMEM_LIMIT = 56 * 1024 * 1024


def _cparams(n_axes):
    return pltpu.CompilerParams(dimension_semantics=("arbitrary",) * n_axes,
                                vmem_limit_bytes=VMEM_LIMIT)


def _split2(x):
    hi = x.astype(BF16)
    lo = (x - hi.astype(F32)).astype(BF16)
    return hi, lo


def _split3(x):
    hi = x.astype(BF16)
    r = x - hi.astype(F32)
    mid = r.astype(BF16)
    lo = (r - mid.astype(F32)).astype(BF16)
    return hi, mid, lo


def _dot(a, b):
    return jnp.dot(a, b, preferred_element_type=F32)


def _sel_dot_l(m_bf16, x, parts=2):
    ps = _split2(x) if parts == 2 else _split3(x)
    acc = _dot(m_bf16, ps[0])
    for p in ps[1:]:
        acc = acc + _dot(m_bf16, p)
    return acc


def _sel_dot_r(x, m_bf16, parts=2):
    ps = _split2(x) if parts == 2 else _split3(x)
    acc = _dot(ps[0], m_bf16)
    for p in ps[1:]:
        acc = acc + _dot(p, m_bf16)
    return acc


def _dot_hl(a, b):
    ah, al = _split2(a)
    bh, bl = _split2(b)
    return _dot(ah, bh) + _dot(al, bh) + _dot(ah, bl)


def _sigmoid(x):
    return 1.0 / (1.0 + jnp.exp(-x))


def _gelu_tanh(x):
    return 0.5 * x * (1.0 + jnp.tanh(math.sqrt(2.0 / math.pi) * (x + 0.044715 * (x * x * x))))


def _norm_mod(x, g, scale, shift):
    ms = jnp.mean(x * x, axis=-1, keepdims=True)
    return (x * lax.rsqrt(ms + NORM_EPS) * g) * (1.0 + scale) + shift


def _mod_row(i):
    return jnp.where(i < N_CTX_BLK, 0, (i - N_CTX_BLK) // LAT_BLK_PER_SEQ + 1)


def _mod_kernel(c_ref, w_ref, b_ref, o_ref):
    c = c_ref[...]
    s = c * _sigmoid(c)
    o_ref[...] = _dot(s.astype(BF16), w_ref[...].astype(BF16)) + b_ref[...]


def _modulation(cond8, w_mod, b_mod):
    tn = 1536
    n = N_MOD * D
    return pl.pallas_call(
        _mod_kernel,
        grid=(DEPTH, n // tn),
        in_specs=[pl.BlockSpec((SUBLANE, D), lambda l, j: (0, 0)),
                  pl.BlockSpec((None, D, tn), lambda l, j: (l, 0, j)),
                  pl.BlockSpec((None, 1, tn), lambda l, j: (l, 0, j))],
        out_specs=pl.BlockSpec((None, SUBLANE, tn), lambda l, j: (l, 0, j)),
        out_shape=jax.ShapeDtypeStruct((DEPTH, SUBLANE, n), F32),
        compiler_params=_cparams(2),
        name="modulation",
    )(cond8, w_mod, b_mod.reshape(DEPTH, 1, n))


def _inproj_kernel(x_ref, mod_ref, g_ref, w_ref, rkv_ref, lora_ref, bcd_ref):
    h = _norm_mod(x_ref[...], g_ref[...], mod_ref[0, 1:2, :], mod_ref[0, 0:1, :])
    z = _dot(h.astype(BF16), w_ref[...])
    rkv_ref[...] = z[:, :3 * W]
    lora_ref[...] = z[:, 3 * W:3 * W + 3 * LANE]
    bcd_ref[...] = z[:, 3 * W + 3 * LANE:]


def _in_projection(x, mod_l, g1, w_small):
    return pl.pallas_call(
        _inproj_kernel,
        grid=(N_BLK,),
        in_specs=[pl.BlockSpec((TM, D), lambda i: (i, 0)),
                  pl.BlockSpec((1, N_MOD, D), lambda i: (_mod_row(i), 0, 0)),
                  pl.BlockSpec((1, D), lambda i: (0, 0)),
                  pl.BlockSpec((D, N_SMALL), lambda i: (0, 0))],
        out_specs=[pl.BlockSpec((TM, 3 * W), lambda i: (i, 0)),
                   pl.BlockSpec((TM, 3 * LANE), lambda i: (i, 0)),
                   pl.BlockSpec((TM, 6 * W), lambda i: (i, 0))],
        out_shape=[jax.ShapeDtypeStruct((T_ALL, 3 * W), F32),
                   jax.ShapeDtypeStruct((T_ALL, 3 * LANE), F32),
                   jax.ShapeDtypeStruct((T_ALL, 6 * W), F32)],
        compiler_params=_cparams(1),
        name="in_projection",
    )(x, mod_l, g1, w_small)


def _prep_kernel(rkv_ref, lora_ref, wup_ref, aup_ref, gup_ref, w0_ref, a0_ref, kk_k_ref, k_a_ref, r_k_ref,
                 ones4_ref, w_ref, b_ref, kin_ref, kk_ref, g_ref, bonus_ref):
    r = rkv_ref[:, 0:W]
    k = rkv_ref[:, W:2 * W]
    v = rkv_ref[:, 2 * W:3 * W]
    xw = lora_ref[:, 0:LANE]
    xa = lora_ref[:, LANE:2 * LANE]
    xg = lora_ref[:, 2 * LANE:3 * LANE]
    ones4 = ones4_ref[...]
    dec = w0_ref[...] + _dot_hl(jnp.tanh(xw), wup_ref[...])
    w_ref[...] = jnp.exp(-DECAY_SCALE * _sigmoid(dec))
    a = _sigmoid(a0_ref[...] + _dot_hl(xa, aup_ref[...]))
    g_ref[...] = _dot_hl(_sigmoid(xg), gup_ref[...])
    kk = k * kk_k_ref[...]
    n2 = _sel_dot_r(kk * kk, ones4, parts=3)
    kk = kk / jnp.maximum(jnp.sqrt(n2), 1e-12)
    kk_ref[...] = kk
    k_a = k_a_ref[...]
    kin0 = k * (1.0 + (a[:, 0:W] - 1.0) * k_a)
    kin1 = k * (1.0 + (a[:, W:2 * W] - 1.0) * k_a)
    kin_ref[:, 0:W] = kin0
    kin_ref[:, W:2 * W] = kin1
    b_ref[:, 0:W] = a[:, 0:W] * kk
    b_ref[:, W:2 * W] = a[:, W:2 * W] * kk
    r_k = r_k_ref[...]
    bonus = _sel_dot_r(r * kin0 * r_k, ones4, parts=3) + _sel_dot_r(r * kin1 * r_k, ones4, parts=3)
    bonus_ref[...] = bonus * v


def _rwkv_prep(rkv, lora, wup_bd, aup_bd, g_up, w0, a0, k_k, k_a, r_k, ones4):
    tok = lambda c: pl.BlockSpec((TM, c), lambda i: (i, 0))
    full = lambda a: pl.BlockSpec(a.shape, lambda i: (0,) * a.ndim)
    params = (wup_bd, aup_bd, g_up, w0, a0, k_k, k_a, r_k, ones4)
    return pl.pallas_call(
        _prep_kernel,
        grid=(N_BLK,),
        in_specs=[tok(3 * W), tok(3 * LANE)] + [full(a) for a in params],
        out_specs=[tok(2 * W), tok(2 * W), tok(2 * W), tok(W), tok(W), tok(W)],
        out_shape=[jax.ShapeDtypeStruct((T_ALL, c), F32) for c in (2 * W, 2 * W, 2 * W, W, W, W)],
        compiler_params=_cparams(1),
        name="rwkv_prep",
    )(rkv, lora, *params)


SEQ_PER_SCAN_BLK = 2
N_CHAIN = SEQ_PER_SCAN_BLK * 2 * 2
T_STEP = SUBLANE
CHAINS_PER_DOT = 2


def _scan_kernel(seq_len, r_ref, v_ref, kk_ref, w_ref, b_ref, kin_ref, s0_ref, ones2_ref, eye2_ref,
                 y_ref, sfin_ref, y0_scr, y1_scr):
    ones2 = ones2_ref[...]
    eye2 = eye2_ref[...]
    y_scr = (y0_scr, y1_scr)
    chains = [(b, d, p) for b in range(SEQ_PER_SCAN_BLK) for d in range(2) for p in range(2)]

    def seg_sum(x):
        return _dot(jnp.concatenate(_split2(x), axis=1), ones2)

    groups = [chains[g:g + CHAINS_PER_DOT] for g in range(0, N_CHAIN, CHAINS_PER_DOT)]

    def body(tb, state):
        state = list(state)
        tiles = {}
        for b in range(SEQ_PER_SCAN_BLK):
            for d in range(2):
                t0 = tb * T_STEP if d == 0 else seq_len - T_STEP - tb * T_STEP
                rows = pl.ds(pl.multiple_of(b * seq_len + t0, T_STEP), T_STEP)
                tiles[(b, d)] = dict(rows=rows, r=r_ref[rows, :], v=v_ref[rows, :], kk=kk_ref[rows, :],
                                     w=w_ref[rows, d * W:(d + 1) * W], b=b_ref[rows, d * W:(d + 1) * W],
                                     k=kin_ref[rows, d * W:(d + 1) * W])

        def row(name, chain, j):
            b, d, p = chain
            jj = j if d == 0 else T_STEP - 1 - j
            return tiles[(b, d)][name][jj:jj + 1, p * LANE:(p + 1) * LANE]

        v_cols = seg_sum(jnp.concatenate(
            [eye2 * row('v', ch, j) for j in range(T_STEP) for ch in chains], axis=0))
        q = {}
        for j in range(T_STEP):
            for grp in groups:
                s_kk = seg_sum(jnp.concatenate(
                    [state[chains.index(ch)] * row('kk', ch, j) for ch in grp], axis=0))
                for i, ch in enumerate(grp):
                    c = chains.index(ch)
                    v_col = v_cols[(j * N_CHAIN + c) * HEAD:(j * N_CHAIN + c + 1) * HEAD]
                    s = (state[c] * row('w', ch, j) - s_kk[i * HEAD:(i + 1) * HEAD] * row('b', ch, j)
                         + v_col * row('k', ch, j))
                    state[c] = s
                    q[(c, j)] = s * row('r', ch, j)
        y_cols = seg_sum(jnp.concatenate(
            [q[(c, j)] for c in range(N_CHAIN) for j in range(T_STEP)], axis=0))
        for b in range(SEQ_PER_SCAN_BLK):
            for d in range(2):
                halves = []
                for p in range(2):
                    c = chains.index((b, d, p))
                    y_rows = [None] * T_STEP
                    for j in range(T_STEP):
                        blk = (c * T_STEP + j) * HEAD
                        jj = j if d == 0 else T_STEP - 1 - j
                        y_rows[jj] = jnp.sum(eye2 * y_cols[blk:blk + HEAD], axis=0, keepdims=True)
                    halves.append(jnp.concatenate(y_rows, axis=0))
                y_scr[d][tiles[(b, d)]['rows'], :] = jnp.concatenate(halves, axis=1)
        return tuple(state)

    init = tuple(s0_ref[b, d, p] for (b, d, p) in chains)
    fin = lax.fori_loop(0, seq_len // T_STEP, body, init)
    for c, (b, d, p) in enumerate(chains):
        sfin_ref[b, d, p] = fin[c]
    y_ref[...] = y0_scr[...] + y1_scr[...]


def _rwkv_scan(seq_len, n_blk, blk0, rkv, kk, wdec, bvec, kin, s0, ones2, eye2):
    rows = SEQ_PER_SCAN_BLK * seq_len
    tok = lambda c, j: pl.BlockSpec((rows, c), lambda i: (i + blk0, j))
    st = pl.BlockSpec((None, SEQ_PER_SCAN_BLK, 2, 2, HEAD, LANE), lambda i: (i, 0, 0, 0, 0, 0))
    full = lambda a: pl.BlockSpec(a.shape, lambda i: (0,) * a.ndim)
    return pl.pallas_call(
        functools.partial(_scan_kernel, seq_len),
        grid=(n_blk,),
        in_specs=[tok(W, 0), tok(W, 2), tok(W, 0), tok(2 * W, 0), tok(2 * W, 0), tok(2 * W, 0), st,
                  full(ones2), full(eye2)],
        out_specs=[pl.BlockSpec((rows, W), lambda i: (i, 0)), st],
        out_shape=[jax.ShapeDtypeStruct((n_blk * rows, W), F32),
                   jax.ShapeDtypeStruct((n_blk, SEQ_PER_SCAN_BLK, 2, 2, HEAD, LANE), F32)],
        scratch_shapes=[pltpu.VMEM((rows, W), F32), pltpu.VMEM((rows, W), F32)],
        compiler_params=_cparams(1),
        name=f"rwkv_scan_{seq_len}",
    )(rkv, rkv, kk, wdec, bvec, kin, s0, ones2, eye2)


def _branch_kernel(y_ref, g_ref, bonus_ref, bcd_ref, lnx_g_ref, lnx_b_ref, band_ref, invcnt_ref, shift_ref,
                   wpool_ref, pscale_ref, conv_w_ref, conv_b_ref, sgu_g_ref, ws_ref, bs_ref, ones4_ref, br_ref):
    ones4 = ones4_ref[...]
    group = lax.broadcasted_iota(jnp.int32, (TM, W), 1) // HEAD

    def seg_mean(x):
        return _sel_dot_r(x, ones4, parts=3) * (1.0 / HEAD)

    y = y_ref[...]
    mu = seg_mean(y)
    yc = y - mu
    var = seg_mean(yc * yc)
    ya = yc * lax.rsqrt(var + GN_EPS) * lnx_g_ref[...] + lnx_b_ref[...] + bonus_ref[...]
    br_ref[:, 0:W] = (ya * g_ref[...]).astype(BF16)

    u = bcd_ref[:, 0:W]
    u_parts = _split3(u)
    win = jnp.zeros((TM, W), F32)
    for gi in range(len(POOL_WINDOWS)):
        band = band_ref[0, gi]
        s = _dot(band, u_parts[0]) + _dot(band, u_parts[1]) + _dot(band, u_parts[2])
        win = jnp.where(group == gi, s, win)
    pooled = win * invcnt_ref[0] - u
    br_ref[:, W:2 * W] = (_dot_hl(pooled, wpool_ref[...]) * pscale_ref[...]).astype(BF16)

    cin = bcd_ref[:, W:2 * W]
    cb = bcd_ref[:, 2 * W:3 * W]
    cc = bcd_ref[:, 3 * W:4 * W]
    t = cc * cin
    t_prev = _sel_dot_l(shift_ref[0, 0], t, parts=3)
    t_next = _sel_dot_l(shift_ref[0, 1], t, parts=3)
    conv = conv_b_ref[...] + t_prev * conv_w_ref[0:1, :]
    conv = conv + t * conv_w_ref[1:2, :]
    conv = conv + t_next * conv_w_ref[2:3, :]
    br_ref[:, 2 * W:3 * W] = (cb * conv).astype(BF16)

    du = _gelu_tanh(bcd_ref[:, 4 * W:5 * W])
    dv = _gelu_tanh(bcd_ref[:, 5 * W:6 * W])
    mu = seg_mean(dv)
    dc = dv - mu
    var = seg_mean(dc * dc)
    vn = dc * lax.rsqrt(var + NORM_EPS) * sgu_g_ref[...]
    group_c = lax.broadcasted_iota(jnp.int32, (CHUNK, W), 1) // HEAD
    for ch in range(TM // CHUNK):
        rows = slice(ch * CHUNK, (ch + 1) * CHUNK)
        vh, vl = _split2(vn[rows, :])
        s = jnp.zeros((CHUNK, W), F32)
        for gi in range(N_HEADS):
            wh, wl = _split2(ws_ref[gi])
            sg = _dot(wh, vh) + _dot(wl, vh) + _dot(wh, vl)
            s = jnp.where(group_c == gi, sg, s)
        br_ref[rows, 3 * W:4 * W] = (du[rows, :] * (s + bs_ref[...])).astype(BF16)


def _branches(y, g, bonus, bcd, p):
    tok = lambda c: pl.BlockSpec((TM, c), lambda i: (i, 0))
    full = lambda a: pl.BlockSpec(a.shape, lambda i: (0,) * a.ndim)
    lay = lambda a: pl.BlockSpec((1,) + a.shape[1:], lambda i: (jnp.where(i < N_CTX_BLK, 0, 1),) + (0,) * (a.ndim - 1))
    return pl.pallas_call(
        _branch_kernel,
        grid=(N_BLK,),
        in_specs=[tok(W), tok(W), tok(W), tok(6 * W), full(p['lnx_g']), full(p['lnx_b']),
                  lay(p['band']), lay(p['invcnt']), lay(p['shift']),
                  full(p['wpool_bd']), full(p['pool_scale']), full(p['conv_w']), full(p['conv_b']),
                  full(p['sgu_g']), full(p['w_s']), full(p['bs_full']), full(p['ones4'])],
        out_specs=tok(4 * W),
        out_shape=jax.ShapeDtypeStruct((T_ALL, 4 * W), BF16),
        compiler_params=_cparams(1),
        name="branches",
    )(y, g, bonus, bcd, p['lnx_g'], p['lnx_b'], p['band'], p['invcnt'], p['shift'], p['wpool_bd'],
      p['pool_scale'], p['conv_w'], p['conv_b'], p['sgu_g'], p['w_s'], p['bs_full'], p['ones4'])


def _merge_kernel(x_ref, mod_ref, g_ref, br_ref, wgl_ref, wbr_ref, wout_ref, o_ref):
    x = x_ref[...]
    h = _norm_mod(x, g_ref[...], mod_ref[0, 1:2, :], mod_ref[0, 0:1, :]).astype(BF16)
    merged = jnp.zeros((TM, D), F32)
    for i in range(4):
        gl = _dot(h, wgl_ref[:, i * D:(i + 1) * D])
        proj = _dot(br_ref[:, i * W:(i + 1) * W], wbr_ref[i])
        merged = merged + _sigmoid(gl) * proj
    mix = _dot(merged.astype(BF16), wout_ref[...])
    o_ref[...] = x + mod_ref[0, 2:3, :] * mix


def _merge(x, mod_l, g1, br, w_gl, w_branch, w_out):
    full = lambda a: pl.BlockSpec(a.shape, lambda i: (0,) * a.ndim)
    return pl.pallas_call(
        _merge_kernel,
        grid=(N_BLK,),
        in_specs=[pl.BlockSpec((TM, D), lambda i: (i, 0)),
                  pl.BlockSpec((1, N_MOD, D), lambda i: (_mod_row(i), 0, 0)),
                  pl.BlockSpec((1, D), lambda i: (0, 0)),
                  pl.BlockSpec((TM, 4 * W), lambda i: (i, 0)),
                  full(w_gl), full(w_branch), full(w_out)],
        out_specs=pl.BlockSpec((TM, D), lambda i: (i, 0)),
        out_shape=jax.ShapeDtypeStruct((T_ALL, D), F32),
        compiler_params=_cparams(1),
        name="merge",
    )(x, mod_l, g1, br, w_gl, w_branch, w_out)


def _router_kernel(x_ref, mod_ref, g_ref, wr_ref, h_ref, aff_ref):
    h = _norm_mod(x_ref[...], g_ref[...], mod_ref[0, 4:5, :], mod_ref[0, 3:4, :])
    h_ref[...] = h.astype(BF16)
    logits = _dot_hl(h, wr_ref[...])
    lane = lax.broadcasted_iota(jnp.int32, (TM, LANE), 1)
    logits = jnp.where(lane < N_EXPERTS, logits, -1e30)
    m = jnp.max(logits, axis=-1, keepdims=True)
    e = jnp.where(lane < N_EXPERTS, jnp.exp(logits - m), 0.0)
    aff_ref[...] = e / jnp.sum(e, axis=-1, keepdims=True)


def _router(x, mod_l, g2, w_router_pad):
    return pl.pallas_call(
        _router_kernel,
        grid=(N_BLK,),
        in_specs=[pl.BlockSpec((TM, D), lambda i: (i, 0)),
                  pl.BlockSpec((1, N_MOD, D), lambda i: (_mod_row(i), 0, 0)),
                  pl.BlockSpec((1, D), lambda i: (0, 0)),
                  pl.BlockSpec((D, LANE), lambda i: (0, 0))],
        out_specs=[pl.BlockSpec((TM, D), lambda i: (i, 0)), pl.BlockSpec((TM, LANE), lambda i: (i, 0))],
        out_shape=[jax.ShapeDtypeStruct((T_ALL, D), BF16), jax.ShapeDtypeStruct((T_ALL, LANE), F32)],
        compiler_params=_cparams(1),
        name="router",
    )(x, mod_l, g2, w_router_pad)


def _rank_kernel(seq_len, aff_ref, afft_ref, ones_ref, rank_ref):
    row0 = pl.program_id(1) * TM
    t_self = lax.broadcasted_iota(jnp.int32, (TM, seq_len), 0) + row0
    t_other = lax.broadcasted_iota(jnp.int32, (TM, seq_len), 1)
    earlier = t_other < t_self
    lane = lax.broadcasted_iota(jnp.int32, (TM, LANE), 1)
    aff = aff_ref[...]
    rank = jnp.zeros((TM, LANE), F32)
    for e in range(N_EXPERTS):
        mine = aff[:, e:e+1]
        other = afft_ref[e:e+1, :]
        beats = jnp.where(other > mine, 1.0, jnp.where(earlier, jnp.where(other == mine, 1.0, 0.0), 0.0))
        cnt = _dot(beats.astype(BF16), ones_ref[...])
        rank = jnp.where(lane == e, cnt, rank)
    rank_ref[...] = rank


def _ranks(seq_len, n_seq, blk0, aff, aff_t):
    nb = seq_len // TM
    ones = jnp.ones((seq_len, LANE), BF16)
    return pl.pallas_call(
        functools.partial(_rank_kernel, seq_len),
        grid=(n_seq, nb),
        in_specs=[pl.BlockSpec((TM, LANE), lambda s, j: (blk0 + s * nb + j, 0)),
                  pl.BlockSpec((None, N_EXPERTS, seq_len), lambda s, j: (s, 0, 0)),
                  pl.BlockSpec((seq_len, LANE), lambda s, j: (0, 0))],
        out_specs=pl.BlockSpec((TM, LANE), lambda s, j: (s * nb + j, 0)),
        out_shape=jax.ShapeDtypeStruct((n_seq * seq_len, LANE), F32),
        compiler_params=_cparams(2),
        name=f"ranks_{seq_len}",
    )(aff, aff_t, ones)


def _expert_kernel(h_ref, rank_ref, w1_ref, w3_ref, w2_ref, o_ref, xs_scr):
    def gather(seq_len, cap, tok0, slot0):
        slot = lax.broadcasted_iota(jnp.int32, (cap, seq_len), 0).astype(F32)
        rk = rank_ref[:, tok0:tok0 + seq_len]
        onehot = jnp.where(rk == slot, 1.0, 0.0).astype(BF16)
        xs_scr[slot0:slot0 + cap, :] = _dot(onehot, h_ref[tok0:tok0 + seq_len, :]).astype(BF16)

    half = pl.program_id(1)

    @pl.when(half == 0)
    def _():
        for s in range(N_CTX_SEQ):
            gather(L_CTX, CAP_CTX, s * L_CTX, s * CAP_CTX)
        for s in range(N_LAT_SEQ):
            gather(L_LAT, CAP_LAT, T_CTX + s * L_LAT, N_CTX_SEQ * CAP_CTX + s * CAP_LAT)

    xs = xs_scr[...]
    a = _dot(xs, w1_ref[...].astype(BF16))
    act = (a * _sigmoid(a)) * _dot(xs, w3_ref[...].astype(BF16))
    out = _dot(act.astype(BF16), w2_ref[...].astype(BF16))

    @pl.when(half == 0)
    def _():
        o_ref[...] = out

    @pl.when(half != 0)
    def _():
        o_ref[...] += out


FF_SPLIT = 2


def _experts(layer, h2, rank_t, w1, w3, w2):
    fb = FF // FF_SPLIT
    wspec = lambda: pl.BlockSpec((None, None, D, fb), lambda e, f: (layer, e, 0, f))
    return pl.pallas_call(
        _expert_kernel,
        grid=(N_EXPERTS, FF_SPLIT),
        in_specs=[pl.BlockSpec((T_ALL, D), lambda e, f: (0, 0)),
                  pl.BlockSpec((None, 1, T_ALL), lambda e, f: (e, 0, 0)),
                  wspec(), wspec(), pl.BlockSpec((None, None, fb, D), lambda e, f: (layer, e, f, 0))],
        out_specs=pl.BlockSpec((None, SLOTS, D), lambda e, f: (e, 0, 0)),
        out_shape=jax.ShapeDtypeStruct((N_EXPERTS, SLOTS, D), F32),
        scratch_shapes=[pltpu.VMEM((SLOTS, D), BF16)],
        compiler_params=_cparams(2),
        name="experts",
    )(h2, rank_t, w1, w3, w2)


def _combine_kernel(cap, final, x_ref, mod_ref, aff_ref, rank_ref, oe_ref, fg_ref, o_ref):
    slot = lax.broadcasted_iota(jnp.int32, (TM, cap), 1).astype(F32)
    aff = aff_ref[...]
    rank = rank_ref[...]
    acc = jnp.zeros((TM, D), F32)
    for e in range(N_EXPERTS):
        onehot = jnp.where(rank[:, e:e+1] == slot, 1.0, 0.0).astype(BF16)
        acc = acc + aff[:, e:e+1] * _sel_dot_l(onehot, oe_ref[e], parts=2)
    x = x_ref[...] + mod_ref[0, 5:6, :] * acc
    if final:
        ms = jnp.mean(x * x, axis=-1, keepdims=True)
        x = x * lax.rsqrt(ms + NORM_EPS) * fg_ref[...]
    o_ref[...] = x


def _combine(seq_len, n_seq, blk0, cap, slot_blk0, final, x, mod_l, aff, rank, oe, final_g):
    nb = seq_len // TM
    tok = lambda c: pl.BlockSpec((TM, c), lambda s, j: (blk0 + s * nb + j, 0))
    return pl.pallas_call(
        functools.partial(_combine_kernel, cap, final),
        grid=(n_seq, nb),
        in_specs=[tok(D),
                  pl.BlockSpec((1, N_MOD, D), lambda s, j: (_mod_row(blk0 + s * nb + j), 0, 0)),
                  tok(LANE),
                  pl.BlockSpec((TM, LANE), lambda s, j: (s * nb + j, 0)),
                  pl.BlockSpec((N_EXPERTS, cap, D), lambda s, j: (0, slot_blk0 + s, 0)),
                  pl.BlockSpec((1, D), lambda s, j: (0, 0))],
        out_specs=pl.BlockSpec((TM, D), lambda s, j: (s * nb + j, 0)),
        out_shape=jax.ShapeDtypeStruct((n_seq * seq_len, D), F32),
        compiler_params=_cparams(2),
        name=f"combine_{seq_len}",
    )(x, mod_l, aff, rank, oe, final_g)


def _row_structure(row_len):
    t = np.arange(TM)
    same_row = (t[:, None] // row_len) == (t[None, :] // row_len)
    delta = t[None, :] - t[:, None]
    band = np.stack([same_row & (delta >= -(w // 2)) & (delta < w // 2) for w in POOL_WINDOWS]).astype(np.float32)
    cnt = band.sum(-1)
    invcnt = np.repeat((1.0 / cnt).T, HEAD, axis=1).astype(np.float32)
    shift = np.stack([same_row & (delta == -1), same_row & (delta == 1)]).astype(np.float32)
    return band, invcnt, shift


def _constants():
    lane = np.arange(W)
    ones4 = (lane[:, None] // HEAD == lane[None, :] // HEAD).astype(np.float32)
    lane = np.arange(LANE)
    ones2 = (lane[:, None] // HEAD == lane[None, :] // HEAD).astype(np.float32)
    ones2 = np.concatenate([ones2, ones2], axis=0)
    eye2 = (np.arange(HEAD)[:, None] == lane[None, :] % HEAD).astype(np.float32)
    structs = [_row_structure(L_CTX), _row_structure(GRID_W)]
    return dict(
        ones4=jnp.asarray(ones4, BF16), ones2=jnp.asarray(ones2, BF16), eye2=jnp.asarray(eye2, F32),
        band=jnp.asarray(np.stack([s[0] for s in structs]), BF16),
        invcnt=jnp.asarray(np.stack([s[1] for s in structs]), F32),
        shift=jnp.asarray(np.stack([s[2] for s in structs]), BF16))


def _block_diag(blocks):
    n = len(blocks)
    rows = []
    for i, b in enumerate(blocks):
        rows.append(jnp.concatenate([b if j == i else jnp.zeros((b.shape[0], blocks[j].shape[1]), b.dtype)
                                     for j in range(n)], axis=1))
    return jnp.concatenate(rows, axis=0)


def _pair_tiles(s):
    lead = s.shape[:-3]
    s = s.reshape(lead + (2, 2, HEAD, HEAD))
    s = jnp.moveaxis(s, -3, -2)
    return s.reshape(lead + (2, HEAD, 2 * HEAD))


def _unpair_tiles(s):
    lead = s.shape[:-3]
    s = s.reshape(lead + (2, HEAD, 2, HEAD))
    s = jnp.moveaxis(s, -2, -3)
    return s.reshape(lead + (N_HEADS, HEAD, HEAD))


def kernel(x_prompt, x_sample, state_rwkv, c, c_ctx, norm1_g, w_mod, b_mod, w_in, w0, w_up, a0, a_up, g_up, k_k, k_a, r_k, lnx_g, lnx_b, w_pool, pool_scale, conv_w, conv_b, sgu_g, w_s, b_s, w_branch, w_out, norm2_g, w_router, w_e1, w_e3, w_e2, final_g):
    const = _constants()
    x = jnp.concatenate([x_prompt.reshape(T_CTX, D), x_sample.reshape(T_LAT, D)], axis=0)
    cond8 = jnp.concatenate([c_ctx[None], c, jnp.zeros((SUBLANE - 1 - N_LAT_SEQ, D), F32)], axis=0)
    mod = _modulation(cond8, w_mod, b_mod).reshape(DEPTH, SUBLANE, N_MOD, D)
    final_g2 = final_g.reshape(1, D)
    ctx_states = []
    y_ctx = y_lat = None
    for l in range(DEPTH):
        mod_l = mod[l]
        g1 = norm1_g[l].reshape(1, D)
        rkv, lora, bcd = _in_projection(x, mod_l, g1, w_in[l, :, :N_SMALL].astype(BF16))
        wup_bd = _block_diag([w_up[l, 0], w_up[l, 1]])
        aup_bd = _block_diag([a_up[l, 0], a_up[l, 1]])
        wdec, bvec, kin, kk, gate_a, bonus = _rwkv_prep(
            rkv, lora, wup_bd, aup_bd, g_up[l], w0[l].reshape(1, 2 * W), a0[l].reshape(1, 2 * W),
            k_k[l].reshape(1, W), k_a[l].reshape(1, W), r_k[l].reshape(1, W), const['ones4'])

        s0_ctx = jnp.zeros((N_CTX_SEQ // SEQ_PER_SCAN_BLK, SEQ_PER_SCAN_BLK, 2, 2, HEAD, LANE), F32)
        y_c, s_ctx = _rwkv_scan(L_CTX, N_CTX_SEQ // SEQ_PER_SCAN_BLK, 0, rkv, kk, wdec, bvec, kin, s0_ctx,
                                const['ones2'], const['eye2'])
        s0_lat = _pair_tiles(state_rwkv[:, l])[None]
        y_l, _ = _rwkv_scan(L_LAT, N_LAT_SEQ // SEQ_PER_SCAN_BLK, T_CTX // (SEQ_PER_SCAN_BLK * L_LAT), rkv, kk,
                            wdec, bvec, kin, s0_lat, const['ones2'], const['eye2'])
        ctx_states.append(_unpair_tiles(s_ctx).reshape(N_CTX_SEQ, 2, N_HEADS, HEAD, HEAD))
        y_scan = jnp.concatenate([y_c, y_l], axis=0)

        bs_full = jnp.repeat(b_s[l].T, HEAD, axis=1)
        bp = dict(lnx_g=lnx_g[l].reshape(1, W), lnx_b=lnx_b[l].reshape(1, W), band=const['band'],
                  invcnt=const['invcnt'], shift=const['shift'],
                  wpool_bd=_block_diag([w_pool[l, i] for i in range(len(POOL_WINDOWS))]),
                  pool_scale=pool_scale[l].reshape(1, W), conv_w=conv_w[l], conv_b=conv_b[l].reshape(1, W),
                  sgu_g=sgu_g[l].reshape(1, W), w_s=w_s[l], bs_full=bs_full, ones4=const['ones4'])
        br = _branches(y_scan, gate_a, bonus, bcd, bp)
        x = _merge(x, mod_l, g1, br, w_in[l, :, N_SMALL:].astype(BF16), w_branch[l].astype(BF16),
                   w_out[l].astype(BF16))

        wr_pad = jnp.concatenate([w_router[l], jnp.zeros((D, LANE - N_EXPERTS), F32)], axis=1)
        h2, aff = _router(x, mod_l, norm2_g[l].reshape(1, D), wr_pad)
        aff_t_ctx = jnp.swapaxes(aff[:T_CTX, :N_EXPERTS].reshape(N_CTX_SEQ, L_CTX, N_EXPERTS), 1, 2)
        aff_t_lat = jnp.swapaxes(aff[T_CTX:, :N_EXPERTS].reshape(N_LAT_SEQ, L_LAT, N_EXPERTS), 1, 2)
        rank_ctx = _ranks(L_CTX, N_CTX_SEQ, 0, aff, aff_t_ctx)
        rank_lat = _ranks(L_LAT, N_LAT_SEQ, N_CTX_BLK, aff, aff_t_lat)
        rank_t = jnp.concatenate([rank_ctx[:, :N_EXPERTS], rank_lat[:, :N_EXPERTS]], axis=0).T
        oe = _experts(l, h2, rank_t.reshape(N_EXPERTS, 1, T_ALL), w_e1, w_e3, w_e2)
        final = l == DEPTH - 1
        y_ctx = _combine(L_CTX, N_CTX_SEQ, 0, CAP_CTX, 0, final, x, mod_l, aff, rank_ctx, oe, final_g2)
        y_lat = _combine(L_LAT, N_LAT_SEQ, N_CTX_BLK, CAP_LAT, N_CTX_SEQ * CAP_CTX // CAP_LAT, final, x, mod_l,
                         aff, rank_lat, oe, final_g2)
        if not final:
            x = jnp.concatenate([y_ctx, y_lat], axis=0)
    y_prompt = y_ctx.reshape(N_CTX_SEQ, L_CTX, D)
    y_sample = y_lat.reshape(N_LAT_SEQ, L_LAT, D)
    new_state = jnp.stack(ctx_states, axis=1)
    return (y_prompt, y_sample, new_state)
```

```python
import functools
import math

import numpy as np
import jax
import jax.numpy as jnp
from jax import lax
from jax.experimental import pallas as pl
from jax.experimental.pallas import tpu as pltpu

F32 = jnp.float32
BF16 = jnp.bfloat16

D = 1024
N_CTX_SEQ, L_CTX = 16, 256
N_LAT_SEQ, L_LAT = 2, 1024
T_CTX = N_CTX_SEQ * L_CTX
T_LAT = N_LAT_SEQ * L_LAT
T_ALL = T_CTX + T_LAT
DEPTH = 2
TM = 256
N_BLK = T_ALL // TM
N_CTX_BLK = T_CTX // TM
LAT_BLK_PER_SEQ = L_LAT // TM
GRID_W = 64
W = 256
HEAD = 64
N_HEADS = W // HEAD
LORA = 64
LORA_G = 128
DECAY_SCALE = math.exp(-0.5)
POOL_WINDOWS = (2, 4, 8, 16)
CHUNK = 128
N_EXPERTS = 16
FF = 1024
CAP_CTX = 2 * L_CTX // N_EXPERTS
CAP_LAT = 2 * L_LAT // N_EXPERTS
SLOTS = N_CTX_SEQ * CAP_CTX + N_LAT_SEQ * CAP_LAT
N_MOD = 6
NORM_EPS = 1e-6
GN_EPS = 64e-5
N_SMALL = 3 * W + 2 * LORA + 2 * LORA + LORA_G + 6 * W
LANE = 128
SUBLANE = 8
VMEM_LIMIT = 56 * 1024 * 1024


def _cparams(n_axes):
    return pltpu.CompilerParams(dimension_semantics=("arbitrary",) * n_axes,
                                vmem_limit_bytes=VMEM_LIMIT)


def _split2(x):
    hi = x.astype(BF16)
    lo = (x - hi.astype(F32)).astype(BF16)
    return hi, lo


def _split3(x):
    hi = x.astype(BF16)
    r = x - hi.astype(F32)
    mid = r.astype(BF16)
    lo = (r - mid.astype(F32)).astype(BF16)
    return hi, mid, lo


def _dot(a, b):
    return jnp.dot(a, b, preferred_element_type=F32)


def _sel_dot_l(m_bf16, x, parts=2):
    ps = _split2(x) if parts == 2 else _split3(x)
    acc = _dot(m_bf16, ps[0])
    for p in ps[1:]:
        acc = acc + _dot(m_bf16, p)
    return acc


def _sel_dot_r(x, m_bf16, parts=2):
    ps = _split2(x) if parts == 2 else _split3(x)
    acc = _dot(ps[0], m_bf16)
    for p in ps[1:]:
        acc = acc + _dot(p, m_bf16)
    return acc


def _dot_hl(a, b):
    ah, al = _split2(a)
    bh, bl = _split2(b)
    return _dot(ah, bh) + _dot(al, bh) + _dot(ah, bl)


def _sigmoid(x):
    return 1.0 / (1.0 + jnp.exp(-x))


def _gelu_tanh(x):
    return 0.5 * x * (1.0 + jnp.tanh(math.sqrt(2.0 / math.pi) * (x + 0.044715 * (x * x * x))))


def _norm_mod(x, g, scale, shift):
    ms = jnp.mean(x * x, axis=-1, keepdims=True)
    return (x * lax.rsqrt(ms + NORM_EPS) * g) * (1.0 + scale) + shift


def _mod_row(i):
    return jnp.where(i < N_CTX_BLK, 0, (i - N_CTX_BLK) // LAT_BLK_PER_SEQ + 1)


def _mod_kernel(c_ref, w_ref, b_ref, o_ref):
    c = c_ref[...]
    s = c * _sigmoid(c)
    o_ref[...] = _dot(s.astype(BF16), w_ref[...].astype(BF16)) + b_ref[...]


def _modulation(cond8, w_mod, b_mod):
    tn = 1536
    n = N_MOD * D
    return pl.pallas_call(
        _mod_kernel,
        grid=(DEPTH, n // tn),
        in_specs=[pl.BlockSpec((SUBLANE, D), lambda l, j: (0, 0)),
                  pl.BlockSpec((None, D, tn), lambda l, j: (l, 0, j)),
                  pl.BlockSpec((None, 1, tn), lambda l, j: (l, 0, j))],
        out_specs=pl.BlockSpec((None, SUBLANE, tn), lambda l, j: (l, 0, j)),
        out_shape=jax.ShapeDtypeStruct((DEPTH, SUBLANE, n), F32),
        compiler_params=_cparams(2),
        name="modulation",
    )(cond8, w_mod, b_mod.reshape(DEPTH, 1, n))


def _inproj_kernel(x_ref, mod_ref, g_ref, w_ref, rkv_ref, lora_ref, bcd_ref):
    h = _norm_mod(x_ref[...], g_ref[...], mod_ref[0, 1:2, :], mod_ref[0, 0:1, :])
    z = _dot(h.astype(BF16), w_ref[...])
    rkv_ref[...] = z[:, :3 * W]
    lora_ref[...] = z[:, 3 * W:3 * W + 3 * LANE]
    bcd_ref[...] = z[:, 3 * W + 3 * LANE:]


def _in_projection(x, mod_l, g1, w_small):
    return pl.pallas_call(
        _inproj_kernel,
        grid=(N_BLK,),
        in_specs=[pl.BlockSpec((TM, D), lambda i: (i, 0)),
                  pl.BlockSpec((1, N_MOD, D), lambda i: (_mod_row(i), 0, 0)),
                  pl.BlockSpec((1, D), lambda i: (0, 0)),
                  pl.BlockSpec((D, N_SMALL), lambda i: (0, 0))],
        out_specs=[pl.BlockSpec((TM, 3 * W), lambda i: (i, 0)),
                   pl.BlockSpec((TM, 3 * LANE), lambda i: (i, 0)),
                   pl.BlockSpec((TM, 6 * W), lambda i: (i, 0))],
        out_shape=[jax.ShapeDtypeStruct((T_ALL, 3 * W), F32),
                   jax.ShapeDtypeStruct((T_ALL, 3 * LANE), F32),
                   jax.ShapeDtypeStruct((T_ALL, 6 * W), F32)],
        compiler_params=_cparams(1),
        name="in_projection",
    )(x, mod_l, g1, w_small)


def _prep_kernel(rkv_ref, lora_ref, wup_ref, aup_ref, gup_ref, w0_ref, a0_ref, kk_k_ref, k_a_ref, r_k_ref,
                 ones4_ref, w_ref, b_ref, kin_ref, kk_ref, g_ref, bonus_ref):
    r = rkv_ref[:, 0:W]
    k = rkv_ref[:, W:2 * W]
    v = rkv_ref[:, 2 * W:3 * W]
    xw = lora_ref[:, 0:LANE]
    xa = lora_ref[:, LANE:2 * LANE]
    xg = lora_ref[:, 2 * LANE:3 * LANE]
    ones4 = ones4_ref[...]
    dec = w0_ref[...] + _dot_hl(jnp.tanh(xw), wup_ref[...])
    w_ref[...] = jnp.exp(-DECAY_SCALE * _sigmoid(dec))
    a = _sigmoid(a0_ref[...] + _dot_hl(xa, aup_ref[...]))
    g_ref[...] = _dot_hl(_sigmoid(xg), gup_ref[...])
    kk = k * kk_k_ref[...]
    n2 = _sel_dot_r(kk * kk, ones4, parts=3)
    kk = kk / jnp.maximum(jnp.sqrt(n2), 1e-12)
    kk_ref[...] = kk
    k_a = k_a_ref[...]
    kin0 = k * (1.0 + (a[:, 0:W] - 1.0) * k_a)
    kin1 = k * (1.0 + (a[:, W:2 * W] - 1.0) * k_a)
    kin_ref[:, 0:W] = kin0
    kin_ref[:, W:2 * W] = kin1
    b_ref[:, 0:W] = a[:, 0:W] * kk
    b_ref[:, W:2 * W] = a[:, W:2 * W] * kk
    r_k = r_k_ref[...]
    bonus = _sel_dot_r(r * kin0 * r_k, ones4, parts=3) + _sel_dot_r(r * kin1 * r_k, ones4, parts=3)
    bonus_ref[...] = bonus * v


def _rwkv_prep(rkv, lora, wup_bd, aup_bd, g_up, w0, a0, k_k, k_a, r_k, ones4):
    tok = lambda c: pl.BlockSpec((TM, c), lambda i: (i, 0))
    full = lambda a: pl.BlockSpec(a.shape, lambda i: (0,) * a.ndim)
    params = (wup_bd, aup_bd, g_up, w0, a0, k_k, k_a, r_k, ones4)
    return pl.pallas_call(
        _prep_kernel,
        grid=(N_BLK,),
        in_specs=[tok(3 * W), tok(3 * LANE)] + [full(a) for a in params],
        out_specs=[tok(2 * W), tok(2 * W), tok(2 * W), tok(W), tok(W), tok(W)],
        out_shape=[jax.ShapeDtypeStruct((T_ALL, c), F32) for c in (2 * W, 2 * W, 2 * W, W, W, W)],
        compiler_params=_cparams(1),
        name="rwkv_prep",
    )(rkv, lora, *params)


T_STEP = SUBLANE
CHAINS_PER_DOT = 2
CTX_SEQ_PER_SCAN_BLK = 4


def _scan_kernel(n_seq, seq_len, r_ref, v_ref, kk_ref, w_ref, b_ref, kin_ref, s0_ref, ones4_ref, eye4_ref,
                 y_ref, sfin_ref, y0_scr, y1_scr):
    ones4 = ones4_ref[...]
    eye4 = eye4_ref[...]
    y_scr = (y0_scr, y1_scr)
    chains = [(b, d) for b in range(n_seq) for d in range(2)]
    n_chain = len(chains)
    per_dot = CHAINS_PER_DOT if n_chain > 2 * CHAINS_PER_DOT else n_chain
    groups = [chains[g:g + per_dot] for g in range(0, n_chain, per_dot)]

    def row_sum(x):
        return _dot(x.astype(BF16), ones4)

    def body(tb, state):
        state = list(state)
        tiles = {}
        for b, d in chains:
            t0 = tb * T_STEP if d == 0 else seq_len - T_STEP - tb * T_STEP
            rows = pl.ds(pl.multiple_of(b * seq_len + t0, T_STEP), T_STEP)
            tiles[(b, d)] = dict(rows=rows, r=r_ref[rows, :], v=v_ref[rows, :], kk=kk_ref[rows, :],
                                 w=w_ref[rows, d * W:(d + 1) * W], b=b_ref[rows, d * W:(d + 1) * W],
                                 k=kin_ref[rows, d * W:(d + 1) * W])

        def row(name, chain, j):
            jj = j if chain[1] == 0 else T_STEP - 1 - j
            return tiles[chain][name][jj:jj + 1, :]

        y_rows = [[None] * T_STEP for _ in chains]
        for j in range(T_STEP):
            v_cols = row_sum(jnp.concatenate([eye4 * row('v', ch, j) for ch in chains], axis=0))
            q = [None] * n_chain
            for grp in groups:
                s_kk = row_sum(jnp.concatenate(
                    [state[chains.index(ch)] * row('kk', ch, j) for ch in grp], axis=0))
                for i, ch in enumerate(grp):
                    c = chains.index(ch)
                    s = (state[c] * row('w', ch, j) - s_kk[i * HEAD:(i + 1) * HEAD] * row('b', ch, j)
                         + v_cols[c * HEAD:(c + 1) * HEAD] * row('k', ch, j))
                    state[c] = s
                    q[c] = s * row('r', ch, j)
            y_cols = row_sum(jnp.concatenate(q, axis=0))
            for c, (b, d) in enumerate(chains):
                jj = j if d == 0 else T_STEP - 1 - j
                y_rows[c][jj] = jnp.sum(eye4 * y_cols[c * HEAD:(c + 1) * HEAD], axis=0, keepdims=True)
        for c, (b, d) in enumerate(chains):
            y_scr[d][tiles[(b, d)]['rows'], :] = jnp.concatenate(y_rows[c], axis=0)
        return tuple(state)

    init = tuple(s0_ref[b, d] for (b, d) in chains)
    fin = lax.fori_loop(0, seq_len // T_STEP, body, init)
    for c, (b, d) in enumerate(chains):
        sfin_ref[b, d] = fin[c]
    y_ref[...] = y0_scr[...] + y1_scr[...]


def _rwkv_scan(n_seq, seq_len, n_blk, blk0, rkv, kk, wdec, bvec, kin, s0, ones4, eye4):
    rows = n_seq * seq_len
    tok = lambda c, j: pl.BlockSpec((rows, c), lambda i: (i + blk0, j))
    st = pl.BlockSpec((None, n_seq, 2, HEAD, W), lambda i: (i, 0, 0, 0, 0))
    full = lambda a: pl.BlockSpec(a.shape, lambda i: (0,) * a.ndim)
    return pl.pallas_call(
        functools.partial(_scan_kernel, n_seq, seq_len),
        grid=(n_blk,),
        in_specs=[tok(W, 0), tok(W, 2), tok(W, 0), tok(2 * W, 0), tok(2 * W, 0), tok(2 * W, 0), st,
                  full(ones4), full(eye4)],
        out_specs=[pl.BlockSpec((rows, W), lambda i: (i, 0)), st],
        out_shape=[jax.ShapeDtypeStruct((n_blk * rows, W), F32),
                   jax.ShapeDtypeStruct((n_blk, n_seq, 2, HEAD, W), F32)],
        scratch_shapes=[pltpu.VMEM((rows, W), F32), pltpu.VMEM((rows, W), F32)],
        compiler_params=_cparams(1),
        name=f"rwkv_scan_{seq_len}",
    )(rkv, rkv, kk, wdec, bvec, kin, s0, ones4, eye4)


def _branch_kernel(y_ref, g_ref, bonus_ref, bcd_ref, lnx_g_ref, lnx_b_ref, band_ref, invcnt_ref, shift_ref,
                   wpool_ref, pscale_ref, conv_w_ref, conv_b_ref, sgu_g_ref, ws_ref, bs_ref, ones4_ref, br_ref):
    ones4 = ones4_ref[...]
    group = lax.broadcasted_iota(jnp.int32, (TM, W), 1) // HEAD

    def seg_mean(x):
        return _sel_dot_r(x, ones4, parts=3) * (1.0 / HEAD)

    y = y_ref[...]
    mu = seg_mean(y)
    yc = y - mu
    var = seg_mean(yc * yc)
    ya = yc * lax.rsqrt(var + GN_EPS) * lnx_g_ref[...] + lnx_b_ref[...] + bonus_ref[...]
    br_ref[:, 0:W] = (ya * g_ref[...]).astype(BF16)

    u = bcd_ref[:, 0:W]
    u_parts = _split3(u)
    win = jnp.zeros((TM, W), F32)
    for gi in range(len(POOL_WINDOWS)):
        band = band_ref[0, gi]
        s = _dot(band, u_parts[0]) + _dot(band, u_parts[1]) + _dot(band, u_parts[2])
        win = jnp.where(group == gi, s, win)
    pooled = win * invcnt_ref[0] - u
    br_ref[:, W:2 * W] = (_dot_hl(pooled, wpool_ref[...]) * pscale_ref[...]).astype(BF16)

    cin = bcd_ref[:, W:2 * W]
    cb = bcd_ref[:, 2 * W:3 * W]
    cc = bcd_ref[:, 3 * W:4 * W]
    t = cc * cin
    t_prev = _sel_dot_l(shift_ref[0, 0], t, parts=3)
    t_next = _sel_dot_l(shift_ref[0, 1], t, parts=3)
    conv = conv_b_ref[...] + t_prev * conv_w_ref[0:1, :]
    conv = conv + t * conv_w_ref[1:2, :]
    conv = conv + t_next * conv_w_ref[2:3, :]
    br_ref[:, 2 * W:3 * W] = (cb * conv).astype(BF16)

    du = _gelu_tanh(bcd_ref[:, 4 * W:5 * W])
    dv = _gelu_tanh(bcd_ref[:, 5 * W:6 * W])
    mu = seg_mean(dv)
    dc = dv - mu
    var = seg_mean(dc * dc)
    vn = dc * lax.rsqrt(var + NORM_EPS) * sgu_g_ref[...]
    group_c = lax.broadcasted_iota(jnp.int32, (CHUNK, W), 1) // HEAD
    for ch in range(TM // CHUNK):
        rows = slice(ch * CHUNK, (ch + 1) * CHUNK)
        vh, vl = _split2(vn[rows, :])
        s = jnp.zeros((CHUNK, W), F32)
        for gi in range(N_HEADS):
            wh, wl = _split2(ws_ref[gi])
            sg = _dot(wh, vh) + _dot(wl, vh) + _dot(wh, vl)
            s = jnp.where(group_c == gi, sg, s)
        br_ref[rows, 3 * W:4 * W] = (du[rows, :] * (s + bs_ref[...])).astype(BF16)


def _branches(y, g, bonus, bcd, p):
    tok = lambda c: pl.BlockSpec((TM, c), lambda i: (i, 0))
    full = lambda a: pl.BlockSpec(a.shape, lambda i: (0,) * a.ndim)
    lay = lambda a: pl.BlockSpec((1,) + a.shape[1:], lambda i: (jnp.where(i < N_CTX_BLK, 0, 1),) + (0,) * (a.ndim - 1))
    return pl.pallas_call(
        _branch_kernel,
        grid=(N_BLK,),
        in_specs=[tok(W), tok(W), tok(W), tok(6 * W), full(p['lnx_g']), full(p['lnx_b']),
                  lay(p['band']), lay(p['invcnt']), lay(p['shift']),
                  full(p['wpool_bd']), full(p['pool_scale']), full(p['conv_w']), full(p['conv_b']),
                  full(p['sgu_g']), full(p['w_s']), full(p['bs_full']), full(p['ones4'])],
        out_specs=tok(4 * W),
        out_shape=jax.ShapeDtypeStruct((T_ALL, 4 * W), BF16),
        compiler_params=_cparams(1),
        name="branches",
    )(y, g, bonus, bcd, p['lnx_g'], p['lnx_b'], p['band'], p['invcnt'], p['shift'], p['wpool_bd'],
      p['pool_scale'], p['conv_w'], p['conv_b'], p['sgu_g'], p['w_s'], p['bs_full'], p['ones4'])


def _merge_kernel(x_ref, mod_ref, g_ref, br_ref, wgl_ref, wbr_ref, wout_ref, o_ref):
    x = x_ref[...]
    h = _norm_mod(x, g_ref[...], mod_ref[0, 1:2, :], mod_ref[0, 0:1, :]).astype(BF16)
    merged = jnp.zeros((TM, D), F32)
    for i in range(4):
        gl = _dot(h, wgl_ref[:, i * D:(i + 1) * D])
        proj = _dot(br_ref[:, i * W:(i + 1) * W], wbr_ref[i])
        merged = merged + _sigmoid(gl) * proj
    mix = _dot(merged.astype(BF16), wout_ref[...])
    o_ref[...] = x + mod_ref[0, 2:3, :] * mix


def _merge(x, mod_l, g1, br, w_gl, w_branch, w_out):
    full = lambda a: pl.BlockSpec(a.shape, lambda i: (0,) * a.ndim)
    return pl.pallas_call(
        _merge_kernel,
        grid=(N_BLK,),
        in_specs=[pl.BlockSpec((TM, D), lambda i: (i, 0)),
                  pl.BlockSpec((1, N_MOD, D), lambda i: (_mod_row(i), 0, 0)),
                  pl.BlockSpec((1, D), lambda i: (0, 0)),
                  pl.BlockSpec((TM, 4 * W), lambda i: (i, 0)),
                  full(w_gl), full(w_branch), full(w_out)],
        out_specs=pl.BlockSpec((TM, D), lambda i: (i, 0)),
        out_shape=jax.ShapeDtypeStruct((T_ALL, D), F32),
        compiler_params=_cparams(1),
        name="merge",
    )(x, mod_l, g1, br, w_gl, w_branch, w_out)


def _router_kernel(x_ref, mod_ref, g_ref, wr_ref, h_ref, aff_ref):
    h = _norm_mod(x_ref[...], g_ref[...], mod_ref[0, 4:5, :], mod_ref[0, 3:4, :])
    h_ref[...] = h.astype(BF16)
    logits = _dot_hl(h, wr_ref[...])
    lane = lax.broadcasted_iota(jnp.int32, (TM, LANE), 1)
    logits = jnp.where(lane < N_EXPERTS, logits, -1e30)
    m = jnp.max(logits, axis=-1, keepdims=True)
    e = jnp.where(lane < N_EXPERTS, jnp.exp(logits - m), 0.0)
    aff_ref[...] = e / jnp.sum(e, axis=-1, keepdims=True)


def _router(x, mod_l, g2, w_router_pad):
    return pl.pallas_call(
        _router_kernel,
        grid=(N_BLK,),
        in_specs=[pl.BlockSpec((TM, D), lambda i: (i, 0)),
                  pl.BlockSpec((1, N_MOD, D), lambda i: (_mod_row(i), 0, 0)),
                  pl.BlockSpec((1, D), lambda i: (0, 0)),
                  pl.BlockSpec((D, LANE), lambda i: (0, 0))],
        out_specs=[pl.BlockSpec((TM, D), lambda i: (i, 0)), pl.BlockSpec((TM, LANE), lambda i: (i, 0))],
        out_shape=[jax.ShapeDtypeStruct((T_ALL, D), BF16), jax.ShapeDtypeStruct((T_ALL, LANE), F32)],
        compiler_params=_cparams(1),
        name="router",
    )(x, mod_l, g2, w_router_pad)


def _rank_kernel(seq_len, aff_ref, afft_ref, ones_ref, rank_ref):
    row0 = pl.program_id(1) * TM
    t_self = lax.broadcasted_iota(jnp.int32, (TM, seq_len), 0) + row0
    t_other = lax.broadcasted_iota(jnp.int32, (TM, seq_len), 1)
    earlier = t_other < t_self
    lane = lax.broadcasted_iota(jnp.int32, (TM, LANE), 1)
    aff = aff_ref[...]
    rank = jnp.zeros((TM, LANE), F32)
    for e in range(N_EXPERTS):
        mine = aff[:, e:e+1]
        other = afft_ref[e:e+1, :]
        beats = jnp.where(other > mine, 1.0, jnp.where(earlier, jnp.where(other == mine, 1.0, 0.0), 0.0))
        cnt = _dot(beats.astype(BF16), ones_ref[...])
        rank = jnp.where(lane == e, cnt, rank)
    rank_ref[...] = rank


def _ranks(seq_len, n_seq, blk0, aff, aff_t):
    nb = seq_len // TM
    ones = jnp.ones((seq_len, LANE), BF16)
    return pl.pallas_call(
        functools.partial(_rank_kernel, seq_len),
        grid=(n_seq, nb),
        in_specs=[pl.BlockSpec((TM, LANE), lambda s, j: (blk0 + s * nb + j, 0)),
                  pl.BlockSpec((None, N_EXPERTS, seq_len), lambda s, j: (s, 0, 0)),
                  pl.BlockSpec((seq_len, LANE), lambda s, j: (0, 0))],
        out_specs=pl.BlockSpec((TM, LANE), lambda s, j: (s * nb + j, 0)),
        out_shape=jax.ShapeDtypeStruct((n_seq * seq_len, LANE), F32),
        compiler_params=_cparams(2),
        name=f"ranks_{seq_len}",
    )(aff, aff_t, ones)


def _expert_kernel(h_ref, rank_ref, w1_ref, w3_ref, w2_ref, o_ref, xs_scr):
    def gather(seq_len, cap, tok0, slot0):
        slot = lax.broadcasted_iota(jnp.int32, (cap, seq_len), 0).astype(F32)
        rk = rank_ref[:, tok0:tok0 + seq_len]
        onehot = jnp.where(rk == slot, 1.0, 0.0).astype(BF16)
        xs_scr[slot0:slot0 + cap, :] = _dot(onehot, h_ref[tok0:tok0 + seq_len, :]).astype(BF16)

    half = pl.program_id(1)

    @pl.when(half == 0)
    def _():
        for s in range(N_CTX_SEQ):
            gather(L_CTX, CAP_CTX, s * L_CTX, s * CAP_CTX)
        for s in range(N_LAT_SEQ):
            gather(L_LAT, CAP_LAT, T_CTX + s * L_LAT, N_CTX_SEQ * CAP_CTX + s * CAP_LAT)

    xs = xs_scr[...]
    a = _dot(xs, w1_ref[...].astype(BF16))
    act = (a * _sigmoid(a)) * _dot(xs, w3_ref[...].astype(BF16))
    out = _dot(act.astype(BF16), w2_ref[...].astype(BF16))

    @pl.when(half == 0)
    def _():
        o_ref[...] = out

    @pl.when(half != 0)
    def _():
        o_ref[...] += out


FF_SPLIT = 2


def _experts(layer, h2, rank_t, w1, w3, w2):
    fb = FF // FF_SPLIT
    wspec = lambda: pl.BlockSpec((None, None, D, fb), lambda e, f: (layer, e, 0, f))
    return pl.pallas_call(
        _expert_kernel,
        grid=(N_EXPERTS, FF_SPLIT),
        in_specs=[pl.BlockSpec((T_ALL, D), lambda e, f: (0, 0)),
                  pl.BlockSpec((None, 1, T_ALL), lambda e, f: (e, 0, 0)),
                  wspec(), wspec(), pl.BlockSpec((None, None, fb, D), lambda e, f: (layer, e, f, 0))],
        out_specs=pl.BlockSpec((None, SLOTS, D), lambda e, f: (e, 0, 0)),
        out_shape=jax.ShapeDtypeStruct((N_EXPERTS, SLOTS, D), F32),
        scratch_shapes=[pltpu.VMEM((SLOTS, D), BF16)],
        compiler_params=_cparams(2),
        name="experts",
    )(h2, rank_t, w1, w3, w2)


def _combine_kernel(cap, final, x_ref, mod_ref, aff_ref, rank_ref, oe_ref, expand_ref, fg_ref, o_ref):
    n = N_EXPERTS * cap
    expand = expand_ref[...]
    rank_x = _sel_dot_r(rank_ref[...], expand, parts=2)
    aff_x = _sel_dot_r(aff_ref[...], expand, parts=3)
    slot = jnp.bitwise_and(lax.broadcasted_iota(jnp.int32, (TM, n), 1), cap - 1).astype(F32)
    gate = jnp.where(rank_x == slot, aff_x, 0.0)
    acc = _dot_hl(gate, oe_ref[...].reshape(n, D))
    x = x_ref[...] + mod_ref[0, 5:6, :] * acc
    if final:
        ms = jnp.mean(x * x, axis=-1, keepdims=True)
        x = x * lax.rsqrt(ms + NORM_EPS) * fg_ref[...]
    o_ref[...] = x


def _combine(seq_len, n_seq, blk0, cap, slot_blk0, final, x, mod_l, aff, rank, oe, final_g):
    nb = seq_len // TM
    assert cap & (cap - 1) == 0
    tok = lambda c: pl.BlockSpec((TM, c), lambda s, j: (blk0 + s * nb + j, 0))
    expand = jnp.asarray(np.arange(LANE)[:, None] == np.arange(N_EXPERTS * cap)[None, :] // cap, BF16)
    return pl.pallas_call(
        functools.partial(_combine_kernel, cap, final),
        grid=(n_seq, nb),
        in_specs=[tok(D),
                  pl.BlockSpec((1, N_MOD, D), lambda s, j: (_mod_row(blk0 + s * nb + j), 0, 0)),
                  tok(LANE),
                  pl.BlockSpec((TM, LANE), lambda s, j: (s * nb + j, 0)),
                  pl.BlockSpec((N_EXPERTS, cap, D), lambda s, j: (0, slot_blk0 + s, 0)),
                  pl.BlockSpec((LANE, N_EXPERTS * cap), lambda s, j: (0, 0)),
                  pl.BlockSpec((1, D), lambda s, j: (0, 0))],
        out_specs=pl.BlockSpec((TM, D), lambda s, j: (s * nb + j, 0)),
        out_shape=jax.ShapeDtypeStruct((n_seq * seq_len, D), F32),
        compiler_params=_cparams(2),
        name=f"combine_{seq_len}",
    )(x, mod_l, aff, rank, oe, expand, final_g)


def _row_structure(row_len):
    t = np.arange(TM)
    same_row = (t[:, None] // row_len) == (t[None, :] // row_len)
    delta = t[None, :] - t[:, None]
    band = np.stack([same_row & (delta >= -(w // 2)) & (delta < w // 2) for w in POOL_WINDOWS]).astype(np.float32)
    cnt = band.sum(-1)
    invcnt = np.repeat((1.0 / cnt).T, HEAD, axis=1).astype(np.float32)
    shift = np.stack([same_row & (delta == -1), same_row & (delta == 1)]).astype(np.float32)
    return band, invcnt, shift


def _constants():
    lane = np.arange(W)
    ones4 = (lane[:, None] // HEAD == lane[None, :] // HEAD).astype(np.float32)
    eye4 = (np.arange(HEAD)[:, None] == lane[None, :] % HEAD).astype(np.float32)
    structs = [_row_structure(L_CTX), _row_structure(GRID_W)]
    return dict(
        ones4=jnp.asarray(ones4, BF16), eye4=jnp.asarray(eye4, F32),
        band=jnp.asarray(np.stack([s[0] for s in structs]), BF16),
        invcnt=jnp.asarray(np.stack([s[1] for s in structs]), F32),
        shift=jnp.asarray(np.stack([s[2] for s in structs]), BF16))


def _block_diag(blocks):
    n = len(blocks)
    rows = []
    for i, b in enumerate(blocks):
        rows.append(jnp.concatenate([b if j == i else jnp.zeros((b.shape[0], blocks[j].shape[1]), b.dtype)
                                     for j in range(n)], axis=1))
    return jnp.concatenate(rows, axis=0)


def _wide_tiles(s):
    return jnp.moveaxis(s, -3, -2).reshape(s.shape[:-3] + (HEAD, W))


def _head_tiles(s):
    return jnp.moveaxis(s.reshape(s.shape[:-2] + (HEAD, N_HEADS, HEAD)), -2, -3)


def kernel(x_prompt, x_sample, state_rwkv, c, c_ctx, norm1_g, w_mod, b_mod, w_in, w0, w_up, a0, a_up, g_up, k_k, k_a, r_k, lnx_g, lnx_b, w_pool, pool_scale, conv_w, conv_b, sgu_g, w_s, b_s, w_branch, w_out, norm2_g, w_router, w_e1, w_e3, w_e2, final_g):
    const = _constants()
    x = jnp.concatenate([x_prompt.reshape(T_CTX, D), x_sample.reshape(T_LAT, D)], axis=0)
    cond8 = jnp.concatenate([c_ctx[None], c, jnp.zeros((SUBLANE - 1 - N_LAT_SEQ, D), F32)], axis=0)
    mod = _modulation(cond8, w_mod, b_mod).reshape(DEPTH, SUBLANE, N_MOD, D)
    final_g2 = final_g.reshape(1, D)
    ctx_states = []
    y_ctx = y_lat = None
    for l in range(DEPTH):
        mod_l = mod[l]
        g1 = norm1_g[l].reshape(1, D)
        rkv, lora, bcd = _in_projection(x, mod_l, g1, w_in[l, :, :N_SMALL].astype(BF16))
        wup_bd = _block_diag([w_up[l, 0], w_up[l, 1]])
        aup_bd = _block_diag([a_up[l, 0], a_up[l, 1]])
        wdec, bvec, kin, kk, gate_a, bonus = _rwkv_prep(
            rkv, lora, wup_bd, aup_bd, g_up[l], w0[l].reshape(1, 2 * W), a0[l].reshape(1, 2 * W),
            k_k[l].reshape(1, W), k_a[l].reshape(1, W), r_k[l].reshape(1, W), const['ones4'])

        n_ctx_blk = N_CTX_SEQ // CTX_SEQ_PER_SCAN_BLK
        s0_ctx = jnp.zeros((n_ctx_blk, CTX_SEQ_PER_SCAN_BLK, 2, HEAD, W), F32)
        y_c, s_ctx = _rwkv_scan(CTX_SEQ_PER_SCAN_BLK, L_CTX, n_ctx_blk, 0, rkv, kk, wdec, bvec, kin, s0_ctx,
                                const['ones4'], const['eye4'])
        s0_lat = _wide_tiles(state_rwkv[:, l])[None]
        y_l, _ = _rwkv_scan(N_LAT_SEQ, L_LAT, 1, T_CTX // T_LAT, rkv, kk, wdec, bvec, kin, s0_lat,
                            const['ones4'], const['eye4'])
        ctx_states.append(_head_tiles(s_ctx).reshape(N_CTX_SEQ, 2, N_HEADS, HEAD, HEAD))
        y_scan = jnp.concatenate([y_c, y_l], axis=0)

        bs_full = jnp.repeat(b_s[l].T, HEAD, axis=1)
        bp = dict(lnx_g=lnx_g[l].reshape(1, W), lnx_b=lnx_b[l].reshape(1, W), band=const['band'],
                  invcnt=const['invcnt'], shift=const['shift'],
                  wpool_bd=_block_diag([w_pool[l, i] for i in range(len(POOL_WINDOWS))]),
                  pool_scale=pool_scale[l].reshape(1, W), conv_w=conv_w[l], conv_b=conv_b[l].reshape(1, W),
                  sgu_g=sgu_g[l].reshape(1, W), w_s=w_s[l], bs_full=bs_full, ones4=const['ones4'])
        br = _branches(y_scan, gate_a, bonus, bcd, bp)
        x = _merge(x, mod_l, g1, br, w_in[l, :, N_SMALL:].astype(BF16), w_branch[l].astype(BF16),
                   w_out[l].astype(BF16))

        wr_pad = jnp.concatenate([w_router[l], jnp.zeros((D, LANE - N_EXPERTS), F32)], axis=1)
        h2, aff = _router(x, mod_l, norm2_g[l].reshape(1, D), wr_pad)
        aff_t_ctx = jnp.swapaxes(aff[:T_CTX, :N_EXPERTS].reshape(N_CTX_SEQ, L_CTX, N_EXPERTS), 1, 2)
        aff_t_lat = jnp.swapaxes(aff[T_CTX:, :N_EXPERTS].reshape(N_LAT_SEQ, L_LAT, N_EXPERTS), 1, 2)
        rank_ctx = _ranks(L_CTX, N_CTX_SEQ, 0, aff, aff_t_ctx)
        rank_lat = _ranks(L_LAT, N_LAT_SEQ, N_CTX_BLK, aff, aff_t_lat)
        rank_t = jnp.concatenate([rank_ctx[:, :N_EXPERTS], rank_lat[:, :N_EXPERTS]], axis=0).T
        oe = _experts(l, h2, rank_t.reshape(N_EXPERTS, 1, T_ALL), w_e1, w_e3, w_e2)
        final = l == DEPTH - 1
        y_ctx = _combine(L_CTX, N_CTX_SEQ, 0, CAP_CTX, 0, final, x, mod_l, aff, rank_ctx, oe, final_g2)
        y_lat = _combine(L_LAT, N_LAT_SEQ, N_CTX_BLK, CAP_LAT, N_CTX_SEQ * CAP_CTX // CAP_LAT, final, x, mod_l,
                         aff, rank_lat, oe, final_g2)
        if not final:
            x = jnp.concatenate([y_ctx, y_lat], axis=0)
    y_prompt = y_ctx.reshape(N_CTX_SEQ, L_CTX, D)
    y_sample = y_lat.reshape(N_LAT_SEQ, L_LAT, D)
    new_state = jnp.stack(ctx_states, axis=1)
    return (y_prompt, y_sample, new_state)
```

```python
import functools
import math

import numpy as np
import jax
import jax.numpy as jnp
from jax import lax
from jax.experimental import pallas as pl
from jax.experimental.pallas import tpu as pltpu

F32 = jnp.float32
BF16 = jnp.bfloat16

D = 1024
N_CTX_SEQ, L_CTX = 16, 256
N_LAT_SEQ, L_LAT = 2, 1024
T_CTX = N_CTX_SEQ * L_CTX
T_LAT = N_LAT_SEQ * L_LAT
T_ALL = T_CTX + T_LAT
DEPTH = 2
TM = 256
N_BLK = T_ALL // TM
N_CTX_BLK = T_CTX // TM
LAT_BLK_PER_SEQ = L_LAT // TM
GRID_W = 64
W = 256
HEAD = 64
N_HEADS = W // HEAD
LORA = 64
LORA_G = 128
DECAY_SCALE = math.exp(-0.5)
POOL_WINDOWS = (2, 4, 8, 16)
CHUNK = 128
N_EXPERTS = 16
FF = 1024
CAP_CTX = 2 * L_CTX // N_EXPERTS
CAP_LAT = 2 * L_LAT // N_EXPERTS
SLOTS = N_CTX_SEQ * CAP_CTX + N_LAT_SEQ * CAP_LAT
N_MOD = 6
NORM_EPS = 1e-6
GN_EPS = 64e-5
N_SMALL = 3 * W + 2 * LORA + 2 * LORA + LORA_G + 6 * W
LANE = 128
SUBLANE = 8
VMEM_LIMIT = 56 * 1024 * 1024


def _cparams(n_axes):
    return pltpu.CompilerParams(dimension_semantics=("arbitrary",) * n_axes,
                                vmem_limit_bytes=VMEM_LIMIT)


def _split2(x):
    hi = x.astype(BF16)
    lo = (x - hi.astype(F32)).astype(BF16)
    return hi, lo


def _split3(x):
    hi = x.astype(BF16)
    r = x - hi.astype(F32)
    mid = r.astype(BF16)
    lo = (r - mid.astype(F32)).astype(BF16)
    return hi, mid, lo


def _dot(a, b):
    return jnp.dot(a, b, preferred_element_type=F32)


def _sel_dot_l(m_bf16, x, parts=2):
    ps = _split2(x) if parts == 2 else _split3(x)
    acc = _dot(m_bf16, ps[0])
    for p in ps[1:]:
        acc = acc + _dot(m_bf16, p)
    return acc


def _sel_dot_r(x, m_bf16, parts=2):
    ps = _split2(x) if parts == 2 else _split3(x)
    acc = _dot(ps[0], m_bf16)
    for p in ps[1:]:
        acc = acc + _dot(p, m_bf16)
    return acc


def _dot_hl(a, b):
    ah, al = _split2(a)
    bh, bl = _split2(b)
    return _dot(ah, bh) + _dot(al, bh) + _dot(ah, bl)


def _sigmoid(x):
    return 1.0 / (1.0 + jnp.exp(-x))


def _gelu_tanh(x):
    return 0.5 * x * (1.0 + jnp.tanh(math.sqrt(2.0 / math.pi) * (x + 0.044715 * (x * x * x))))


def _norm_mod(x, g, scale, shift):
    ms = jnp.mean(x * x, axis=-1, keepdims=True)
    return (x * lax.rsqrt(ms + NORM_EPS) * g) * (1.0 + scale) + shift


def _mod_row(i):
    return jnp.where(i < N_CTX_BLK, 0, (i - N_CTX_BLK) // LAT_BLK_PER_SEQ + 1)


def _mod_kernel(c_ref, w_ref, b_ref, o_ref):
    c = c_ref[...]
    s = c * _sigmoid(c)
    o_ref[...] = _dot(s.astype(BF16), w_ref[...].astype(BF16)) + b_ref[...]


def _modulation(cond8, w_mod, b_mod):
    tn = 1536
    n = N_MOD * D
    return pl.pallas_call(
        _mod_kernel,
        grid=(DEPTH, n // tn),
        in_specs=[pl.BlockSpec((SUBLANE, D), lambda l, j: (0, 0)),
                  pl.BlockSpec((None, D, tn), lambda l, j: (l, 0, j)),
                  pl.BlockSpec((None, 1, tn), lambda l, j: (l, 0, j))],
        out_specs=pl.BlockSpec((None, SUBLANE, tn), lambda l, j: (l, 0, j)),
        out_shape=jax.ShapeDtypeStruct((DEPTH, SUBLANE, n), F32),
        compiler_params=_cparams(2),
        name="modulation",
    )(cond8, w_mod, b_mod.reshape(DEPTH, 1, n))


def _inproj_kernel(x_ref, mod_ref, g_ref, w_ref, wup_ref, aup_ref, gup_ref, w0_ref, a0_ref, kk_k_ref, k_a_ref,
                   r_k_ref, ones4_ref, rkv_ref, bcd_ref, wdec_ref, b_ref, kin_ref, kk_ref, gate_ref, bonus_ref):
    h = _norm_mod(x_ref[...], g_ref[...], mod_ref[0, 1:2, :], mod_ref[0, 0:1, :])
    z = _dot(h.astype(BF16), w_ref[...])
    rkv_ref[...] = z[:, :3 * W]
    bcd_ref[...] = z[:, 3 * W + 3 * LANE:]
    r = z[:, 0:W]
    k = z[:, W:2 * W]
    v = z[:, 2 * W:3 * W]
    xw = z[:, 3 * W:3 * W + LANE]
    xa = z[:, 3 * W + LANE:3 * W + 2 * LANE]
    xg = z[:, 3 * W + 2 * LANE:3 * W + 3 * LANE]
    ones4 = ones4_ref[...]
    dec = w0_ref[...] + _dot_hl(jnp.tanh(xw), wup_ref[...])
    wdec_ref[...] = jnp.exp(-DECAY_SCALE * _sigmoid(dec))
    a = _sigmoid(a0_ref[...] + _dot_hl(xa, aup_ref[...]))
    gate_ref[...] = _dot_hl(_sigmoid(xg), gup_ref[...])
    kk = k * kk_k_ref[...]
    n2 = _sel_dot_r(kk * kk, ones4, parts=3)
    kk = kk / jnp.maximum(jnp.sqrt(n2), 1e-12)
    kk_ref[...] = kk
    k_a = k_a_ref[...]
    kin0 = k * (1.0 + (a[:, 0:W] - 1.0) * k_a)
    kin1 = k * (1.0 + (a[:, W:2 * W] - 1.0) * k_a)
    kin_ref[:, 0:W] = kin0
    kin_ref[:, W:2 * W] = kin1
    b_ref[:, 0:W] = a[:, 0:W] * kk
    b_ref[:, W:2 * W] = a[:, W:2 * W] * kk
    bonus = _sel_dot_r(r * (kin0 + kin1) * r_k_ref[...], ones4, parts=3)
    bonus_ref[...] = bonus * v


def _in_projection(x, mod_l, g1, w_small, wup_bd, aup_bd, g_up, w0, a0, k_k, k_a, r_k, ones4):
    tok = lambda c: pl.BlockSpec((TM, c), lambda i: (i, 0))
    full = lambda a: pl.BlockSpec(a.shape, lambda i: (0,) * a.ndim)
    params = (w_small, wup_bd, aup_bd, g_up, w0, a0, k_k, k_a, r_k, ones4)
    widths = (3 * W, 6 * W, 2 * W, 2 * W, 2 * W, W, W, W)
    return pl.pallas_call(
        _inproj_kernel,
        grid=(N_BLK,),
        in_specs=[tok(D), pl.BlockSpec((1, N_MOD, D), lambda i: (_mod_row(i), 0, 0)), full(g1)]
                 + [full(a) for a in params],
        out_specs=[tok(c) for c in widths],
        out_shape=[jax.ShapeDtypeStruct((T_ALL, c), F32) for c in widths],
        compiler_params=_cparams(1),
        name="in_projection",
    )(x, mod_l, g1, *params)


T_STEP = SUBLANE
CHAINS_PER_DOT = 6
CTX_SEQ_PER_SCAN_BLK = 4
N_SCAN_BLK = N_CTX_SEQ // CTX_SEQ_PER_SCAN_BLK
LAT_WIN = L_LAT // N_SCAN_BLK
SCAN_FIELDS = ('r', 'v', 'kk', 'w', 'b', 'k')


def _scan_kernel(*refs):
    n_f = len(SCAN_FIELDS)
    ctx_refs = dict(zip(SCAN_FIELDS, refs[:n_f]))
    lat_refs = {}
    pos = n_f
    for s in range(N_LAT_SEQ):
        for d in range(2):
            lat_refs[(s, d)] = dict(zip(SCAN_FIELDS, refs[pos:pos + n_f]))
            pos += n_f
    s0c_ref, s0l_ref, ones4_ref, eye4_ref, yc_ref, sfin_ref, yl_ref, slat_scr = refs[pos:]
    step_i = pl.program_id(0)
    ones4 = ones4_ref[...]
    eye4 = eye4_ref[...]
    chains = ([(False, b, d) for b in range(CTX_SEQ_PER_SCAN_BLK) for d in range(2)]
              + [(True, s, d) for s in range(N_LAT_SEQ) for d in range(2)])
    n_chain = len(chains)
    groups = [chains[g:g + CHAINS_PER_DOT] for g in range(0, n_chain, CHAINS_PER_DOT)]

    @pl.when(step_i == 0)
    def _():
        yl_ref[...] = jnp.zeros_like(yl_ref)
        for s in range(N_LAT_SEQ):
            for d in range(2):
                slat_scr[s, d] = s0l_ref[s, d]

    yc_ref[...] = jnp.zeros_like(yc_ref)

    def row_sum(blocks):
        return _dot(jnp.concatenate(blocks, axis=0).astype(BF16), ones4)

    def body(tb, state):
        state = list(state)
        tiles = []
        for lat, b, d in chains:
            t0 = tb * T_STEP if d == 0 else L_CTX - T_STEP - tb * T_STEP
            if lat:
                rows = pl.ds(pl.multiple_of(t0, T_STEP), T_STEP)
                tiles.append({f: lat_refs[(b, d)][f][rows, :] for f in SCAN_FIELDS})
            else:
                rows = pl.ds(pl.multiple_of(b * L_CTX + t0, T_STEP), T_STEP)
                col = lambda f: slice(d * W, (d + 1) * W) if f in ('w', 'b', 'k') else slice(0, W)
                tiles.append({f: ctx_refs[f][rows, col(f)] for f in SCAN_FIELDS})

        def row(name, c, j):
            jj = j if chains[c][2] == 0 else T_STEP - 1 - j
            return tiles[c][name][jj:jj + 1, :]

        y_rows = [[None] * T_STEP for _ in chains]
        for j in range(T_STEP):
            v_cols = row_sum([eye4 * row('v', c, j) for c in range(n_chain)])
            q = [None] * n_chain
            for grp in groups:
                cs = [chains.index(ch) for ch in grp]
                s_kk = row_sum([state[c] * row('kk', c, j) for c in cs])
                for i, c in enumerate(cs):
                    s = (state[c] * row('w', c, j) - s_kk[i * HEAD:(i + 1) * HEAD] * row('b', c, j)
                         + v_cols[c * HEAD:(c + 1) * HEAD] * row('k', c, j))
                    state[c] = s
                    q[c] = s * row('r', c, j)
            y_cols = row_sum(q)
            for c, (lat, b, d) in enumerate(chains):
                jj = j if d == 0 else T_STEP - 1 - j
                y_rows[c][jj] = jnp.sum(eye4 * y_cols[c * HEAD:(c + 1) * HEAD], axis=0, keepdims=True)
        for c, (lat, b, d) in enumerate(chains):
            t0 = tb * T_STEP if d == 0 else L_CTX - T_STEP - tb * T_STEP
            y_tile = jnp.concatenate(y_rows[c], axis=0)
            if lat:
                win0 = step_i * LAT_WIN if d == 0 else L_LAT - LAT_WIN - step_i * LAT_WIN
                rows = pl.ds(pl.multiple_of(b * L_LAT + win0 + t0, T_STEP), T_STEP)
                yl_ref[rows, :] += y_tile
            else:
                rows = pl.ds(pl.multiple_of(b * L_CTX + t0, T_STEP), T_STEP)
                yc_ref[rows, :] += y_tile
        return tuple(state)

    init = tuple(slat_scr[b, d] if lat else s0c_ref[b, d] for (lat, b, d) in chains)
    fin = lax.fori_loop(0, L_CTX // T_STEP, body, init)
    for c, (lat, b, d) in enumerate(chains):
        if lat:
            slat_scr[b, d] = fin[c]
        else:
            sfin_ref[b, d] = fin[c]


def _rwkv_scan(rkv, kk, wdec, bvec, kin, s0_ctx, s0_lat, ones4, eye4):
    assert LAT_WIN == L_CTX
    arrays = dict(r=(rkv, 0), v=(rkv, 2), kk=(kk, 0), w=(wdec, None), b=(bvec, None), k=(kin, None))
    rows_c = CTX_SEQ_PER_SCAN_BLK * L_CTX
    operands, in_specs = [], []
    for f in SCAN_FIELDS:
        a, col = arrays[f]
        operands.append(a)
        in_specs.append(pl.BlockSpec((rows_c, W if col is not None else 2 * W),
                                     lambda i, col=col: (i, col if col is not None else 0)))
    win_blk0 = T_CTX // LAT_WIN
    for s in range(N_LAT_SEQ):
        for d in range(2):
            for f in SCAN_FIELDS:
                a, col = arrays[f]
                operands.append(a)
                in_specs.append(pl.BlockSpec(
                    (LAT_WIN, W),
                    lambda i, s=s, d=d, col=col: (win_blk0 + s * N_SCAN_BLK + (i if d == 0 else N_SCAN_BLK - 1 - i),
                                                  col if col is not None else d)))
    st = pl.BlockSpec((None, CTX_SEQ_PER_SCAN_BLK, 2, HEAD, W), lambda i: (i, 0, 0, 0, 0))
    full = lambda a: pl.BlockSpec(a.shape, lambda i: (0,) * a.ndim)
    operands += [s0_ctx, s0_lat, ones4, eye4]
    in_specs += [st, full(s0_lat), full(ones4), full(eye4)]
    return pl.pallas_call(
        _scan_kernel,
        grid=(N_SCAN_BLK,),
        in_specs=in_specs,
        out_specs=[pl.BlockSpec((rows_c, W), lambda i: (i, 0)), st, pl.BlockSpec((T_LAT, W), lambda i: (0, 0))],
        out_shape=[jax.ShapeDtypeStruct((T_CTX, W), F32),
                   jax.ShapeDtypeStruct((N_SCAN_BLK, CTX_SEQ_PER_SCAN_BLK, 2, HEAD, W), F32),
                   jax.ShapeDtypeStruct((T_LAT, W), F32)],
        scratch_shapes=[pltpu.VMEM((N_LAT_SEQ, 2, HEAD, W), F32)],
        compiler_params=_cparams(1),
        name="rwkv_scan",
    )(*operands)


def _branch_kernel(y_ref, g_ref, bonus_ref, bcd_ref, lnx_g_ref, lnx_b_ref, band_ref, invcnt_ref, shift_ref,
                   wpool_ref, pscale_ref, conv_w_ref, conv_b_ref, sgu_g_ref, ws_ref, bs_ref, ones4_ref, br_ref):
    ones4 = ones4_ref[...]
    group = lax.broadcasted_iota(jnp.int32, (TM, W), 1) // HEAD

    def seg_mean(x):
        return _sel_dot_r(x, ones4, parts=3) * (1.0 / HEAD)

    y = y_ref[...]
    mu = seg_mean(y)
    yc = y - mu
    var = seg_mean(yc * yc)
    ya = yc * lax.rsqrt(var + GN_EPS) * lnx_g_ref[...] + lnx_b_ref[...] + bonus_ref[...]
    br_ref[:, 0:W] = (ya * g_ref[...]).astype(BF16)

    u = bcd_ref[:, 0:W]
    u_parts = _split3(u)
    win = jnp.zeros((TM, W), F32)
    for gi in range(len(POOL_WINDOWS)):
        band = band_ref[0, gi]
        s = _dot(band, u_parts[0]) + _dot(band, u_parts[1]) + _dot(band, u_parts[2])
        win = jnp.where(group == gi, s, win)
    pooled = win * invcnt_ref[0] - u
    br_ref[:, W:2 * W] = (_dot_hl(pooled, wpool_ref[...]) * pscale_ref[...]).astype(BF16)

    cin = bcd_ref[:, W:2 * W]
    cb = bcd_ref[:, 2 * W:3 * W]
    cc = bcd_ref[:, 3 * W:4 * W]
    t = cc * cin
    t_prev = _sel_dot_l(shift_ref[0, 0], t, parts=3)
    t_next = _sel_dot_l(shift_ref[0, 1], t, parts=3)
    conv = conv_b_ref[...] + t_prev * conv_w_ref[0:1, :]
    conv = conv + t * conv_w_ref[1:2, :]
    conv = conv + t_next * conv_w_ref[2:3, :]
    br_ref[:, 2 * W:3 * W] = (cb * conv).astype(BF16)

    du = _gelu_tanh(bcd_ref[:, 4 * W:5 * W])
    dv = _gelu_tanh(bcd_ref[:, 5 * W:6 * W])
    mu = seg_mean(dv)
    dc = dv - mu
    var = seg_mean(dc * dc)
    vn = dc * lax.rsqrt(var + NORM_EPS) * sgu_g_ref[...]
    group_c = lax.broadcasted_iota(jnp.int32, (CHUNK, W), 1) // HEAD
    for ch in range(TM // CHUNK):
        rows = slice(ch * CHUNK, (ch + 1) * CHUNK)
        vh, vl = _split2(vn[rows, :])
        s = jnp.zeros((CHUNK, W), F32)
        for gi in range(N_HEADS):
            wh, wl = _split2(ws_ref[gi])
            sg = _dot(wh, vh) + _dot(wl, vh) + _dot(wh, vl)
            s = jnp.where(group_c == gi, sg, s)
        br_ref[rows, 3 * W:4 * W] = (du[rows, :] * (s + bs_ref[...])).astype(BF16)


def _branches(y, g, bonus, bcd, p):
    tok = lambda c: pl.BlockSpec((TM, c), lambda i: (i, 0))
    full = lambda a: pl.BlockSpec(a.shape, lambda i: (0,) * a.ndim)
    lay = lambda a: pl.BlockSpec((1,) + a.shape[1:], lambda i: (jnp.where(i < N_CTX_BLK, 0, 1),) + (0,) * (a.ndim - 1))
    return pl.pallas_call(
        _branch_kernel,
        grid=(N_BLK,),
        in_specs=[tok(W), tok(W), tok(W), tok(6 * W), full(p['lnx_g']), full(p['lnx_b']),
                  lay(p['band']), lay(p['invcnt']), lay(p['shift']),
                  full(p['wpool_bd']), full(p['pool_scale']), full(p['conv_w']), full(p['conv_b']),
                  full(p['sgu_g']), full(p['w_s']), full(p['bs_full']), full(p['ones4'])],
        out_specs=tok(4 * W),
        out_shape=jax.ShapeDtypeStruct((T_ALL, 4 * W), BF16),
        compiler_params=_cparams(1),
        name="branches",
    )(y, g, bonus, bcd, p['lnx_g'], p['lnx_b'], p['band'], p['invcnt'], p['shift'], p['wpool_bd'],
      p['pool_scale'], p['conv_w'], p['conv_b'], p['sgu_g'], p['w_s'], p['bs_full'], p['ones4'])


def _merge_kernel(x_ref, mod_ref, g_ref, br_ref, wgl_ref, wbr_ref, wout_ref, o_ref):
    x = x_ref[...]
    h = _norm_mod(x, g_ref[...], mod_ref[0, 1:2, :], mod_ref[0, 0:1, :]).astype(BF16)
    merged = jnp.zeros((TM, D), F32)
    for i in range(4):
        gl = _dot(h, wgl_ref[:, i * D:(i + 1) * D])
        proj = _dot(br_ref[:, i * W:(i + 1) * W], wbr_ref[i])
        merged = merged + _sigmoid(gl) * proj
    mix = _dot(merged.astype(BF16), wout_ref[...])
    o_ref[...] = x + mod_ref[0, 2:3, :] * mix


def _merge(x, mod_l, g1, br, w_gl, w_branch, w_out):
    full = lambda a: pl.BlockSpec(a.shape, lambda i: (0,) * a.ndim)
    return pl.pallas_call(
        _merge_kernel,
        grid=(N_BLK,),
        in_specs=[pl.BlockSpec((TM, D), lambda i: (i, 0)),
                  pl.BlockSpec((1, N_MOD, D), lambda i: (_mod_row(i), 0, 0)),
                  pl.BlockSpec((1, D), lambda i: (0, 0)),
                  pl.BlockSpec((TM, 4 * W), lambda i: (i, 0)),
                  full(w_gl), full(w_branch), full(w_out)],
        out_specs=pl.BlockSpec((TM, D), lambda i: (i, 0)),
        out_shape=jax.ShapeDtypeStruct((T_ALL, D), F32),
        compiler_params=_cparams(1),
        name="merge",
    )(x, mod_l, g1, br, w_gl, w_branch, w_out)


def _router_kernel(x_ref, mod_ref, g_ref, wr_ref, h_ref, aff_ref):
    h = _norm_mod(x_ref[...], g_ref[...], mod_ref[0, 4:5, :], mod_ref[0, 3:4, :])
    h_ref[...] = h.astype(BF16)
    logits = _dot_hl(h, wr_ref[...])
    lane = lax.broadcasted_iota(jnp.int32, (TM, LANE), 1)
    logits = jnp.where(lane < N_EXPERTS, logits, -1e30)
    m = jnp.max(logits, axis=-1, keepdims=True)
    e = jnp.where(lane < N_EXPERTS, jnp.exp(logits - m), 0.0)
    aff_ref[...] = e / jnp.sum(e, axis=-1, keepdims=True)


def _router(x, mod_l, g2, w_router_pad):
    return pl.pallas_call(
        _router_kernel,
        grid=(N_BLK,),
        in_specs=[pl.BlockSpec((TM, D), lambda i: (i, 0)),
                  pl.BlockSpec((1, N_MOD, D), lambda i: (_mod_row(i), 0, 0)),
                  pl.BlockSpec((1, D), lambda i: (0, 0)),
                  pl.BlockSpec((D, LANE), lambda i: (0, 0))],
        out_specs=[pl.BlockSpec((TM, D), lambda i: (i, 0)), pl.BlockSpec((TM, LANE), lambda i: (i, 0))],
        out_shape=[jax.ShapeDtypeStruct((T_ALL, D), BF16), jax.ShapeDtypeStruct((T_ALL, LANE), F32)],
        compiler_params=_cparams(1),
        name="router",
    )(x, mod_l, g2, w_router_pad)


def _rank_kernel(seq_len, aff_ref, afft_ref, ones_ref, rank_ref):
    row0 = pl.program_id(1) * TM
    t_self = lax.broadcasted_iota(jnp.int32, (TM, seq_len), 0) + row0
    t_other = lax.broadcasted_iota(jnp.int32, (TM, seq_len), 1)
    earlier = t_other < t_self
    lane = lax.broadcasted_iota(jnp.int32, (TM, LANE), 1)
    aff = aff_ref[...]
    rank = jnp.zeros((TM, LANE), F32)
    for e in range(N_EXPERTS):
        mine = aff[:, e:e+1]
        other = afft_ref[e:e+1, :]
        beats = jnp.where(other > mine, 1.0, jnp.where(earlier, jnp.where(other == mine, 1.0, 0.0), 0.0))
        cnt = _dot(beats.astype(BF16), ones_ref[...])
        rank = jnp.where(lane == e, cnt, rank)
    rank_ref[...] = rank


def _ranks(seq_len, n_seq, blk0, aff, aff_t):
    nb = seq_len // TM
    ones = jnp.ones((seq_len, LANE), BF16)
    return pl.pallas_call(
        functools.partial(_rank_kernel, seq_len),
        grid=(n_seq, nb),
        in_specs=[pl.BlockSpec((TM, LANE), lambda s, j: (blk0 + s * nb + j, 0)),
                  pl.BlockSpec((None, N_EXPERTS, seq_len), lambda s, j: (s, 0, 0)),
                  pl.BlockSpec((seq_len, LANE), lambda s, j: (0, 0))],
        out_specs=pl.BlockSpec((TM, LANE), lambda s, j: (s * nb + j, 0)),
        out_shape=jax.ShapeDtypeStruct((n_seq * seq_len, LANE), F32),
        compiler_params=_cparams(2),
        name=f"ranks_{seq_len}",
    )(aff, aff_t, ones)


def _expert_kernel(h_ref, rank_ref, w1_ref, w3_ref, w2_ref, o_ref, xs_scr):
    def gather(seq_len, cap, tok0, slot0):
        slot = lax.broadcasted_iota(jnp.int32, (cap, seq_len), 0).astype(F32)
        rk = rank_ref[:, tok0:tok0 + seq_len]
        onehot = jnp.where(rk == slot, 1.0, 0.0).astype(BF16)
        xs_scr[slot0:slot0 + cap, :] = _dot(onehot, h_ref[tok0:tok0 + seq_len, :]).astype(BF16)

    half = pl.program_id(1)

    @pl.when(half == 0)
    def _():
        for s in range(N_CTX_SEQ):
            gather(L_CTX, CAP_CTX, s * L_CTX, s * CAP_CTX)
        for s in range(N_LAT_SEQ):
            gather(L_LAT, CAP_LAT, T_CTX + s * L_LAT, N_CTX_SEQ * CAP_CTX + s * CAP_LAT)

    xs = xs_scr[...]
    a = _dot(xs, w1_ref[...].astype(BF16))
    act = (a * _sigmoid(a)) * _dot(xs, w3_ref[...].astype(BF16))
    out = _dot(act.astype(BF16), w2_ref[...].astype(BF16))

    @pl.when(half == 0)
    def _():
        o_ref[...] = out

    @pl.when(half != 0)
    def _():
        o_ref[...] += out


FF_SPLIT = 2


def _experts(layer, h2, rank_t, w1, w3, w2):
    fb = FF // FF_SPLIT
    wspec = lambda: pl.BlockSpec((None, None, D, fb), lambda e, f: (layer, e, 0, f))
    return pl.pallas_call(
        _expert_kernel,
        grid=(N_EXPERTS, FF_SPLIT),
        in_specs=[pl.BlockSpec((T_ALL, D), lambda e, f: (0, 0)),
                  pl.BlockSpec((None, 1, T_ALL), lambda e, f: (e, 0, 0)),
                  wspec(), wspec(), pl.BlockSpec((None, None, fb, D), lambda e, f: (layer, e, f, 0))],
        out_specs=pl.BlockSpec((None, SLOTS, D), lambda e, f: (e, 0, 0)),
        out_shape=jax.ShapeDtypeStruct((N_EXPERTS, SLOTS, D), F32),
        scratch_shapes=[pltpu.VMEM((SLOTS, D), BF16)],
        compiler_params=_cparams(2),
        name="experts",
    )(h2, rank_t, w1, w3, w2)


def _combine_kernel(cap, final, x_ref, mod_ref, aff_ref, rank_ref, oe_ref, expand_ref, fg_ref, o_ref):
    n = N_EXPERTS * cap
    expand = expand_ref[...]
    rank_x = _sel_dot_r(rank_ref[...], expand, parts=2)
    aff_x = _sel_dot_r(aff_ref[...], expand, parts=3)
    slot = jnp.bitwise_and(lax.broadcasted_iota(jnp.int32, (TM, n), 1), cap - 1).astype(F32)
    gate = jnp.where(rank_x == slot, aff_x, 0.0)
    acc = _dot_hl(gate, oe_ref[...].reshape(n, D))
    x = x_ref[...] + mod_ref[0, 5:6, :] * acc
    if final:
        ms = jnp.mean(x * x, axis=-1, keepdims=True)
        x = x * lax.rsqrt(ms + NORM_EPS) * fg_ref[...]
    o_ref[...] = x


def _combine(seq_len, n_seq, blk0, cap, slot_blk0, final, x, mod_l, aff, rank, oe, final_g):
    nb = seq_len // TM
    assert cap & (cap - 1) == 0
    tok = lambda c: pl.BlockSpec((TM, c), lambda s, j: (blk0 + s * nb + j, 0))
    expand = jnp.asarray(np.arange(LANE)[:, None] == np.arange(N_EXPERTS * cap)[None, :] // cap, BF16)
    return pl.pallas_call(
        functools.partial(_combine_kernel, cap, final),
        grid=(n_seq, nb),
        in_specs=[tok(D),
                  pl.BlockSpec((1, N_MOD, D), lambda s, j: (_mod_row(blk0 + s * nb + j), 0, 0)),
                  tok(LANE),
                  pl.BlockSpec((TM, LANE), lambda s, j: (s * nb + j, 0)),
                  pl.BlockSpec((N_EXPERTS, cap, D), lambda s, j: (0, slot_blk0 + s, 0)),
                  pl.BlockSpec((LANE, N_EXPERTS * cap), lambda s, j: (0, 0)),
                  pl.BlockSpec((1, D), lambda s, j: (0, 0))],
        out_specs=pl.BlockSpec((TM, D), lambda s, j: (s * nb + j, 0)),
        out_shape=jax.ShapeDtypeStruct((n_seq * seq_len, D), F32),
        compiler_params=_cparams(2),
        name=f"combine_{seq_len}",
    )(x, mod_l, aff, rank, oe, expand, final_g)


def _row_structure(row_len):
    t = np.arange(TM)
    same_row = (t[:, None] // row_len) == (t[None, :] // row_len)
    delta = t[None, :] - t[:, None]
    band = np.stack([same_row & (delta >= -(w // 2)) & (delta < w // 2) for w in POOL_WINDOWS]).astype(np.float32)
    cnt = band.sum(-1)
    invcnt = np.repeat((1.0 / cnt).T, HEAD, axis=1).astype(np.float32)
    shift = np.stack([same_row & (delta == -1), same_row & (delta == 1)]).astype(np.float32)
    return band, invcnt, shift


def _constants():
    lane = np.arange(W)
    ones4 = (lane[:, None] // HEAD == lane[None, :] // HEAD).astype(np.float32)
    eye4 = (np.arange(HEAD)[:, None] == lane[None, :] % HEAD).astype(np.float32)
    structs = [_row_structure(L_CTX), _row_structure(GRID_W)]
    return dict(
        ones4=jnp.asarray(ones4, BF16), eye4=jnp.asarray(eye4, F32),
        band=jnp.asarray(np.stack([s[0] for s in structs]), BF16),
        invcnt=jnp.asarray(np.stack([s[1] for s in structs]), F32),
        shift=jnp.asarray(np.stack([s[2] for s in structs]), BF16))


def _block_diag(blocks):
    n = len(blocks)
    rows = []
    for i, b in enumerate(blocks):
        rows.append(jnp.concatenate([b if j == i else jnp.zeros((b.shape[0], blocks[j].shape[1]), b.dtype)
                                     for j in range(n)], axis=1))
    return jnp.concatenate(rows, axis=0)


def _wide_tiles(s):
    return jnp.moveaxis(s, -3, -2).reshape(s.shape[:-3] + (HEAD, W))


def _head_tiles(s):
    return jnp.moveaxis(s.reshape(s.shape[:-2] + (HEAD, N_HEADS, HEAD)), -2, -3)


def kernel(x_prompt, x_sample, state_rwkv, c, c_ctx, norm1_g, w_mod, b_mod, w_in, w0, w_up, a0, a_up, g_up, k_k, k_a, r_k, lnx_g, lnx_b, w_pool, pool_scale, conv_w, conv_b, sgu_g, w_s, b_s, w_branch, w_out, norm2_g, w_router, w_e1, w_e3, w_e2, final_g):
    const = _constants()
    x = jnp.concatenate([x_prompt.reshape(T_CTX, D), x_sample.reshape(T_LAT, D)], axis=0)
    cond8 = jnp.concatenate([c_ctx[None], c, jnp.zeros((SUBLANE - 1 - N_LAT_SEQ, D), F32)], axis=0)
    mod = _modulation(cond8, w_mod, b_mod).reshape(DEPTH, SUBLANE, N_MOD, D)
    final_g2 = final_g.reshape(1, D)
    ctx_states = []
    y_ctx = y_lat = None
    for l in range(DEPTH):
        mod_l = mod[l]
        g1 = norm1_g[l].reshape(1, D)
        wup_bd = _block_diag([w_up[l, 0], w_up[l, 1]])
        aup_bd = _block_diag([a_up[l, 0], a_up[l, 1]])
        rkv, bcd, wdec, bvec, kin, kk, gate_a, bonus = _in_projection(
            x, mod_l, g1, w_in[l, :, :N_SMALL].astype(BF16), wup_bd, aup_bd, g_up[l], w0[l].reshape(1, 2 * W),
            a0[l].reshape(1, 2 * W), k_k[l].reshape(1, W), k_a[l].reshape(1, W), r_k[l].reshape(1, W),
            const['ones4'])

        s0_ctx = jnp.zeros((N_SCAN_BLK, CTX_SEQ_PER_SCAN_BLK, 2, HEAD, W), F32)
        y_c, s_ctx, y_l = _rwkv_scan(rkv, kk, wdec, bvec, kin, s0_ctx, _wide_tiles(state_rwkv[:, l]),
                                     const['ones4'], const['eye4'])
        ctx_states.append(_head_tiles(s_ctx).reshape(N_CTX_SEQ, 2, N_HEADS, HEAD, HEAD))
        y_scan = jnp.concatenate([y_c, y_l], axis=0)

        bs_full = jnp.repeat(b_s[l].T, HEAD, axis=1)
        bp = dict(lnx_g=lnx_g[l].reshape(1, W), lnx_b=lnx_b[l].reshape(1, W), band=const['band'],
                  invcnt=const['invcnt'], shift=const['shift'],
                  wpool_bd=_block_diag([w_pool[l, i] for i in range(len(POOL_WINDOWS))]),
                  pool_scale=pool_scale[l].reshape(1, W), conv_w=conv_w[l], conv_b=conv_b[l].reshape(1, W),
                  sgu_g=sgu_g[l].reshape(1, W), w_s=w_s[l], bs_full=bs_full, ones4=const['ones4'])
        br = _branches(y_scan, gate_a, bonus, bcd, bp)
        x = _merge(x, mod_l, g1, br, w_in[l, :, N_SMALL:].astype(BF16), w_branch[l].astype(BF16),
                   w_out[l].astype(BF16))

        wr_pad = jnp.concatenate([w_router[l], jnp.zeros((D, LANE - N_EXPERTS), F32)], axis=1)
        h2, aff = _router(x, mod_l, norm2_g[l].reshape(1, D), wr_pad)
        aff_t_ctx = jnp.swapaxes(aff[:T_CTX, :N_EXPERTS].reshape(N_CTX_SEQ, L_CTX, N_EXPERTS), 1, 2)
        aff_t_lat = jnp.swapaxes(aff[T_CTX:, :N_EXPERTS].reshape(N_LAT_SEQ, L_LAT, N_EXPERTS), 1, 2)
        rank_ctx = _ranks(L_CTX, N_CTX_SEQ, 0, aff, aff_t_ctx)
        rank_lat = _ranks(L_LAT, N_LAT_SEQ, N_CTX_BLK, aff, aff_t_lat)
        rank_t = jnp.concatenate([rank_ctx[:, :N_EXPERTS], rank_lat[:, :N_EXPERTS]], axis=0).T
        oe = _experts(l, h2, rank_t.reshape(N_EXPERTS, 1, T_ALL), w_e1, w_e3, w_e2)
        final = l == DEPTH - 1
        y_ctx = _combine(L_CTX, N_CTX_SEQ, 0, CAP_CTX, 0, final, x, mod_l, aff, rank_ctx, oe, final_g2)
        y_lat = _combine(L_LAT, N_LAT_SEQ, N_CTX_BLK, CAP_LAT, N_CTX_SEQ * CAP_CTX // CAP_LAT, final, x, mod_l,
                         aff, rank_lat, oe, final_g2)
        if not final:
            x = jnp.concatenate([y_ctx, y_lat], axis=0)
    y_prompt = y_ctx.reshape(N_CTX_SEQ, L_CTX, D)
    y_sample = y_lat.reshape(N_LAT_SEQ, L_LAT, D)
    new_state = jnp.stack(ctx_states, axis=1)
    return (y_prompt, y_sample, new_state)
```

```python
import functools
import math

import numpy as np
import jax
import jax.numpy as jnp
from jax import lax
from jax.experimental import pallas as pl
from jax.experimental.pallas import tpu as pltpu

F32 = jnp.float32
BF16 = jnp.bfloat16

D = 1024
N_CTX_SEQ, L_CTX = 16, 256
N_LAT_SEQ, L_LAT = 2, 1024
T_CTX = N_CTX_SEQ * L_CTX
T_LAT = N_LAT_SEQ * L_LAT
T_ALL = T_CTX + T_LAT
DEPTH = 2
TM = 256
TM_WIDE = 512
N_BLK = T_ALL // TM
N_CTX_BLK = T_CTX // TM
LAT_BLK_PER_SEQ = L_LAT // TM
GRID_W = 64
W = 256
HEAD = 64
N_HEADS = W // HEAD
LORA = 64
LORA_G = 128
DECAY_SCALE = math.exp(-0.5)
POOL_WINDOWS = (2, 4, 8, 16)
CHUNK = 128
N_EXPERTS = 16
FF = 1024
CAP_CTX = 2 * L_CTX // N_EXPERTS
CAP_LAT = 2 * L_LAT // N_EXPERTS
SLOTS = N_CTX_SEQ * CAP_CTX + N_LAT_SEQ * CAP_LAT
N_MOD = 6
NORM_EPS = 1e-6
GN_EPS = 64e-5
N_SMALL = 3 * W + 2 * LORA + 2 * LORA + LORA_G + 6 * W
LANE = 128
SUBLANE = 8
VMEM_LIMIT = 56 * 1024 * 1024


def _cparams(n_axes):
    return pltpu.CompilerParams(dimension_semantics=("arbitrary",) * n_axes,
                                vmem_limit_bytes=VMEM_LIMIT)


def _split2(x):
    hi = x.astype(BF16)
    lo = (x - hi.astype(F32)).astype(BF16)
    return hi, lo


def _split3(x):
    hi = x.astype(BF16)
    r = x - hi.astype(F32)
    mid = r.astype(BF16)
    lo = (r - mid.astype(F32)).astype(BF16)
    return hi, mid, lo


def _dot(a, b):
    return jnp.dot(a, b, preferred_element_type=F32)


def _sel_dot_l(m_bf16, x, parts=2):
    ps = _split2(x) if parts == 2 else _split3(x)
    acc = _dot(m_bf16, ps[0])
    for p in ps[1:]:
        acc = acc + _dot(m_bf16, p)
    return acc


def _sel_dot_r(x, m_bf16, parts=2):
    ps = _split2(x) if parts == 2 else _split3(x)
    acc = _dot(ps[0], m_bf16)
    for p in ps[1:]:
        acc = acc + _dot(p, m_bf16)
    return acc


def _dot_hl(a, b):
    ah, al = _split2(a)
    bh, bl = _split2(b)
    return _dot(ah, bh) + _dot(al, bh) + _dot(ah, bl)


def _sigmoid(x):
    return 1.0 / (1.0 + jnp.exp(-x))


def _gelu_tanh(x):
    return 0.5 * x * (1.0 + jnp.tanh(math.sqrt(2.0 / math.pi) * (x + 0.044715 * (x * x * x))))


def _norm_mod(x, g, scale, shift):
    ms = jnp.mean(x * x, axis=-1, keepdims=True)
    return (x * lax.rsqrt(ms + NORM_EPS) * g) * (1.0 + scale) + shift


def _mod_row(i, tm=TM):
    return jnp.where(i < T_CTX // tm, 0, (i - T_CTX // tm) // (L_LAT // tm) + 1)


def _mod_kernel(c_ref, w_ref, b_ref, o_ref):
    c = c_ref[...]
    s = c * _sigmoid(c)
    o_ref[...] = _dot(s.astype(BF16), w_ref[...].astype(BF16)) + b_ref[...]


def _modulation(cond8, w_mod, b_mod):
    tn = 1536
    n = N_MOD * D
    return pl.pallas_call(
        _mod_kernel,
        grid=(DEPTH, n // tn),
        in_specs=[pl.BlockSpec((SUBLANE, D), lambda l, j: (0, 0)),
                  pl.BlockSpec((None, D, tn), lambda l, j: (l, 0, j)),
                  pl.BlockSpec((None, 1, tn), lambda l, j: (l, 0, j))],
        out_specs=pl.BlockSpec((None, SUBLANE, tn), lambda l, j: (l, 0, j)),
        out_shape=jax.ShapeDtypeStruct((DEPTH, SUBLANE, n), F32),
        compiler_params=_cparams(2),
        name="modulation",
    )(cond8, w_mod, b_mod.reshape(DEPTH, 1, n))


def _inproj_kernel(x_ref, mod_ref, g_ref, w_ref, wup_ref, aup_ref, gup_ref, w0_ref, a0_ref, kk_k_ref, k_a_ref,
                   r_k_ref, ones4_ref, rkv_ref, bcd_ref, wdec_ref, b_ref, kin_ref, kk_ref, gate_ref, bonus_ref):
    h = _norm_mod(x_ref[...], g_ref[...], mod_ref[0, 1:2, :], mod_ref[0, 0:1, :])
    z = _dot(h.astype(BF16), w_ref[...])
    rkv_ref[...] = z[:, :3 * W]
    bcd_ref[...] = z[:, 3 * W + 3 * LANE:]
    r = z[:, 0:W]
    k = z[:, W:2 * W]
    v = z[:, 2 * W:3 * W]
    xw = z[:, 3 * W:3 * W + LANE]
    xa = z[:, 3 * W + LANE:3 * W + 2 * LANE]
    xg = z[:, 3 * W + 2 * LANE:3 * W + 3 * LANE]
    ones4 = ones4_ref[...]
    dec = w0_ref[...] + _dot_hl(jnp.tanh(xw), wup_ref[...])
    wdec_ref[...] = jnp.exp(-DECAY_SCALE * _sigmoid(dec))
    a = _sigmoid(a0_ref[...] + _dot_hl(xa, aup_ref[...]))
    gate_ref[...] = _dot_hl(_sigmoid(xg), gup_ref[...])
    kk = k * kk_k_ref[...]
    n2 = _sel_dot_r(kk * kk, ones4, parts=3)
    kk = kk / jnp.maximum(jnp.sqrt(n2), 1e-12)
    kk_ref[...] = kk
    k_a = k_a_ref[...]
    kin0 = k * (1.0 + (a[:, 0:W] - 1.0) * k_a)
    kin1 = k * (1.0 + (a[:, W:2 * W] - 1.0) * k_a)
    kin_ref[:, 0:W] = kin0
    kin_ref[:, W:2 * W] = kin1
    b_ref[:, 0:W] = a[:, 0:W] * kk
    b_ref[:, W:2 * W] = a[:, W:2 * W] * kk
    bonus = _sel_dot_r(r * (kin0 + kin1) * r_k_ref[...], ones4, parts=3)
    bonus_ref[...] = bonus * v


def _in_projection(x, mod_l, g1, w_small, wup_bd, aup_bd, g_up, w0, a0, k_k, k_a, r_k, ones4):
    tok = lambda c: pl.BlockSpec((TM, c), lambda i: (i, 0))
    full = lambda a: pl.BlockSpec(a.shape, lambda i: (0,) * a.ndim)
    params = (w_small, wup_bd, aup_bd, g_up, w0, a0, k_k, k_a, r_k, ones4)
    widths = (3 * W, 6 * W, 2 * W, 2 * W, 2 * W, W, W, W)
    return pl.pallas_call(
        _inproj_kernel,
        grid=(N_BLK,),
        in_specs=[tok(D), pl.BlockSpec((1, N_MOD, D), lambda i: (_mod_row(i), 0, 0)), full(g1)]
                 + [full(a) for a in params],
        out_specs=[tok(c) for c in widths],
        out_shape=[jax.ShapeDtypeStruct((T_ALL, c), F32) for c in widths],
        compiler_params=_cparams(1),
        name="in_projection",
    )(x, mod_l, g1, *params)


T_STEP = SUBLANE
CHAINS_PER_DOT = 6
CTX_SEQ_PER_SCAN_BLK = 4
N_SCAN_BLK = N_CTX_SEQ // CTX_SEQ_PER_SCAN_BLK
LAT_WIN = L_LAT // N_SCAN_BLK
SCAN_FIELDS = ('r', 'v', 'kk', 'w', 'b', 'k')


def _scan_kernel(*refs):
    n_f = len(SCAN_FIELDS)
    ctx_refs = dict(zip(SCAN_FIELDS, refs[:n_f]))
    lat_refs = {}
    pos = n_f
    for s in range(N_LAT_SEQ):
        for d in range(2):
            lat_refs[(s, d)] = dict(zip(SCAN_FIELDS, refs[pos:pos + n_f]))
            pos += n_f
    s0c_ref, s0l_ref, ones4_ref, eye4_ref, yc_ref, sfin_ref, yl_ref, slat_scr = refs[pos:]
    step_i = pl.program_id(0)
    ones4 = ones4_ref[...]
    eye4 = eye4_ref[...]
    chains = ([(False, b, d) for b in range(CTX_SEQ_PER_SCAN_BLK) for d in range(2)]
              + [(True, s, d) for s in range(N_LAT_SEQ) for d in range(2)])
    n_chain = len(chains)
    groups = [chains[g:g + CHAINS_PER_DOT] for g in range(0, n_chain, CHAINS_PER_DOT)]

    @pl.when(step_i == 0)
    def _():
        yl_ref[...] = jnp.zeros_like(yl_ref)
        for s in range(N_LAT_SEQ):
            for d in range(2):
                slat_scr[s, d] = s0l_ref[s, d]

    yc_ref[...] = jnp.zeros_like(yc_ref)

    def row_sum(blocks):
        return _dot(jnp.concatenate(blocks, axis=0).astype(BF16), ones4)

    def body(tb, state):
        state = list(state)
        tiles = []
        for lat, b, d in chains:
            t0 = tb * T_STEP if d == 0 else L_CTX - T_STEP - tb * T_STEP
            if lat:
                rows = pl.ds(pl.multiple_of(t0, T_STEP), T_STEP)
                tiles.append({f: lat_refs[(b, d)][f][rows, :] for f in SCAN_FIELDS})
            else:
                rows = pl.ds(pl.multiple_of(b * L_CTX + t0, T_STEP), T_STEP)
                col = lambda f: slice(d * W, (d + 1) * W) if f in ('w', 'b', 'k') else slice(0, W)
                tiles.append({f: ctx_refs[f][rows, col(f)] for f in SCAN_FIELDS})

        def row(name, c, j):
            jj = j if chains[c][2] == 0 else T_STEP - 1 - j
            return tiles[c][name][jj:jj + 1, :]

        y_rows = [[None] * T_STEP for _ in chains]
        for j in range(T_STEP):
            v_cols = row_sum([eye4 * row('v', c, j) for c in range(n_chain)])
            q = [None] * n_chain
            for grp in groups:
                cs = [chains.index(ch) for ch in grp]
                s_kk = row_sum([state[c] * row('kk', c, j) for c in cs])
                for i, c in enumerate(cs):
                    s = (state[c] * row('w', c, j) - s_kk[i * HEAD:(i + 1) * HEAD] * row('b', c, j)
                         + v_cols[c * HEAD:(c + 1) * HEAD] * row('k', c, j))
                    state[c] = s
                    q[c] = s * row('r', c, j)
            y_cols = row_sum(q)
            for c, (lat, b, d) in enumerate(chains):
                jj = j if d == 0 else T_STEP - 1 - j
                y_rows[c][jj] = jnp.sum(eye4 * y_cols[c * HEAD:(c + 1) * HEAD], axis=0, keepdims=True)
        for c, (lat, b, d) in enumerate(chains):
            t0 = tb * T_STEP if d == 0 else L_CTX - T_STEP - tb * T_STEP
            y_tile = jnp.concatenate(y_rows[c], axis=0)
            if lat:
                win0 = step_i * LAT_WIN if d == 0 else L_LAT - LAT_WIN - step_i * LAT_WIN
                rows = pl.ds(pl.multiple_of(b * L_LAT + win0 + t0, T_STEP), T_STEP)
                yl_ref[rows, :] += y_tile
            else:
                rows = pl.ds(pl.multiple_of(b * L_CTX + t0, T_STEP), T_STEP)
                yc_ref[rows, :] += y_tile
        return tuple(state)

    init = tuple(slat_scr[b, d] if lat else s0c_ref[b, d] for (lat, b, d) in chains)
    fin = lax.fori_loop(0, L_CTX // T_STEP, body, init)
    for c, (lat, b, d) in enumerate(chains):
        if lat:
            slat_scr[b, d] = fin[c]
        else:
            sfin_ref[b, d] = fin[c]


def _rwkv_scan(rkv, kk, wdec, bvec, kin, s0_ctx, s0_lat, ones4, eye4):
    assert LAT_WIN == L_CTX
    arrays = dict(r=(rkv, 0), v=(rkv, 2), kk=(kk, 0), w=(wdec, None), b=(bvec, None), k=(kin, None))
    rows_c = CTX_SEQ_PER_SCAN_BLK * L_CTX
    operands, in_specs = [], []
    for f in SCAN_FIELDS:
        a, col = arrays[f]
        operands.append(a)
        in_specs.append(pl.BlockSpec((rows_c, W if col is not None else 2 * W),
                                     lambda i, col=col: (i, col if col is not None else 0)))
    win_blk0 = T_CTX // LAT_WIN
    for s in range(N_LAT_SEQ):
        for d in range(2):
            for f in SCAN_FIELDS:
                a, col = arrays[f]
                operands.append(a)
                in_specs.append(pl.BlockSpec(
                    (LAT_WIN, W),
                    lambda i, s=s, d=d, col=col: (win_blk0 + s * N_SCAN_BLK + (i if d == 0 else N_SCAN_BLK - 1 - i),
                                                  col if col is not None else d)))
    st = pl.BlockSpec((None, CTX_SEQ_PER_SCAN_BLK, 2, HEAD, W), lambda i: (i, 0, 0, 0, 0))
    full = lambda a: pl.BlockSpec(a.shape, lambda i: (0,) * a.ndim)
    operands += [s0_ctx, s0_lat, ones4, eye4]
    in_specs += [st, full(s0_lat), full(ones4), full(eye4)]
    return pl.pallas_call(
        _scan_kernel,
        grid=(N_SCAN_BLK,),
        in_specs=in_specs,
        out_specs=[pl.BlockSpec((rows_c, W), lambda i: (i, 0)), st, pl.BlockSpec((T_LAT, W), lambda i: (0, 0))],
        out_shape=[jax.ShapeDtypeStruct((T_CTX, W), F32),
                   jax.ShapeDtypeStruct((N_SCAN_BLK, CTX_SEQ_PER_SCAN_BLK, 2, HEAD, W), F32),
                   jax.ShapeDtypeStruct((T_LAT, W), F32)],
        scratch_shapes=[pltpu.VMEM((N_LAT_SEQ, 2, HEAD, W), F32)],
        compiler_params=_cparams(1),
        name="rwkv_scan",
    )(*operands)


def _branch_kernel(y_ref, g_ref, bonus_ref, bcd_ref, lnx_g_ref, lnx_b_ref, band_ref, invcnt_ref, shift_ref,
                   wpool_ref, pscale_ref, conv_w_ref, conv_b_ref, sgu_g_ref, ws_ref, bs_ref, ones4_ref, br_ref):
    ones4 = ones4_ref[...]
    group = lax.broadcasted_iota(jnp.int32, (TM, W), 1) // HEAD

    def seg_mean(x):
        return _sel_dot_r(x, ones4, parts=2) * (1.0 / HEAD)

    y = y_ref[...]
    mu = seg_mean(y)
    yc = y - mu
    var = seg_mean(yc * yc)
    ya = yc * lax.rsqrt(var + GN_EPS) * lnx_g_ref[...] + lnx_b_ref[...] + bonus_ref[...]
    br_ref[:, 0:W] = (ya * g_ref[...]).astype(BF16)

    u = bcd_ref[:, 0:W]
    u_parts = _split2(u)
    win = jnp.zeros((TM, W), F32)
    for gi in range(len(POOL_WINDOWS)):
        band = band_ref[0, gi]
        s = _dot(band, u_parts[0]) + _dot(band, u_parts[1])
        win = jnp.where(group == gi, s, win)
    pooled = win * invcnt_ref[0] - u
    br_ref[:, W:2 * W] = (_dot_hl(pooled, wpool_ref[...]) * pscale_ref[...]).astype(BF16)

    cin = bcd_ref[:, W:2 * W]
    cb = bcd_ref[:, 2 * W:3 * W]
    cc = bcd_ref[:, 3 * W:4 * W]
    t = cc * cin
    t_prev = _sel_dot_l(shift_ref[0, 0], t, parts=2)
    t_next = _sel_dot_l(shift_ref[0, 1], t, parts=2)
    conv = conv_b_ref[...] + t_prev * conv_w_ref[0:1, :]
    conv = conv + t * conv_w_ref[1:2, :]
    conv = conv + t_next * conv_w_ref[2:3, :]
    br_ref[:, 2 * W:3 * W] = (cb * conv).astype(BF16)

    du = _gelu_tanh(bcd_ref[:, 4 * W:5 * W])
    dv = _gelu_tanh(bcd_ref[:, 5 * W:6 * W])
    mu = seg_mean(dv)
    dc = dv - mu
    var = seg_mean(dc * dc)
    vn = dc * lax.rsqrt(var + NORM_EPS) * sgu_g_ref[...]
    group_c = lax.broadcasted_iota(jnp.int32, (CHUNK, W), 1) // HEAD
    for ch in range(TM // CHUNK):
        rows = slice(ch * CHUNK, (ch + 1) * CHUNK)
        vh, vl = _split2(vn[rows, :])
        s = jnp.zeros((CHUNK, W), F32)
        for gi in range(N_HEADS):
            wh, wl = _split2(ws_ref[gi])
            sg = _dot(wh, vh) + _dot(wl, vh) + _dot(wh, vl)
            s = jnp.where(group_c == gi, sg, s)
        br_ref[rows, 3 * W:4 * W] = (du[rows, :] * (s + bs_ref[...])).astype(BF16)


def _branches(y, g, bonus, bcd, p):
    tok = lambda c: pl.BlockSpec((TM, c), lambda i: (i, 0))
    full = lambda a: pl.BlockSpec(a.shape, lambda i: (0,) * a.ndim)
    lay = lambda a: pl.BlockSpec((1,) + a.shape[1:], lambda i: (jnp.where(i < N_CTX_BLK, 0, 1),) + (0,) * (a.ndim - 1))
    return pl.pallas_call(
        _branch_kernel,
        grid=(N_BLK,),
        in_specs=[tok(W), tok(W), tok(W), tok(6 * W), full(p['lnx_g']), full(p['lnx_b']),
                  lay(p['band']), lay(p['invcnt']), lay(p['shift']),
                  full(p['wpool_bd']), full(p['pool_scale']), full(p['conv_w']), full(p['conv_b']),
                  full(p['sgu_g']), full(p['w_s']), full(p['bs_full']), full(p['ones4'])],
        out_specs=tok(4 * W),
        out_shape=jax.ShapeDtypeStruct((T_ALL, 4 * W), BF16),
        compiler_params=_cparams(1),
        name="branches",
    )(y, g, bonus, bcd, p['lnx_g'], p['lnx_b'], p['band'], p['invcnt'], p['shift'], p['wpool_bd'],
      p['pool_scale'], p['conv_w'], p['conv_b'], p['sgu_g'], p['w_s'], p['bs_full'], p['ones4'])


def _merge_kernel(x_ref, mod_ref, g_ref, br_ref, wgl_ref, wbr_ref, wout_ref, o_ref):
    x = x_ref[...]
    h = _norm_mod(x, g_ref[...], mod_ref[0, 1:2, :], mod_ref[0, 0:1, :]).astype(BF16)
    merged = jnp.zeros(x.shape, F32)
    for i in range(4):
        gl = _dot(h, wgl_ref[:, i * D:(i + 1) * D])
        proj = _dot(br_ref[:, i * W:(i + 1) * W], wbr_ref[i])
        merged = merged + _sigmoid(gl) * proj
    mix = _dot(merged.astype(BF16), wout_ref[...])
    o_ref[...] = x + mod_ref[0, 2:3, :] * mix


def _merge(x, mod_l, g1, br, w_gl, w_branch, w_out):
    full = lambda a: pl.BlockSpec(a.shape, lambda i: (0,) * a.ndim)
    tm = TM_WIDE
    return pl.pallas_call(
        _merge_kernel,
        grid=(T_ALL // tm,),
        in_specs=[pl.BlockSpec((tm, D), lambda i: (i, 0)),
                  pl.BlockSpec((1, N_MOD, D), lambda i: (_mod_row(i, tm), 0, 0)),
                  pl.BlockSpec((1, D), lambda i: (0, 0)),
                  pl.BlockSpec((tm, 4 * W), lambda i: (i, 0)),
                  full(w_gl), full(w_branch), full(w_out)],
        out_specs=pl.BlockSpec((tm, D), lambda i: (i, 0)),
        out_shape=jax.ShapeDtypeStruct((T_ALL, D), F32),
        compiler_params=_cparams(1),
        name="merge",
    )(x, mod_l, g1, br, w_gl, w_branch, w_out)


def _router_kernel(x_ref, mod_ref, g_ref, wr_ref, h_ref, aff_ref):
    h = _norm_mod(x_ref[...], g_ref[...], mod_ref[0, 4:5, :], mod_ref[0, 3:4, :])
    h_ref[...] = h.astype(BF16)
    logits = _dot_hl(h, wr_ref[...])
    lane = lax.broadcasted_iota(jnp.int32, (TM, LANE), 1)
    logits = jnp.where(lane < N_EXPERTS, logits, -1e30)
    m = jnp.max(logits, axis=-1, keepdims=True)
    e = jnp.where(lane < N_EXPERTS, jnp.exp(logits - m), 0.0)
    aff_ref[...] = e / jnp.sum(e, axis=-1, keepdims=True)


def _router(x, mod_l, g2, w_router_pad):
    return pl.pallas_call(
        _router_kernel,
        grid=(N_BLK,),
        in_specs=[pl.BlockSpec((TM, D), lambda i: (i, 0)),
                  pl.BlockSpec((1, N_MOD, D), lambda i: (_mod_row(i), 0, 0)),
                  pl.BlockSpec((1, D), lambda i: (0, 0)),
                  pl.BlockSpec((D, LANE), lambda i: (0, 0))],
        out_specs=[pl.BlockSpec((TM, D), lambda i: (i, 0)), pl.BlockSpec((TM, LANE), lambda i: (i, 0))],
        out_shape=[jax.ShapeDtypeStruct((T_ALL, D), BF16), jax.ShapeDtypeStruct((T_ALL, LANE), F32)],
        compiler_params=_cparams(1),
        name="router",
    )(x, mod_l, g2, w_router_pad)


def _rank_kernel(seq_len, aff_ref, afft_ref, ones_ref, rank_ref):
    nb = seq_len // TM
    earlier = jnp.where(lax.broadcasted_iota(jnp.int32, (TM, TM), 1) < lax.broadcasted_iota(jnp.int32, (TM, TM), 0),
                        1.0, 0.0)
    lane = lax.broadcasted_iota(jnp.int32, (TM, LANE), 1)
    for j in range(nb):
        aff = aff_ref[j * TM:(j + 1) * TM, :]
        rank = jnp.zeros((TM, LANE), F32)
        for e in range(N_EXPERTS):
            mine = aff[:, e:e+1]
            tiles = []
            for c in range(nb):
                other = afft_ref[e:e+1, c * TM:(c + 1) * TM]
                if c < j:
                    tiles.append(jnp.where(other >= mine, 1.0, 0.0))
                elif c > j:
                    tiles.append(jnp.where(other > mine, 1.0, 0.0))
                else:
                    tiles.append(jnp.where(other > mine, 1.0, jnp.where(other == mine, earlier, 0.0)))
            cnt = _dot(jnp.concatenate(tiles, axis=1).astype(BF16), ones_ref[...])
            rank = jnp.where(lane == e, cnt, rank)
        rank_ref[j * TM:(j + 1) * TM, :] = rank


def _ranks(seq_len, n_seq, blk0, aff, aff_t):
    nb = seq_len // TM
    ones = jnp.ones((seq_len, LANE), BF16)
    return pl.pallas_call(
        functools.partial(_rank_kernel, seq_len),
        grid=(n_seq,),
        in_specs=[pl.BlockSpec((seq_len, LANE), lambda s: (blk0 // nb + s, 0)),
                  pl.BlockSpec((None, N_EXPERTS, seq_len), lambda s: (s, 0, 0)),
                  pl.BlockSpec((seq_len, LANE), lambda s: (0, 0))],
        out_specs=pl.BlockSpec((seq_len, LANE), lambda s: (s, 0)),
        out_shape=jax.ShapeDtypeStruct((n_seq * seq_len, LANE), F32),
        compiler_params=_cparams(1),
        name=f"ranks_{seq_len}",
    )(aff, aff_t, ones)


def _expert_kernel(h_ref, rank_ref, w1_ref, w3_ref, w2_ref, o_ref, xs_scr):
    def gather(seq_len, cap, tok0, slot0):
        slot = lax.broadcasted_iota(jnp.int32, (cap, seq_len), 0).astype(F32)
        rk = rank_ref[:, tok0:tok0 + seq_len]
        onehot = jnp.where(rk == slot, 1.0, 0.0).astype(BF16)
        xs_scr[slot0:slot0 + cap, :] = _dot(onehot, h_ref[tok0:tok0 + seq_len, :]).astype(BF16)

    for s in range(N_CTX_SEQ):
        gather(L_CTX, CAP_CTX, s * L_CTX, s * CAP_CTX)
    for s in range(N_LAT_SEQ):
        gather(L_LAT, CAP_LAT, T_CTX + s * L_LAT, N_CTX_SEQ * CAP_CTX + s * CAP_LAT)
    xs = xs_scr[...]
    a = _dot(xs, w1_ref[...].astype(BF16))
    act = (a * _sigmoid(a)) * _dot(xs, w3_ref[...].astype(BF16))
    o_ref[...] = _dot(act.astype(BF16), w2_ref[...].astype(BF16)).astype(BF16)


def _experts(layer, h2, rank_t, w1, w3, w2):
    wspec = lambda: pl.BlockSpec((None, None, D, FF), lambda e: (layer, e, 0, 0))
    return pl.pallas_call(
        _expert_kernel,
        grid=(N_EXPERTS,),
        in_specs=[pl.BlockSpec((T_ALL, D), lambda e: (0, 0), pipeline_mode=pl.Buffered(1)),
                  pl.BlockSpec((None, 1, T_ALL), lambda e: (e, 0, 0)),
                  wspec(), wspec(), pl.BlockSpec((None, None, FF, D), lambda e: (layer, e, 0, 0))],
        out_specs=pl.BlockSpec((None, SLOTS, D), lambda e: (e, 0, 0)),
        out_shape=jax.ShapeDtypeStruct((N_EXPERTS, SLOTS, D), BF16),
        scratch_shapes=[pltpu.VMEM((SLOTS, D), BF16)],
        compiler_params=_cparams(1),
        name="experts",
    )(h2, rank_t, w1, w3, w2)


def _combine_kernel(cap, final, x_ref, mod_ref, aff_ref, rank_ref, oe_ref, expand_ref, fg_ref, o_ref):
    n = N_EXPERTS * cap
    expand = expand_ref[...]
    rank_x = _sel_dot_r(rank_ref[...], expand, parts=2)
    aff_x = _dot(aff_ref[...].astype(BF16), expand)
    slot = jnp.bitwise_and(lax.broadcasted_iota(jnp.int32, (TM, n), 1), cap - 1).astype(F32)
    gate = jnp.where(rank_x == slot, aff_x, 0.0)
    acc = _dot(gate.astype(BF16), oe_ref[...].reshape(n, D))
    x = x_ref[...] + mod_ref[0, 5:6, :] * acc
    if final:
        ms = jnp.mean(x * x, axis=-1, keepdims=True)
        x = x * lax.rsqrt(ms + NORM_EPS) * fg_ref[...]
    o_ref[...] = x


def _combine(seq_len, n_seq, blk0, cap, slot_blk0, final, x, mod_l, aff, rank, oe, final_g):
    nb = seq_len // TM
    assert cap & (cap - 1) == 0
    tok = lambda c: pl.BlockSpec((TM, c), lambda s, j: (blk0 + s * nb + j, 0))
    expand = jnp.asarray(np.arange(LANE)[:, None] == np.arange(N_EXPERTS * cap)[None, :] // cap, BF16)
    return pl.pallas_call(
        functools.partial(_combine_kernel, cap, final),
        grid=(n_seq, nb),
        in_specs=[tok(D),
                  pl.BlockSpec((1, N_MOD, D), lambda s, j: (_mod_row(blk0 + s * nb + j), 0, 0)),
                  tok(LANE),
                  pl.BlockSpec((TM, LANE), lambda s, j: (s * nb + j, 0)),
                  pl.BlockSpec((N_EXPERTS, cap, D), lambda s, j: (0, slot_blk0 + s, 0)),
                  pl.BlockSpec((LANE, N_EXPERTS * cap), lambda s, j: (0, 0)),
                  pl.BlockSpec((1, D), lambda s, j: (0, 0))],
        out_specs=pl.BlockSpec((TM, D), lambda s, j: (s * nb + j, 0)),
        out_shape=jax.ShapeDtypeStruct((n_seq * seq_len, D), F32),
        compiler_params=_cparams(2),
        name=f"combine_{seq_len}",
    )(x, mod_l, aff, rank, oe, expand, final_g)


def _row_structure(row_len):
    t = np.arange(TM)
    same_row = (t[:, None] // row_len) == (t[None, :] // row_len)
    delta = t[None, :] - t[:, None]
    band = np.stack([same_row & (delta >= -(w // 2)) & (delta < w // 2) for w in POOL_WINDOWS]).astype(np.float32)
    cnt = band.sum(-1)
    invcnt = np.repeat((1.0 / cnt).T, HEAD, axis=1).astype(np.float32)
    shift = np.stack([same_row & (delta == -1), same_row & (delta == 1)]).astype(np.float32)
    return band, invcnt, shift


def _constants():
    lane = np.arange(W)
    ones4 = (lane[:, None] // HEAD == lane[None, :] // HEAD).astype(np.float32)
    eye4 = (np.arange(HEAD)[:, None] == lane[None, :] % HEAD).astype(np.float32)
    structs = [_row_structure(L_CTX), _row_structure(GRID_W)]
    return dict(
        ones4=jnp.asarray(ones4, BF16), eye4=jnp.asarray(eye4, F32),
        band=jnp.asarray(np.stack([s[0] for s in structs]), BF16),
        invcnt=jnp.asarray(np.stack([s[1] for s in structs]), F32),
        shift=jnp.asarray(np.stack([s[2] for s in structs]), BF16))


def _block_diag(blocks):
    n = len(blocks)
    rows = []
    for i, b in enumerate(blocks):
        rows.append(jnp.concatenate([b if j == i else jnp.zeros((b.shape[0], blocks[j].shape[1]), b.dtype)
                                     for j in range(n)], axis=1))
    return jnp.concatenate(rows, axis=0)


def _wide_tiles(s):
    return jnp.moveaxis(s, -3, -2).reshape(s.shape[:-3] + (HEAD, W))


def _head_tiles(s):
    return jnp.moveaxis(s.reshape(s.shape[:-2] + (HEAD, N_HEADS, HEAD)), -2, -3)


def kernel(x_prompt, x_sample, state_rwkv, c, c_ctx, norm1_g, w_mod, b_mod, w_in, w0, w_up, a0, a_up, g_up, k_k, k_a, r_k, lnx_g, lnx_b, w_pool, pool_scale, conv_w, conv_b, sgu_g, w_s, b_s, w_branch, w_out, norm2_g, w_router, w_e1, w_e3, w_e2, final_g):
    const = _constants()
    x = jnp.concatenate([x_prompt.reshape(T_CTX, D), x_sample.reshape(T_LAT, D)], axis=0)
    cond8 = jnp.concatenate([c_ctx[None], c, jnp.zeros((SUBLANE - 1 - N_LAT_SEQ, D), F32)], axis=0)
    mod = _modulation(cond8, w_mod, b_mod).reshape(DEPTH, SUBLANE, N_MOD, D)
    final_g2 = final_g.reshape(1, D)
    ctx_states = []
    y_ctx = y_lat = None
    for l in range(DEPTH):
        mod_l = mod[l]
        g1 = norm1_g[l].reshape(1, D)
        wup_bd = _block_diag([w_up[l, 0], w_up[l, 1]])
        aup_bd = _block_diag([a_up[l, 0], a_up[l, 1]])
        rkv, bcd, wdec, bvec, kin, kk, gate_a, bonus = _in_projection(
            x, mod_l, g1, w_in[l, :, :N_SMALL].astype(BF16), wup_bd, aup_bd, g_up[l], w0[l].reshape(1, 2 * W),
            a0[l].reshape(1, 2 * W), k_k[l].reshape(1, W), k_a[l].reshape(1, W), r_k[l].reshape(1, W),
            const['ones4'])

        s0_ctx = jnp.zeros((N_SCAN_BLK, CTX_SEQ_PER_SCAN_BLK, 2, HEAD, W), F32)
        y_c, s_ctx, y_l = _rwkv_scan(rkv, kk, wdec, bvec, kin, s0_ctx, _wide_tiles(state_rwkv[:, l]),
                                     const['ones4'], const['eye4'])
        ctx_states.append(_head_tiles(s_ctx).reshape(N_CTX_SEQ, 2, N_HEADS, HEAD, HEAD))
        y_scan = jnp.concatenate([y_c, y_l], axis=0)

        bs_full = jnp.repeat(b_s[l].T, HEAD, axis=1)
        bp = dict(lnx_g=lnx_g[l].reshape(1, W), lnx_b=lnx_b[l].reshape(1, W), band=const['band'],
                  invcnt=const['invcnt'], shift=const['shift'],
                  wpool_bd=_block_diag([w_pool[l, i] for i in range(len(POOL_WINDOWS))]),
                  pool_scale=pool_scale[l].reshape(1, W), conv_w=conv_w[l], conv_b=conv_b[l].reshape(1, W),
                  sgu_g=sgu_g[l].reshape(1, W), w_s=w_s[l], bs_full=bs_full, ones4=const['ones4'])
        br = _branches(y_scan, gate_a, bonus, bcd, bp)
        x = _merge(x, mod_l, g1, br, w_in[l, :, N_SMALL:].astype(BF16), w_branch[l].astype(BF16),
                   w_out[l].astype(BF16))

        wr_pad = jnp.concatenate([w_router[l], jnp.zeros((D, LANE - N_EXPERTS), F32)], axis=1)
        h2, aff = _router(x, mod_l, norm2_g[l].reshape(1, D), wr_pad)
        aff_t_ctx = jnp.swapaxes(aff[:T_CTX, :N_EXPERTS].reshape(N_CTX_SEQ, L_CTX, N_EXPERTS), 1, 2)
        aff_t_lat = jnp.swapaxes(aff[T_CTX:, :N_EXPERTS].reshape(N_LAT_SEQ, L_LAT, N_EXPERTS), 1, 2)
        rank_ctx = _ranks(L_CTX, N_CTX_SEQ, 0, aff, aff_t_ctx)
        rank_lat = _ranks(L_LAT, N_LAT_SEQ, N_CTX_BLK, aff, aff_t_lat)
        rank_t = jnp.concatenate([rank_ctx[:, :N_EXPERTS], rank_lat[:, :N_EXPERTS]], axis=0).T
        oe = _experts(l, h2, rank_t.reshape(N_EXPERTS, 1, T_ALL), w_e1, w_e3, w_e2)
        final = l == DEPTH - 1
        y_ctx = _combine(L_CTX, N_CTX_SEQ, 0, CAP_CTX, 0, final, x, mod_l, aff, rank_ctx, oe, final_g2)
        y_lat = _combine(L_LAT, N_LAT_SEQ, N_CTX_BLK, CAP_LAT, N_CTX_SEQ * CAP_CTX // CAP_LAT, final, x, mod_l,
                         aff, rank_lat, oe, final_g2)
        if not final:
            x = jnp.concatenate([y_ctx, y_lat], axis=0)
    y_prompt = y_ctx.reshape(N_CTX_SEQ, L_CTX, D)
    y_sample = y_lat.reshape(N_LAT_SEQ, L_LAT, D)
    new_state = jnp.stack(ctx_states, axis=1)
    return (y_prompt, y_sample, new_state)
```

```python
import functools
import math

import numpy as np
import jax
import jax.numpy as jnp
from jax import lax
from jax.experimental import pallas as pl
from jax.experimental.pallas import tpu as pltpu

F32 = jnp.float32
BF16 = jnp.bfloat16

D = 1024
N_CTX_SEQ, L_CTX = 16, 256
N_LAT_SEQ, L_LAT = 2, 1024
T_CTX = N_CTX_SEQ * L_CTX
T_LAT = N_LAT_SEQ * L_LAT
T_ALL = T_CTX + T_LAT
DEPTH = 2
TM = 256
TM_WIDE = 512
N_BLK = T_ALL // TM
N_CTX_BLK = T_CTX // TM
LAT_BLK_PER_SEQ = L_LAT // TM
GRID_W = 64
W = 256
HEAD = 64
N_HEADS = W // HEAD
LORA = 64
LORA_G = 128
DECAY_SCALE = math.exp(-0.5)
POOL_WINDOWS = (2, 4, 8, 16)
CHUNK = 128
N_EXPERTS = 16
FF = 1024
CAP_CTX = 2 * L_CTX // N_EXPERTS
CAP_LAT = 2 * L_LAT // N_EXPERTS
SLOTS = N_CTX_SEQ * CAP_CTX + N_LAT_SEQ * CAP_LAT
N_MOD = 6
NORM_EPS = 1e-6
GN_EPS = 64e-5
N_SMALL = 3 * W + 2 * LORA + 2 * LORA + LORA_G + 6 * W
LANE = 128
SUBLANE = 8
VMEM_LIMIT = 56 * 1024 * 1024


def _cparams(n_axes):
    return pltpu.CompilerParams(dimension_semantics=("arbitrary",) * n_axes,
                                vmem_limit_bytes=VMEM_LIMIT)


def _split2(x):
    hi = x.astype(BF16)
    lo = (x - hi.astype(F32)).astype(BF16)
    return hi, lo


def _split3(x):
    hi = x.astype(BF16)
    r = x - hi.astype(F32)
    mid = r.astype(BF16)
    lo = (r - mid.astype(F32)).astype(BF16)
    return hi, mid, lo


def _dot(a, b):
    return jnp.dot(a, b, preferred_element_type=F32)


def _sel_dot_l(m_bf16, x, parts=2):
    ps = _split2(x) if parts == 2 else _split3(x)
    acc = _dot(m_bf16, ps[0])
    for p in ps[1:]:
        acc = acc + _dot(m_bf16, p)
    return acc


def _sel_dot_r(x, m_bf16, parts=2):
    ps = _split2(x) if parts == 2 else _split3(x)
    acc = _dot(ps[0], m_bf16)
    for p in ps[1:]:
        acc = acc + _dot(p, m_bf16)
    return acc


def _dot_hl(a, b):
    ah, al = _split2(a)
    bh, bl = _split2(b)
    return _dot(ah, bh) + _dot(al, bh) + _dot(ah, bl)


def _sigmoid(x):
    return 1.0 / (1.0 + jnp.exp(-x))


def _gelu_tanh(x):
    return 0.5 * x * (1.0 + jnp.tanh(math.sqrt(2.0 / math.pi) * (x + 0.044715 * (x * x * x))))


def _norm_mod(x, g, scale, shift):
    ms = jnp.mean(x * x, axis=-1, keepdims=True)
    return (x * lax.rsqrt(ms + NORM_EPS) * g) * (1.0 + scale) + shift


def _mod_row(i, tm=TM):
    return jnp.where(i < T_CTX // tm, 0, (i - T_CTX // tm) // (L_LAT // tm) + 1)


def _mod_kernel(c_ref, w_ref, b_ref, o_ref):
    c = c_ref[...]
    s = c * _sigmoid(c)
    o_ref[...] = _dot(s.astype(BF16), w_ref[...].astype(BF16)) + b_ref[...]


def _modulation(cond8, w_mod, b_mod):
    tn = 1536
    n = N_MOD * D
    return pl.pallas_call(
        _mod_kernel,
        grid=(DEPTH, n // tn),
        in_specs=[pl.BlockSpec((SUBLANE, D), lambda l, j: (0, 0)),
                  pl.BlockSpec((None, D, tn), lambda l, j: (l, 0, j)),
                  pl.BlockSpec((None, 1, tn), lambda l, j: (l, 0, j))],
        out_specs=pl.BlockSpec((None, SUBLANE, tn), lambda l, j: (l, 0, j)),
        out_shape=jax.ShapeDtypeStruct((DEPTH, SUBLANE, n), F32),
        compiler_params=_cparams(2),
        name="modulation",
    )(cond8, w_mod, b_mod.reshape(DEPTH, 1, n))


def _inproj_kernel(x_ref, mod_ref, g_ref, w_ref, wup_ref, aup_ref, gup_ref, w0_ref, a0_ref, kk_k_ref, k_a_ref,
                   r_k_ref, ones4_ref, rkv_ref, bcd_ref, wdec_ref, b_ref, kin_ref, kk_ref, gate_ref, bonus_ref):
    h = _norm_mod(x_ref[...], g_ref[...], mod_ref[0, 1:2, :], mod_ref[0, 0:1, :])
    z = _dot(h.astype(BF16), w_ref[...])
    rkv_ref[...] = z[:, :3 * W]
    bcd_ref[...] = z[:, 3 * W + 3 * LANE:]
    r = z[:, 0:W]
    k = z[:, W:2 * W]
    v = z[:, 2 * W:3 * W]
    xw = z[:, 3 * W:3 * W + LANE]
    xa = z[:, 3 * W + LANE:3 * W + 2 * LANE]
    xg = z[:, 3 * W + 2 * LANE:3 * W + 3 * LANE]
    ones4 = ones4_ref[...]
    dec = w0_ref[...] + _dot_hl(jnp.tanh(xw), wup_ref[...])
    wdec_ref[...] = jnp.exp(-DECAY_SCALE * _sigmoid(dec))
    a = _sigmoid(a0_ref[...] + _dot_hl(xa, aup_ref[...]))
    gate_ref[...] = _dot_hl(_sigmoid(xg), gup_ref[...])
    kk = k * kk_k_ref[...]
    n2 = _sel_dot_r(kk * kk, ones4, parts=3)
    kk = kk / jnp.maximum(jnp.sqrt(n2), 1e-12)
    kk_ref[...] = kk
    k_a = k_a_ref[...]
    kin0 = k * (1.0 + (a[:, 0:W] - 1.0) * k_a)
    kin1 = k * (1.0 + (a[:, W:2 * W] - 1.0) * k_a)
    kin_ref[:, 0:W] = kin0
    kin_ref[:, W:2 * W] = kin1
    b_ref[:, 0:W] = a[:, 0:W] * kk
    b_ref[:, W:2 * W] = a[:, W:2 * W] * kk
    bonus = _sel_dot_r(r * (kin0 + kin1) * r_k_ref[...], ones4, parts=3)
    bonus_ref[...] = bonus * v


def _in_projection(x, mod_l, g1, w_small, wup_bd, aup_bd, g_up, w0, a0, k_k, k_a, r_k, ones4):
    tok = lambda c: pl.BlockSpec((TM, c), lambda i: (i, 0))
    full = lambda a: pl.BlockSpec(a.shape, lambda i: (0,) * a.ndim)
    params = (w_small, wup_bd, aup_bd, g_up, w0, a0, k_k, k_a, r_k, ones4)
    widths = (3 * W, 6 * W, 2 * W, 2 * W, 2 * W, W, W, W)
    return pl.pallas_call(
        _inproj_kernel,
        grid=(N_BLK,),
        in_specs=[tok(D), pl.BlockSpec((1, N_MOD, D), lambda i: (_mod_row(i), 0, 0)), full(g1)]
                 + [full(a) for a in params],
        out_specs=[tok(c) for c in widths],
        out_shape=[jax.ShapeDtypeStruct((T_ALL, c), F32) for c in widths],
        compiler_params=_cparams(1),
        name="in_projection",
    )(x, mod_l, g1, *params)


T_STEP = SUBLANE
CHAINS_PER_DOT = 4
CTX_SEQ_PER_SCAN_BLK = 4
N_SCAN_BLK = N_CTX_SEQ // CTX_SEQ_PER_SCAN_BLK
LAT_WIN = L_LAT // N_SCAN_BLK
SCAN_FIELDS = ('r', 'v', 'kk', 'w', 'b', 'k')


def _scan_kernel(*refs):
    n_f = len(SCAN_FIELDS)
    ctx_refs = dict(zip(SCAN_FIELDS, refs[:n_f]))
    lat_refs = {}
    pos = n_f
    for s in range(N_LAT_SEQ):
        for d in range(2):
            lat_refs[(s, d)] = dict(zip(SCAN_FIELDS, refs[pos:pos + n_f]))
            pos += n_f
    s0c_ref, s0l_ref, ones4_ref, eye4_ref, yc_ref, sfin_ref, yl_ref, slat_scr = refs[pos:]
    step_i = pl.program_id(0)
    ones4 = ones4_ref[...]
    eye4 = eye4_ref[...]
    chains = ([(False, b, d) for b in range(CTX_SEQ_PER_SCAN_BLK) for d in range(2)]
              + [(True, s, d) for s in range(N_LAT_SEQ) for d in range(2)])
    n_chain = len(chains)
    groups = [chains[g:g + CHAINS_PER_DOT] for g in range(0, n_chain, CHAINS_PER_DOT)]

    @pl.when(step_i == 0)
    def _():
        yl_ref[...] = jnp.zeros_like(yl_ref)
        for s in range(N_LAT_SEQ):
            for d in range(2):
                slat_scr[s, d] = s0l_ref[s, d]

    yc_ref[...] = jnp.zeros_like(yc_ref)

    eye16 = eye4.astype(BF16)
    lane_in_head = jnp.bitwise_and(lax.broadcasted_iota(jnp.int32, (SUBLANE, W), 1), HEAD - 1)
    row_group = lane_in_head // SUBLANE
    on_diag = jnp.bitwise_and(lane_in_head, SUBLANE - 1) == lax.broadcasted_iota(jnp.int32, (SUBLANE, W), 0)

    def diagonal(x):
        m = x[0:SUBLANE]
        for g in range(1, HEAD // SUBLANE):
            m = jnp.where(row_group == g, x[g * SUBLANE:(g + 1) * SUBLANE], m)
        return jnp.sum(jnp.where(on_diag, m, 0.0), axis=0, keepdims=True)

    def row_sum(blocks):
        return _dot(jnp.concatenate(blocks, axis=0), ones4)

    def body(tb, state):
        state = list(state)
        tiles = []
        for lat, b, d in chains:
            t0 = tb * T_STEP if d == 0 else L_CTX - T_STEP - tb * T_STEP
            if lat:
                rows = pl.ds(pl.multiple_of(t0, T_STEP), T_STEP)
                tiles.append({f: lat_refs[(b, d)][f][rows, :] for f in SCAN_FIELDS})
            else:
                rows = pl.ds(pl.multiple_of(b * L_CTX + t0, T_STEP), T_STEP)
                col = lambda f: slice(d * W, (d + 1) * W) if f in ('w', 'b', 'k') else slice(0, W)
                tiles.append({f: ctx_refs[f][rows, col(f)] for f in SCAN_FIELDS})

        def row(name, c, j):
            jj = j if chains[c][2] == 0 else T_STEP - 1 - j
            return tiles[c][name][jj:jj + 1, :]

        def row16(name, c, j):
            return row(name, c, j).astype(BF16)

        state16 = [s.astype(BF16) for s in state]
        y_rows = [[None] * T_STEP for _ in chains]
        def outputs(j, q):
            y_cols = row_sum(q)
            for c, (lat, b, d) in enumerate(chains):
                jj = j if d == 0 else T_STEP - 1 - j
                y_rows[c][jj] = diagonal(y_cols[c * HEAD:(c + 1) * HEAD])

        def feedback(grp_cs, j):
            return row_sum([state16[c] * row16('kk', c, j) for c in grp_cs]
                           + [eye16 * row16('v', c, j) for c in grp_cs])

        group_cs = [[chains.index(ch) for ch in grp] for grp in groups]
        sums = [feedback(cs, 0) for cs in group_cs]
        for j in range(T_STEP):
            q = [None] * n_chain
            for g, cs in enumerate(group_cs):
                n = len(cs)
                for i, c in enumerate(cs):
                    s = (state[c] * row('w', c, j) - sums[g][i * HEAD:(i + 1) * HEAD] * row('b', c, j)
                         + sums[g][(n + i) * HEAD:(n + i + 1) * HEAD] * row('k', c, j))
                    state[c] = s
                    state16[c] = s.astype(BF16)
                    q[c] = state16[c] * row16('r', c, j)
                if j + 1 < T_STEP:
                    sums[g] = feedback(cs, j + 1)
            outputs(j, q)
        for c, (lat, b, d) in enumerate(chains):
            t0 = tb * T_STEP if d == 0 else L_CTX - T_STEP - tb * T_STEP
            y_tile = jnp.concatenate(y_rows[c], axis=0)
            if lat:
                win0 = step_i * LAT_WIN if d == 0 else L_LAT - LAT_WIN - step_i * LAT_WIN
                rows = pl.ds(pl.multiple_of(b * L_LAT + win0 + t0, T_STEP), T_STEP)
                yl_ref[rows, :] += y_tile
            else:
                rows = pl.ds(pl.multiple_of(b * L_CTX + t0, T_STEP), T_STEP)
                yc_ref[rows, :] += y_tile
        return tuple(state)

    init = tuple(slat_scr[b, d] if lat else s0c_ref[b, d] for (lat, b, d) in chains)
    fin = lax.fori_loop(0, L_CTX // T_STEP, body, init)
    for c, (lat, b, d) in enumerate(chains):
        if lat:
            slat_scr[b, d] = fin[c]
        else:
            sfin_ref[b, d] = fin[c]


def _rwkv_scan(rkv, kk, wdec, bvec, kin, s0_ctx, s0_lat, ones4, eye4):
    assert LAT_WIN == L_CTX
    arrays = dict(r=(rkv, 0), v=(rkv, 2), kk=(kk, 0), w=(wdec, None), b=(bvec, None), k=(kin, None))
    rows_c = CTX_SEQ_PER_SCAN_BLK * L_CTX
    operands, in_specs = [], []
    for f in SCAN_FIELDS:
        a, col = arrays[f]
        operands.append(a)
        in_specs.append(pl.BlockSpec((rows_c, W if col is not None else 2 * W),
                                     lambda i, col=col: (i, col if col is not None else 0)))
    win_blk0 = T_CTX // LAT_WIN
    for s in range(N_LAT_SEQ):
        for d in range(2):
            for f in SCAN_FIELDS:
                a, col = arrays[f]
                operands.append(a)
                in_specs.append(pl.BlockSpec(
                    (LAT_WIN, W),
                    lambda i, s=s, d=d, col=col: (win_blk0 + s * N_SCAN_BLK + (i if d == 0 else N_SCAN_BLK - 1 - i),
                                                  col if col is not None else d)))
    st = pl.BlockSpec((None, CTX_SEQ_PER_SCAN_BLK, 2, HEAD, W), lambda i: (i, 0, 0, 0, 0))
    full = lambda a: pl.BlockSpec(a.shape, lambda i: (0,) * a.ndim)
    operands += [s0_ctx, s0_lat, ones4, eye4]
    in_specs += [st, full(s0_lat), full(ones4), full(eye4)]
    return pl.pallas_call(
        _scan_kernel,
        grid=(N_SCAN_BLK,),
        in_specs=in_specs,
        out_specs=[pl.BlockSpec((rows_c, W), lambda i: (i, 0)), st, pl.BlockSpec((T_LAT, W), lambda i: (0, 0))],
        out_shape=[jax.ShapeDtypeStruct((T_CTX, W), F32),
                   jax.ShapeDtypeStruct((N_SCAN_BLK, CTX_SEQ_PER_SCAN_BLK, 2, HEAD, W), F32),
                   jax.ShapeDtypeStruct((T_LAT, W), F32)],
        scratch_shapes=[pltpu.VMEM((N_LAT_SEQ, 2, HEAD, W), F32)],
        compiler_params=_cparams(1),
        name="rwkv_scan",
    )(*operands)


def _branch_kernel(y_ref, g_ref, bonus_ref, bcd_ref, lnx_g_ref, lnx_b_ref, band_ref, invcnt_ref, shift_ref,
                   wpool_ref, pscale_ref, conv_w_ref, conv_b_ref, sgu_g_ref, ws_ref, bs_ref, ones4_ref, br_ref):
    ones4 = ones4_ref[...]
    group = lax.broadcasted_iota(jnp.int32, (TM, W), 1) // HEAD

    def seg_mean(x):
        return _sel_dot_r(x, ones4, parts=2) * (1.0 / HEAD)

    y = y_ref[...]
    mu = seg_mean(y)
    yc = y - mu
    var = seg_mean(yc * yc)
    ya = yc * lax.rsqrt(var + GN_EPS) * lnx_g_ref[...] + lnx_b_ref[...] + bonus_ref[...]
    br_ref[:, 0:W] = (ya * g_ref[...]).astype(BF16)

    u = bcd_ref[:, 0:W]
    u_parts = _split2(u)
    win = jnp.zeros((TM, W), F32)
    for gi in range(len(POOL_WINDOWS)):
        band = band_ref[0, gi]
        s = _dot(band, u_parts[0]) + _dot(band, u_parts[1])
        win = jnp.where(group == gi, s, win)
    pooled = win * invcnt_ref[0] - u
    br_ref[:, W:2 * W] = (_dot_hl(pooled, wpool_ref[...]) * pscale_ref[...]).astype(BF16)

    cin = bcd_ref[:, W:2 * W]
    cb = bcd_ref[:, 2 * W:3 * W]
    cc = bcd_ref[:, 3 * W:4 * W]
    t = cc * cin
    t_prev = _sel_dot_l(shift_ref[0, 0], t, parts=2)
    t_next = _sel_dot_l(shift_ref[0, 1], t, parts=2)
    conv = conv_b_ref[...] + t_prev * conv_w_ref[0:1, :]
    conv = conv + t * conv_w_ref[1:2, :]
    conv = conv + t_next * conv_w_ref[2:3, :]
    br_ref[:, 2 * W:3 * W] = (cb * conv).astype(BF16)

    du = _gelu_tanh(bcd_ref[:, 4 * W:5 * W])
    dv = _gelu_tanh(bcd_ref[:, 5 * W:6 * W])
    mu = seg_mean(dv)
    dc = dv - mu
    var = seg_mean(dc * dc)
    vn = dc * lax.rsqrt(var + NORM_EPS) * sgu_g_ref[...]
    group_c = lax.broadcasted_iota(jnp.int32, (CHUNK, W), 1) // HEAD
    for ch in range(TM // CHUNK):
        rows = slice(ch * CHUNK, (ch + 1) * CHUNK)
        vh, vl = _split2(vn[rows, :])
        s = jnp.zeros((CHUNK, W), F32)
        for gi in range(N_HEADS):
            wh, wl = _split2(ws_ref[gi])
            sg = _dot(wh, vh) + _dot(wl, vh) + _dot(wh, vl)
            s = jnp.where(group_c == gi, sg, s)
        br_ref[rows, 3 * W:4 * W] = (du[rows, :] * (s + bs_ref[...])).astype(BF16)


def _branches(y, g, bonus, bcd, p):
    tok = lambda c: pl.BlockSpec((TM, c), lambda i: (i, 0))
    full = lambda a: pl.BlockSpec(a.shape, lambda i: (0,) * a.ndim)
    lay = lambda a: pl.BlockSpec((1,) + a.shape[1:], lambda i: (jnp.where(i < N_CTX_BLK, 0, 1),) + (0,) * (a.ndim - 1))
    return pl.pallas_call(
        _branch_kernel,
        grid=(N_BLK,),
        in_specs=[tok(W), tok(W), tok(W), tok(6 * W), full(p['lnx_g']), full(p['lnx_b']),
                  lay(p['band']), lay(p['invcnt']), lay(p['shift']),
                  full(p['wpool_bd']), full(p['pool_scale']), full(p['conv_w']), full(p['conv_b']),
                  full(p['sgu_g']), full(p['w_s']), full(p['bs_full']), full(p['ones4'])],
        out_specs=tok(4 * W),
        out_shape=jax.ShapeDtypeStruct((T_ALL, 4 * W), BF16),
        compiler_params=_cparams(1),
        name="branches",
    )(y, g, bonus, bcd, p['lnx_g'], p['lnx_b'], p['band'], p['invcnt'], p['shift'], p['wpool_bd'],
      p['pool_scale'], p['conv_w'], p['conv_b'], p['sgu_g'], p['w_s'], p['bs_full'], p['ones4'])


def _merge_kernel(x_ref, mod_ref, g_ref, br_ref, wgl_ref, wbr_ref, wout_ref, o_ref):
    x = x_ref[...]
    h = _norm_mod(x, g_ref[...], mod_ref[0, 1:2, :], mod_ref[0, 0:1, :]).astype(BF16)
    merged = jnp.zeros(x.shape, F32)
    for i in range(4):
        gl = _dot(h, wgl_ref[:, i * D:(i + 1) * D])
        proj = _dot(br_ref[:, i * W:(i + 1) * W], wbr_ref[i])
        merged = merged + _sigmoid(gl) * proj
    mix = _dot(merged.astype(BF16), wout_ref[...])
    o_ref[...] = x + mod_ref[0, 2:3, :] * mix


def _merge(x, mod_l, g1, br, w_gl, w_branch, w_out):
    full = lambda a: pl.BlockSpec(a.shape, lambda i: (0,) * a.ndim)
    tm = TM_WIDE
    return pl.pallas_call(
        _merge_kernel,
        grid=(T_ALL // tm,),
        in_specs=[pl.BlockSpec((tm, D), lambda i: (i, 0)),
                  pl.BlockSpec((1, N_MOD, D), lambda i: (_mod_row(i, tm), 0, 0)),
                  pl.BlockSpec((1, D), lambda i: (0, 0)),
                  pl.BlockSpec((tm, 4 * W), lambda i: (i, 0)),
                  full(w_gl), full(w_branch), full(w_out)],
        out_specs=pl.BlockSpec((tm, D), lambda i: (i, 0)),
        out_shape=jax.ShapeDtypeStruct((T_ALL, D), F32),
        compiler_params=_cparams(1),
        name="merge",
    )(x, mod_l, g1, br, w_gl, w_branch, w_out)


def _router_kernel(x_ref, mod_ref, g_ref, wr_ref, h_ref, aff_ref):
    h = _norm_mod(x_ref[...], g_ref[...], mod_ref[0, 4:5, :], mod_ref[0, 3:4, :])
    h_ref[...] = h.astype(BF16)
    logits = _dot_hl(h, wr_ref[...])
    lane = lax.broadcasted_iota(jnp.int32, (TM, LANE), 1)
    logits = jnp.where(lane < N_EXPERTS, logits, -1e30)
    m = jnp.max(logits, axis=-1, keepdims=True)
    e = jnp.where(lane < N_EXPERTS, jnp.exp(logits - m), 0.0)
    aff_ref[...] = e / jnp.sum(e, axis=-1, keepdims=True)


def _router(x, mod_l, g2, w_router_pad):
    return pl.pallas_call(
        _router_kernel,
        grid=(N_BLK,),
        in_specs=[pl.BlockSpec((TM, D), lambda i: (i, 0)),
                  pl.BlockSpec((1, N_MOD, D), lambda i: (_mod_row(i), 0, 0)),
                  pl.BlockSpec((1, D), lambda i: (0, 0)),
                  pl.BlockSpec((D, LANE), lambda i: (0, 0))],
        out_specs=[pl.BlockSpec((TM, D), lambda i: (i, 0)), pl.BlockSpec((TM, LANE), lambda i: (i, 0))],
        out_shape=[jax.ShapeDtypeStruct((T_ALL, D), BF16), jax.ShapeDtypeStruct((T_ALL, LANE), F32)],
        compiler_params=_cparams(1),
        name="router",
    )(x, mod_l, g2, w_router_pad)


def _rank_kernel(seq_len, aff_ref, afft_ref, ones_ref, rank_ref):
    nb = seq_len // TM
    earlier = jnp.where(lax.broadcasted_iota(jnp.int32, (TM, TM), 1) < lax.broadcasted_iota(jnp.int32, (TM, TM), 0),
                        1.0, 0.0)
    lane = lax.broadcasted_iota(jnp.int32, (TM, LANE), 1)
    for j in range(nb):
        aff = aff_ref[j * TM:(j + 1) * TM, :]
        rank = jnp.zeros((TM, LANE), F32)
        for e in range(N_EXPERTS):
            mine = aff[:, e:e+1]
            tiles = []
            for c in range(nb):
                other = afft_ref[e:e+1, c * TM:(c + 1) * TM]
                if c < j:
                    tiles.append(jnp.where(other >= mine, 1.0, 0.0))
                elif c > j:
                    tiles.append(jnp.where(other > mine, 1.0, 0.0))
                else:
                    tiles.append(jnp.where(other > mine, 1.0, jnp.where(other == mine, earlier, 0.0)))
            cnt = _dot(jnp.concatenate(tiles, axis=1).astype(BF16), ones_ref[...])
            rank = jnp.where(lane == e, cnt, rank)
        rank_ref[j * TM:(j + 1) * TM, :] = rank


def _ranks(seq_len, n_seq, blk0, aff, aff_t):
    nb = seq_len // TM
    ones = jnp.ones((seq_len, LANE), BF16)
    return pl.pallas_call(
        functools.partial(_rank_kernel, seq_len),
        grid=(n_seq,),
        in_specs=[pl.BlockSpec((seq_len, LANE), lambda s: (blk0 // nb + s, 0)),
                  pl.BlockSpec((None, N_EXPERTS, seq_len), lambda s: (s, 0, 0)),
                  pl.BlockSpec((seq_len, LANE), lambda s: (0, 0))],
        out_specs=pl.BlockSpec((seq_len, LANE), lambda s: (s, 0)),
        out_shape=jax.ShapeDtypeStruct((n_seq * seq_len, LANE), F32),
        compiler_params=_cparams(1),
        name=f"ranks_{seq_len}",
    )(aff, aff_t, ones)


def _expert_kernel(h_ref, rank_ref, w1_ref, w3_ref, w2_ref, o_ref, xs_scr):
    def gather(seq_len, cap, tok0, slot0):
        slot = lax.broadcasted_iota(jnp.int32, (cap, seq_len), 0).astype(F32)
        rk = rank_ref[:, tok0:tok0 + seq_len]
        onehot = jnp.where(rk == slot, 1.0, 0.0).astype(BF16)
        xs_scr[slot0:slot0 + cap, :] = _dot(onehot, h_ref[tok0:tok0 + seq_len, :]).astype(BF16)

    for s in range(N_CTX_SEQ):
        gather(L_CTX, CAP_CTX, s * L_CTX, s * CAP_CTX)
    for s in range(N_LAT_SEQ):
        gather(L_LAT, CAP_LAT, T_CTX + s * L_LAT, N_CTX_SEQ * CAP_CTX + s * CAP_LAT)
    xs = xs_scr[...]
    a = _dot(xs, w1_ref[...].astype(BF16))
    act = (a * _sigmoid(a)) * _dot(xs, w3_ref[...].astype(BF16))
    o_ref[...] = _dot(act.astype(BF16), w2_ref[...].astype(BF16)).astype(BF16)


def _experts(layer, h2, rank_t, w1, w3, w2):
    wspec = lambda: pl.BlockSpec((None, None, D, FF), lambda e: (layer, e, 0, 0))
    return pl.pallas_call(
        _expert_kernel,
        grid=(N_EXPERTS,),
        in_specs=[pl.BlockSpec((T_ALL, D), lambda e: (0, 0), pipeline_mode=pl.Buffered(1)),
                  pl.BlockSpec((None, 1, T_ALL), lambda e: (e, 0, 0)),
                  wspec(), wspec(), pl.BlockSpec((None, None, FF, D), lambda e: (layer, e, 0, 0))],
        out_specs=pl.BlockSpec((None, SLOTS, D), lambda e: (e, 0, 0)),
        out_shape=jax.ShapeDtypeStruct((N_EXPERTS, SLOTS, D), BF16),
        scratch_shapes=[pltpu.VMEM((SLOTS, D), BF16)],
        compiler_params=_cparams(1),
        name="experts",
    )(h2, rank_t, w1, w3, w2)


def _combine_kernel(cap, final, x_ref, mod_ref, aff_ref, rank_ref, oe_ref, expand_ref, fg_ref, o_ref):
    n = N_EXPERTS * cap
    expand = expand_ref[...]
    rank_x = _sel_dot_r(rank_ref[...], expand, parts=2)
    aff_x = _dot(aff_ref[...].astype(BF16), expand)
    slot = jnp.bitwise_and(lax.broadcasted_iota(jnp.int32, (TM, n), 1), cap - 1).astype(F32)
    gate = jnp.where(rank_x == slot, aff_x, 0.0)
    acc = _dot(gate.astype(BF16), oe_ref[...].reshape(n, D))
    x = x_ref[...] + mod_ref[0, 5:6, :] * acc
    if final:
        ms = jnp.mean(x * x, axis=-1, keepdims=True)
        x = x * lax.rsqrt(ms + NORM_EPS) * fg_ref[...]
    o_ref[...] = x


def _combine(seq_len, n_seq, blk0, cap, slot_blk0, final, x, mod_l, aff, rank, oe, final_g):
    nb = seq_len // TM
    assert cap & (cap - 1) == 0
    tok = lambda c: pl.BlockSpec((TM, c), lambda s, j: (blk0 + s * nb + j, 0))
    expand = jnp.asarray(np.arange(LANE)[:, None] == np.arange(N_EXPERTS * cap)[None, :] // cap, BF16)
    return pl.pallas_call(
        functools.partial(_combine_kernel, cap, final),
        grid=(n_seq, nb),
        in_specs=[tok(D),
                  pl.BlockSpec((1, N_MOD, D), lambda s, j: (_mod_row(blk0 + s * nb + j), 0, 0)),
                  tok(LANE),
                  pl.BlockSpec((TM, LANE), lambda s, j: (s * nb + j, 0)),
                  pl.BlockSpec((N_EXPERTS, cap, D), lambda s, j: (0, slot_blk0 + s, 0)),
                  pl.BlockSpec((LANE, N_EXPERTS * cap), lambda s, j: (0, 0)),
                  pl.BlockSpec((1, D), lambda s, j: (0, 0))],
        out_specs=pl.BlockSpec((TM, D), lambda s, j: (s * nb + j, 0)),
        out_shape=jax.ShapeDtypeStruct((n_seq * seq_len, D), F32),
        compiler_params=_cparams(2),
        name=f"combine_{seq_len}",
    )(x, mod_l, aff, rank, oe, expand, final_g)


def _row_structure(row_len):
    t = np.arange(TM)
    same_row = (t[:, None] // row_len) == (t[None, :] // row_len)
    delta = t[None, :] - t[:, None]
    band = np.stack([same_row & (delta >= -(w // 2)) & (delta < w // 2) for w in POOL_WINDOWS]).astype(np.float32)
    cnt = band.sum(-1)
    invcnt = np.repeat((1.0 / cnt).T, HEAD, axis=1).astype(np.float32)
    shift = np.stack([same_row & (delta == -1), same_row & (delta == 1)]).astype(np.float32)
    return band, invcnt, shift


def _constants():
    lane = np.arange(W)
    ones4 = (lane[:, None] // HEAD == lane[None, :] // HEAD).astype(np.float32)
    eye4 = (np.arange(HEAD)[:, None] == lane[None, :] % HEAD).astype(np.float32)
    structs = [_row_structure(L_CTX), _row_structure(GRID_W)]
    return dict(
        ones4=jnp.asarray(ones4, BF16), eye4=jnp.asarray(eye4, F32),
        band=jnp.asarray(np.stack([s[0] for s in structs]), BF16),
        invcnt=jnp.asarray(np.stack([s[1] for s in structs]), F32),
        shift=jnp.asarray(np.stack([s[2] for s in structs]), BF16))


def _block_diag(blocks):
    n = len(blocks)
    rows = []
    for i, b in enumerate(blocks):
        rows.append(jnp.concatenate([b if j == i else jnp.zeros((b.shape[0], blocks[j].shape[1]), b.dtype)
                                     for j in range(n)], axis=1))
    return jnp.concatenate(rows, axis=0)


def _wide_tiles(s):
    return jnp.moveaxis(s, -3, -2).reshape(s.shape[:-3] + (HEAD, W))


def _head_tiles(s):
    return jnp.moveaxis(s.reshape(s.shape[:-2] + (HEAD, N_HEADS, HEAD)), -2, -3)


def kernel(x_prompt, x_sample, state_rwkv, c, c_ctx, norm1_g, w_mod, b_mod, w_in, w0, w_up, a0, a_up, g_up, k_k, k_a, r_k, lnx_g, lnx_b, w_pool, pool_scale, conv_w, conv_b, sgu_g, w_s, b_s, w_branch, w_out, norm2_g, w_router, w_e1, w_e3, w_e2, final_g):
    const = _constants()
    x = jnp.concatenate([x_prompt.reshape(T_CTX, D), x_sample.reshape(T_LAT, D)], axis=0)
    cond8 = jnp.concatenate([c_ctx[None], c, jnp.zeros((SUBLANE - 1 - N_LAT_SEQ, D), F32)], axis=0)
    mod = _modulation(cond8, w_mod, b_mod).reshape(DEPTH, SUBLANE, N_MOD, D)
    final_g2 = final_g.reshape(1, D)
    ctx_states = []
    y_ctx = y_lat = None
    for l in range(DEPTH):
        mod_l = mod[l]
        g1 = norm1_g[l].reshape(1, D)
        wup_bd = _block_diag([w_up[l, 0], w_up[l, 1]])
        aup_bd = _block_diag([a_up[l, 0], a_up[l, 1]])
        rkv, bcd, wdec, bvec, kin, kk, gate_a, bonus = _in_projection(
            x, mod_l, g1, w_in[l, :, :N_SMALL].astype(BF16), wup_bd, aup_bd, g_up[l], w0[l].reshape(1, 2 * W),
            a0[l].reshape(1, 2 * W), k_k[l].reshape(1, W), k_a[l].reshape(1, W), r_k[l].reshape(1, W),
            const['ones4'])

        s0_ctx = jnp.zeros((N_SCAN_BLK, CTX_SEQ_PER_SCAN_BLK, 2, HEAD, W), F32)
        y_c, s_ctx, y_l = _rwkv_scan(rkv, kk, wdec, bvec, kin, s0_ctx, _wide_tiles(state_rwkv[:, l]),
                                     const['ones4'], const['eye4'])
        ctx_states.append(_head_tiles(s_ctx).reshape(N_CTX_SEQ, 2, N_HEADS, HEAD, HEAD))
        y_scan = jnp.concatenate([y_c, y_l], axis=0)

        bs_full = jnp.repeat(b_s[l].T, HEAD, axis=1)
        bp = dict(lnx_g=lnx_g[l].reshape(1, W), lnx_b=lnx_b[l].reshape(1, W), band=const['band'],
                  invcnt=const['invcnt'], shift=const['shift'],
                  wpool_bd=_block_diag([w_pool[l, i] for i in range(len(POOL_WINDOWS))]),
                  pool_scale=pool_scale[l].reshape(1, W), conv_w=conv_w[l], conv_b=conv_b[l].reshape(1, W),
                  sgu_g=sgu_g[l].reshape(1, W), w_s=w_s[l], bs_full=bs_full, ones4=const['ones4'])
        br = _branches(y_scan, gate_a, bonus, bcd, bp)
        x = _merge(x, mod_l, g1, br, w_in[l, :, N_SMALL:].astype(BF16), w_branch[l].astype(BF16),
                   w_out[l].astype(BF16))

        wr_pad = jnp.concatenate([w_router[l], jnp.zeros((D, LANE - N_EXPERTS), F32)], axis=1)
        h2, aff = _router(x, mod_l, norm2_g[l].reshape(1, D), wr_pad)
        aff_t_ctx = jnp.swapaxes(aff[:T_CTX, :N_EXPERTS].reshape(N_CTX_SEQ, L_CTX, N_EXPERTS), 1, 2)
        aff_t_lat = jnp.swapaxes(aff[T_CTX:, :N_EXPERTS].reshape(N_LAT_SEQ, L_LAT, N_EXPERTS), 1, 2)
        rank_ctx = _ranks(L_CTX, N_CTX_SEQ, 0, aff, aff_t_ctx)
        rank_lat = _ranks(L_LAT, N_LAT_SEQ, N_CTX_BLK, aff, aff_t_lat)
        rank_t = jnp.concatenate([rank_ctx[:, :N_EXPERTS], rank_lat[:, :N_EXPERTS]], axis=0).T
        oe = _experts(l, h2, rank_t.reshape(N_EXPERTS, 1, T_ALL), w_e1, w_e3, w_e2)
        final = l == DEPTH - 1
        y_ctx = _combine(L_CTX, N_CTX_SEQ, 0, CAP_CTX, 0, final, x, mod_l, aff, rank_ctx, oe, final_g2)
        y_lat = _combine(L_LAT, N_LAT_SEQ, N_CTX_BLK, CAP_LAT, N_CTX_SEQ * CAP_CTX // CAP_LAT, final, x, mod_l,
                         aff, rank_lat, oe, final_g2)
        if not final:
            x = jnp.concatenate([y_ctx, y_lat], axis=0)
    y_prompt = y_ctx.reshape(N_CTX_SEQ, L_CTX, D)
    y_sample = y_lat.reshape(N_LAT_SEQ, L_LAT, D)
    new_state = jnp.stack(ctx_states, axis=1)
    return (y_prompt, y_sample, new_state)
```

```python
import functools
import math

import numpy as np
import jax
import jax.numpy as jnp
from jax import lax
from jax.experimental import pallas as pl
from jax.experimental.pallas import tpu as pltpu

F32 = jnp.float32
BF16 = jnp.bfloat16

D = 1024
N_CTX_SEQ, L_CTX = 16, 256
N_LAT_SEQ, L_LAT = 2, 1024
T_CTX = N_CTX_SEQ * L_CTX
T_LAT = N_LAT_SEQ * L_LAT
T_ALL = T_CTX + T_LAT
DEPTH = 2
TM = 256
TM_WIDE = 512
N_BLK = T_ALL // TM
N_CTX_BLK = T_CTX // TM
LAT_BLK_PER_SEQ = L_LAT // TM
GRID_W = 64
W = 256
HEAD = 64
N_HEADS = W // HEAD
LORA = 64
LORA_G = 128
DECAY_SCALE = math.exp(-0.5)
POOL_WINDOWS = (2, 4, 8, 16)
CHUNK = 128
N_EXPERTS = 16
FF = 1024
CAP_CTX = 2 * L_CTX // N_EXPERTS
CAP_LAT = 2 * L_LAT // N_EXPERTS
SLOTS = N_CTX_SEQ * CAP_CTX + N_LAT_SEQ * CAP_LAT
N_MOD = 6
NORM_EPS = 1e-6
GN_EPS = 64e-5
N_SMALL = 3 * W + 2 * LORA + 2 * LORA + LORA_G + 6 * W
LANE = 128
SUBLANE = 8
VMEM_LIMIT = 56 * 1024 * 1024


def _cparams(n_axes):
    return pltpu.CompilerParams(dimension_semantics=("arbitrary",) * n_axes,
                                vmem_limit_bytes=VMEM_LIMIT)


def _split2(x):
    hi = x.astype(BF16)
    lo = (x - hi.astype(F32)).astype(BF16)
    return hi, lo


def _split3(x):
    hi = x.astype(BF16)
    r = x - hi.astype(F32)
    mid = r.astype(BF16)
    lo = (r - mid.astype(F32)).astype(BF16)
    return hi, mid, lo


def _dot(a, b):
    return jnp.dot(a, b, preferred_element_type=F32)


def _sel_dot_l(m_bf16, x, parts=2):
    ps = _split2(x) if parts == 2 else _split3(x)
    acc = _dot(m_bf16, ps[0])
    for p in ps[1:]:
        acc = acc + _dot(m_bf16, p)
    return acc


def _sel_dot_r(x, m_bf16, parts=2):
    ps = _split2(x) if parts == 2 else _split3(x)
    acc = _dot(ps[0], m_bf16)
    for p in ps[1:]:
        acc = acc + _dot(p, m_bf16)
    return acc


def _dot_hl(a, b):
    ah, al = _split2(a)
    bh, bl = _split2(b)
    return _dot(ah, bh) + _dot(al, bh) + _dot(ah, bl)


def _sigmoid(x):
    return 1.0 / (1.0 + jnp.exp(-x))


def _gelu_tanh(x):
    return 0.5 * x * (1.0 + jnp.tanh(math.sqrt(2.0 / math.pi) * (x + 0.044715 * (x * x * x))))


def _norm_mod(x, g, scale, shift):
    ms = jnp.mean(x * x, axis=-1, keepdims=True)
    return (x * lax.rsqrt(ms + NORM_EPS) * g) * (1.0 + scale) + shift


def _mod_row(i, tm=TM):
    return jnp.where(i < T_CTX // tm, 0, (i - T_CTX // tm) // (L_LAT // tm) + 1)


def _mod_kernel(c_ref, w_ref, b_ref, o_ref):
    c = c_ref[...]
    s = c * _sigmoid(c)
    o_ref[...] = _dot(s.astype(BF16), w_ref[...].astype(BF16)) + b_ref[...]


def _modulation(cond8, w_mod, b_mod):
    tn = 1536
    n = N_MOD * D
    return pl.pallas_call(
        _mod_kernel,
        grid=(DEPTH, n // tn),
        in_specs=[pl.BlockSpec((SUBLANE, D), lambda l, j: (0, 0)),
                  pl.BlockSpec((None, D, tn), lambda l, j: (l, 0, j)),
                  pl.BlockSpec((None, 1, tn), lambda l, j: (l, 0, j))],
        out_specs=pl.BlockSpec((None, SUBLANE, tn), lambda l, j: (l, 0, j)),
        out_shape=jax.ShapeDtypeStruct((DEPTH, SUBLANE, n), F32),
        compiler_params=_cparams(2),
        name="modulation",
    )(cond8, w_mod, b_mod.reshape(DEPTH, 1, n))


def _inproj_kernel(x_ref, mod_ref, g_ref, w_ref, wup_ref, aup_ref, gup_ref, w0_ref, a0_ref, kk_k_ref, k_a_ref,
                   r_k_ref, ones4_ref, rkv_ref, bcd_ref, wdec_ref, b_ref, kin_ref, kk_ref, gate_ref, bonus_ref):
    h = _norm_mod(x_ref[...], g_ref[...], mod_ref[0, 1:2, :], mod_ref[0, 0:1, :])
    z = _dot(h.astype(BF16), w_ref[...])
    rkv_ref[...] = z[:, :3 * W]
    bcd_ref[...] = z[:, 3 * W + 3 * LANE:]
    r = z[:, 0:W]
    k = z[:, W:2 * W]
    v = z[:, 2 * W:3 * W]
    xw = z[:, 3 * W:3 * W + LANE]
    xa = z[:, 3 * W + LANE:3 * W + 2 * LANE]
    xg = z[:, 3 * W + 2 * LANE:3 * W + 3 * LANE]
    ones4 = ones4_ref[...]
    dec = w0_ref[...] + _dot_hl(jnp.tanh(xw), wup_ref[...])
    wdec_ref[...] = jnp.exp(-DECAY_SCALE * _sigmoid(dec))
    a = _sigmoid(a0_ref[...] + _dot_hl(xa, aup_ref[...]))
    gate_ref[...] = _dot_hl(_sigmoid(xg), gup_ref[...])
    kk = k * kk_k_ref[...]
    n2 = _sel_dot_r(kk * kk, ones4, parts=3)
    kk = kk / jnp.maximum(jnp.sqrt(n2), 1e-12)
    kk_ref[...] = kk
    k_a = k_a_ref[...]
    kin0 = k * (1.0 + (a[:, 0:W] - 1.0) * k_a)
    kin1 = k * (1.0 + (a[:, W:2 * W] - 1.0) * k_a)
    kin_ref[:, 0:W] = kin0
    kin_ref[:, W:2 * W] = kin1
    b_ref[:, 0:W] = a[:, 0:W] * kk
    b_ref[:, W:2 * W] = a[:, W:2 * W] * kk
    bonus = _sel_dot_r(r * (kin0 + kin1) * r_k_ref[...], ones4, parts=3)
    bonus_ref[...] = bonus * v


def _in_projection(x, mod_l, g1, w_small, wup_bd, aup_bd, g_up, w0, a0, k_k, k_a, r_k, ones4):
    tok = lambda c: pl.BlockSpec((TM, c), lambda i: (i, 0))
    full = lambda a: pl.BlockSpec(a.shape, lambda i: (0,) * a.ndim)
    params = (w_small, wup_bd, aup_bd, g_up, w0, a0, k_k, k_a, r_k, ones4)
    widths = (3 * W, 6 * W, 2 * W, 2 * W, 2 * W, W, W, W)
    return pl.pallas_call(
        _inproj_kernel,
        grid=(N_BLK,),
        in_specs=[tok(D), pl.BlockSpec((1, N_MOD, D), lambda i: (_mod_row(i), 0, 0)), full(g1)]
                 + [full(a) for a in params],
        out_specs=[tok(c) for c in widths],
        out_shape=[jax.ShapeDtypeStruct((T_ALL, c), F32) for c in widths],
        compiler_params=_cparams(1),
        name="in_projection",
    )(x, mod_l, g1, *params)


T_STEP = SUBLANE
CHAINS_PER_DOT = 4
CTX_SEQ_PER_SCAN_BLK = 4
N_SCAN_BLK = N_CTX_SEQ // CTX_SEQ_PER_SCAN_BLK
LAT_WIN = L_LAT // N_SCAN_BLK
SCAN_FIELDS = ('r', 'v', 'kk', 'w', 'b', 'k')


def _scan_kernel(*refs):
    n_f = len(SCAN_FIELDS)
    ctx_refs = dict(zip(SCAN_FIELDS, refs[:n_f]))
    lat_refs = {}
    pos = n_f
    for s in range(N_LAT_SEQ):
        for d in range(2):
            lat_refs[(s, d)] = dict(zip(SCAN_FIELDS, refs[pos:pos + n_f]))
            pos += n_f
    s0c_ref, s0l_ref, ones4_ref, eye4_ref, yc_ref, sfin_ref, yl_ref, slat_scr = refs[pos:]
    step_i = pl.program_id(0)
    ones4 = ones4_ref[...]
    eye4 = eye4_ref[...]
    chains = ([(False, b, d) for b in range(CTX_SEQ_PER_SCAN_BLK) for d in range(2)]
              + [(True, s, d) for s in range(N_LAT_SEQ) for d in range(2)])
    n_chain = len(chains)
    groups = [chains[g:g + CHAINS_PER_DOT] for g in range(0, n_chain, CHAINS_PER_DOT)]

    @pl.when(step_i == 0)
    def _():
        yl_ref[...] = jnp.zeros_like(yl_ref)
        for s in range(N_LAT_SEQ):
            for d in range(2):
                slat_scr[s, d] = s0l_ref[s, d]

    yc_ref[...] = jnp.zeros_like(yc_ref)

    eye16 = eye4.astype(BF16)
    lane_in_head = jnp.bitwise_and(lax.broadcasted_iota(jnp.int32, (SUBLANE, W), 1), HEAD - 1)
    row_group = lane_in_head // SUBLANE
    on_diag = jnp.bitwise_and(lane_in_head, SUBLANE - 1) == lax.broadcasted_iota(jnp.int32, (SUBLANE, W), 0)

    def diagonal(x):
        m = x[0:SUBLANE]
        for g in range(1, HEAD // SUBLANE):
            m = jnp.where(row_group == g, x[g * SUBLANE:(g + 1) * SUBLANE], m)
        return jnp.sum(jnp.where(on_diag, m, 0.0), axis=0, keepdims=True)

    def row_sum(blocks):
        return _dot(jnp.concatenate(blocks, axis=0), ones4)

    def body(tb, state):
        state = list(state)
        tiles = []
        for lat, b, d in chains:
            t0 = tb * T_STEP if d == 0 else L_CTX - T_STEP - tb * T_STEP
            if lat:
                rows = pl.ds(pl.multiple_of(t0, T_STEP), T_STEP)
                tiles.append({f: lat_refs[(b, d)][f][rows, :] for f in SCAN_FIELDS})
            else:
                rows = pl.ds(pl.multiple_of(b * L_CTX + t0, T_STEP), T_STEP)
                col = lambda f: slice(d * W, (d + 1) * W) if f in ('w', 'b', 'k') else slice(0, W)
                tiles.append({f: ctx_refs[f][rows, col(f)] for f in SCAN_FIELDS})

        def row(name, c, j):
            jj = j if chains[c][2] == 0 else T_STEP - 1 - j
            return tiles[c][name][jj:jj + 1, :]

        def row16(name, c, j):
            return row(name, c, j).astype(BF16)

        state16 = [s.astype(BF16) for s in state]
        y_rows = [[None] * T_STEP for _ in chains]
        def outputs(j, q):
            y_cols = row_sum(q)
            for c, (lat, b, d) in enumerate(chains):
                jj = j if d == 0 else T_STEP - 1 - j
                y_rows[c][jj] = diagonal(y_cols[c * HEAD:(c + 1) * HEAD])

        def feedback(grp_cs, j):
            return row_sum([state16[c] * row16('kk', c, j) for c in grp_cs]
                           + [eye16 * row16('v', c, j) for c in grp_cs])

        group_cs = [[chains.index(ch) for ch in grp] for grp in groups]
        sums = [feedback(cs, 0) for cs in group_cs]
        for j in range(T_STEP):
            q = [None] * n_chain
            for g, cs in enumerate(group_cs):
                n = len(cs)
                for i, c in enumerate(cs):
                    s = (state[c] * row('w', c, j) - sums[g][i * HEAD:(i + 1) * HEAD] * row('b', c, j)
                         + sums[g][(n + i) * HEAD:(n + i + 1) * HEAD] * row('k', c, j))
                    state[c] = s
                    state16[c] = s.astype(BF16)
                    q[c] = state16[c] * row16('r', c, j)
                if j + 1 < T_STEP:
                    sums[g] = feedback(cs, j + 1)
            outputs(j, q)
        for c, (lat, b, d) in enumerate(chains):
            t0 = tb * T_STEP if d == 0 else L_CTX - T_STEP - tb * T_STEP
            y_tile = jnp.concatenate(y_rows[c], axis=0)
            if lat:
                win0 = step_i * LAT_WIN if d == 0 else L_LAT - LAT_WIN - step_i * LAT_WIN
                rows = pl.ds(pl.multiple_of(b * L_LAT + win0 + t0, T_STEP), T_STEP)
                yl_ref[rows, :] += y_tile
            else:
                rows = pl.ds(pl.multiple_of(b * L_CTX + t0, T_STEP), T_STEP)
                yc_ref[rows, :] += y_tile
        return tuple(state)

    init = tuple(slat_scr[b, d] if lat else s0c_ref[b, d] for (lat, b, d) in chains)
    fin = lax.fori_loop(0, L_CTX // T_STEP, body, init)
    for c, (lat, b, d) in enumerate(chains):
        if lat:
            slat_scr[b, d] = fin[c]
        else:
            sfin_ref[b, d] = fin[c]


def _rwkv_scan(rkv, kk, wdec, bvec, kin, s0_ctx, s0_lat, ones4, eye4):
    assert LAT_WIN == L_CTX
    arrays = dict(r=(rkv, 0), v=(rkv, 2), kk=(kk, 0), w=(wdec, None), b=(bvec, None), k=(kin, None))
    rows_c = CTX_SEQ_PER_SCAN_BLK * L_CTX
    operands, in_specs = [], []
    for f in SCAN_FIELDS:
        a, col = arrays[f]
        operands.append(a)
        in_specs.append(pl.BlockSpec((rows_c, W if col is not None else 2 * W),
                                     lambda i, col=col: (i, col if col is not None else 0)))
    win_blk0 = T_CTX // LAT_WIN
    for s in range(N_LAT_SEQ):
        for d in range(2):
            for f in SCAN_FIELDS:
                a, col = arrays[f]
                operands.append(a)
                in_specs.append(pl.BlockSpec(
                    (LAT_WIN, W),
                    lambda i, s=s, d=d, col=col: (win_blk0 + s * N_SCAN_BLK + (i if d == 0 else N_SCAN_BLK - 1 - i),
                                                  col if col is not None else d)))
    st = pl.BlockSpec((None, CTX_SEQ_PER_SCAN_BLK, 2, HEAD, W), lambda i: (i, 0, 0, 0, 0))
    full = lambda a: pl.BlockSpec(a.shape, lambda i: (0,) * a.ndim)
    operands += [s0_ctx, s0_lat, ones4, eye4]
    in_specs += [st, full(s0_lat), full(ones4), full(eye4)]
    return pl.pallas_call(
        _scan_kernel,
        grid=(N_SCAN_BLK,),
        in_specs=in_specs,
        out_specs=[pl.BlockSpec((rows_c, W), lambda i: (i, 0)), st, pl.BlockSpec((T_LAT, W), lambda i: (0, 0))],
        out_shape=[jax.ShapeDtypeStruct((T_CTX, W), F32),
                   jax.ShapeDtypeStruct((N_SCAN_BLK, CTX_SEQ_PER_SCAN_BLK, 2, HEAD, W), F32),
                   jax.ShapeDtypeStruct((T_LAT, W), F32)],
        scratch_shapes=[pltpu.VMEM((N_LAT_SEQ, 2, HEAD, W), F32)],
        compiler_params=_cparams(1),
        name="rwkv_scan",
    )(*operands)


def _branch_kernel(yc_ref, yl_ref, g_ref, bonus_ref, bcd_ref, lnx_g_ref, lnx_b_ref, band_ref, invcnt_ref, shift_ref,
                   wpool_ref, pscale_ref, conv_w_ref, conv_b_ref, sgu_g_ref, ws_ref, bs_ref, ones4_ref, br_ref):
    ones4 = ones4_ref[...]
    group = lax.broadcasted_iota(jnp.int32, (TM, W), 1) // HEAD

    def seg_mean(x):
        return _sel_dot_r(x, ones4, parts=2) * (1.0 / HEAD)

    y = jnp.where(pl.program_id(0) < N_CTX_BLK, yc_ref[...], yl_ref[...])
    mu = seg_mean(y)
    yc = y - mu
    var = seg_mean(yc * yc)
    ya = yc * lax.rsqrt(var + GN_EPS) * lnx_g_ref[...] + lnx_b_ref[...] + bonus_ref[...]
    br_ref[:, 0:W] = (ya * g_ref[...]).astype(BF16)

    u = bcd_ref[:, 0:W]
    u_parts = _split2(u)
    win = jnp.zeros((TM, W), F32)
    for gi in range(len(POOL_WINDOWS)):
        band = band_ref[0, gi]
        s = _dot(band, u_parts[0]) + _dot(band, u_parts[1])
        win = jnp.where(group == gi, s, win)
    pooled = win * invcnt_ref[0] - u
    br_ref[:, W:2 * W] = (_dot_hl(pooled, wpool_ref[...]) * pscale_ref[...]).astype(BF16)

    cin = bcd_ref[:, W:2 * W]
    cb = bcd_ref[:, 2 * W:3 * W]
    cc = bcd_ref[:, 3 * W:4 * W]
    t = cc * cin
    t_prev = _sel_dot_l(shift_ref[0, 0], t, parts=2)
    t_next = _sel_dot_l(shift_ref[0, 1], t, parts=2)
    conv = conv_b_ref[...] + t_prev * conv_w_ref[0:1, :]
    conv = conv + t * conv_w_ref[1:2, :]
    conv = conv + t_next * conv_w_ref[2:3, :]
    br_ref[:, 2 * W:3 * W] = (cb * conv).astype(BF16)

    du = _gelu_tanh(bcd_ref[:, 4 * W:5 * W])
    dv = _gelu_tanh(bcd_ref[:, 5 * W:6 * W])
    mu = seg_mean(dv)
    dc = dv - mu
    var = seg_mean(dc * dc)
    vn = dc * lax.rsqrt(var + NORM_EPS) * sgu_g_ref[...]
    group_c = lax.broadcasted_iota(jnp.int32, (CHUNK, W), 1) // HEAD
    for ch in range(TM // CHUNK):
        rows = slice(ch * CHUNK, (ch + 1) * CHUNK)
        vh, vl = _split2(vn[rows, :])
        s = jnp.zeros((CHUNK, W), F32)
        for gi in range(N_HEADS):
            wh, wl = _split2(ws_ref[gi])
            sg = _dot(wh, vh) + _dot(wl, vh) + _dot(wh, vl)
            s = jnp.where(group_c == gi, sg, s)
        br_ref[rows, 3 * W:4 * W] = (du[rows, :] * (s + bs_ref[...])).astype(BF16)


def _branches(y_ctx, y_lat, g, bonus, bcd, p):
    tok = lambda c: pl.BlockSpec((TM, c), lambda i: (i, 0))
    y_specs = [pl.BlockSpec((TM, W), lambda i: (jnp.minimum(i, N_CTX_BLK - 1), 0)),
               pl.BlockSpec((TM, W), lambda i: (jnp.maximum(i - N_CTX_BLK, 0), 0))]
    full = lambda a: pl.BlockSpec(a.shape, lambda i: (0,) * a.ndim)
    lay = lambda a: pl.BlockSpec((1,) + a.shape[1:], lambda i: (jnp.where(i < N_CTX_BLK, 0, 1),) + (0,) * (a.ndim - 1))
    return pl.pallas_call(
        _branch_kernel,
        grid=(N_BLK,),
        in_specs=y_specs + [tok(W), tok(W), tok(6 * W), full(p['lnx_g']), full(p['lnx_b']),
                  lay(p['band']), lay(p['invcnt']), lay(p['shift']),
                  full(p['wpool_bd']), full(p['pool_scale']), full(p['conv_w']), full(p['conv_b']),
                  full(p['sgu_g']), full(p['w_s']), full(p['bs_full']), full(p['ones4'])],
        out_specs=tok(4 * W),
        out_shape=jax.ShapeDtypeStruct((T_ALL, 4 * W), BF16),
        compiler_params=_cparams(1),
        name="branches",
    )(y_ctx, y_lat, g, bonus, bcd, p['lnx_g'], p['lnx_b'], p['band'], p['invcnt'], p['shift'], p['wpool_bd'],
      p['pool_scale'], p['conv_w'], p['conv_b'], p['sgu_g'], p['w_s'], p['bs_full'], p['ones4'])


def _merge_kernel(x_ref, mod_ref, g1_ref, br_ref, wgl_ref, wbr_ref, wout_ref, g2_ref, wr_ref, o_ref, h_ref, aff_ref):
    x = x_ref[...]
    h = _norm_mod(x, g1_ref[...], mod_ref[0, 1:2, :], mod_ref[0, 0:1, :]).astype(BF16)
    merged = jnp.zeros(x.shape, F32)
    for i in range(4):
        gl = _dot(h, wgl_ref[:, i * D:(i + 1) * D])
        proj = _dot(br_ref[:, i * W:(i + 1) * W], wbr_ref[i])
        merged = merged + _sigmoid(gl) * proj
    mix = _dot(merged.astype(BF16), wout_ref[...])
    x = x + mod_ref[0, 2:3, :] * mix
    o_ref[...] = x

    h2 = _norm_mod(x, g2_ref[...], mod_ref[0, 4:5, :], mod_ref[0, 3:4, :])
    h_ref[...] = h2.astype(BF16)
    logits = _dot_hl(h2, wr_ref[...])
    lane = lax.broadcasted_iota(jnp.int32, logits.shape, 1)
    logits = jnp.where(lane < N_EXPERTS, logits, -1e30)
    m = jnp.max(logits, axis=-1, keepdims=True)
    e = jnp.where(lane < N_EXPERTS, jnp.exp(logits - m), 0.0)
    aff_ref[...] = e / jnp.sum(e, axis=-1, keepdims=True)


def _merge(x, mod_l, g1, br, w_gl, w_branch, w_out, g2, w_router_pad):
    full = lambda a: pl.BlockSpec(a.shape, lambda i: (0,) * a.ndim)
    tm = TM_WIDE
    tok = lambda c: pl.BlockSpec((tm, c), lambda i: (i, 0))
    return pl.pallas_call(
        _merge_kernel,
        grid=(T_ALL // tm,),
        in_specs=[tok(D), pl.BlockSpec((1, N_MOD, D), lambda i: (_mod_row(i, tm), 0, 0)), full(g1), tok(4 * W),
                  full(w_gl), full(w_branch), full(w_out), full(g2), full(w_router_pad)],
        out_specs=[tok(D), tok(D), tok(LANE)],
        out_shape=[jax.ShapeDtypeStruct((T_ALL, D), F32), jax.ShapeDtypeStruct((T_ALL, D), BF16),
                   jax.ShapeDtypeStruct((T_ALL, LANE), F32)],
        compiler_params=_cparams(1),
        name="merge",
    )(x, mod_l, g1, br, w_gl, w_branch, w_out, g2, w_router_pad)


def _rank_kernel(seq_len, aff_ref, afft_ref, ones_ref, rank_ref):
    nb = seq_len // TM
    earlier = jnp.where(lax.broadcasted_iota(jnp.int32, (TM, TM), 1) < lax.broadcasted_iota(jnp.int32, (TM, TM), 0),
                        1.0, 0.0)
    lane = lax.broadcasted_iota(jnp.int32, (TM, LANE), 1)
    for j in range(nb):
        aff = aff_ref[j * TM:(j + 1) * TM, :]
        rank = jnp.zeros((TM, LANE), F32)
        for e in range(N_EXPERTS):
            mine = aff[:, e:e+1]
            tiles = []
            for c in range(nb):
                other = afft_ref[e:e+1, c * TM:(c + 1) * TM]
                if c < j:
                    tiles.append(jnp.where(other >= mine, 1.0, 0.0))
                elif c > j:
                    tiles.append(jnp.where(other > mine, 1.0, 0.0))
                else:
                    tiles.append(jnp.where(other > mine, 1.0, jnp.where(other == mine, earlier, 0.0)))
            cnt = _dot(jnp.concatenate(tiles, axis=1).astype(BF16), ones_ref[...])
            rank = jnp.where(lane == e, cnt, rank)
        rank_ref[j * TM:(j + 1) * TM, :] = rank


def _ranks(seq_len, n_seq, blk0, aff, aff_t):
    nb = seq_len // TM
    ones = jnp.ones((seq_len, LANE), BF16)
    return pl.pallas_call(
        functools.partial(_rank_kernel, seq_len),
        grid=(n_seq,),
        in_specs=[pl.BlockSpec((seq_len, LANE), lambda s: (blk0 // nb + s, 0)),
                  pl.BlockSpec((None, N_EXPERTS, seq_len), lambda s: (s, 0, 0)),
                  pl.BlockSpec((seq_len, LANE), lambda s: (0, 0))],
        out_specs=pl.BlockSpec((seq_len, LANE), lambda s: (s, 0)),
        out_shape=jax.ShapeDtypeStruct((n_seq * seq_len, LANE), F32),
        compiler_params=_cparams(1),
        name=f"ranks_{seq_len}",
    )(aff, aff_t, ones)


def _expert_kernel(h_ref, rank_ref, w1_ref, w3_ref, w2_ref, o_ref, xs_scr):
    def gather(seq_len, cap, tok0, slot0):
        slot = lax.broadcasted_iota(jnp.int32, (cap, seq_len), 0).astype(F32)
        rk = rank_ref[:, tok0:tok0 + seq_len]
        onehot = jnp.where(rk == slot, 1.0, 0.0).astype(BF16)
        xs_scr[slot0:slot0 + cap, :] = _dot(onehot, h_ref[tok0:tok0 + seq_len, :]).astype(BF16)

    for s in range(N_CTX_SEQ):
        gather(L_CTX, CAP_CTX, s * L_CTX, s * CAP_CTX)
    for s in range(N_LAT_SEQ):
        gather(L_LAT, CAP_LAT, T_CTX + s * L_LAT, N_CTX_SEQ * CAP_CTX + s * CAP_LAT)
    xs = xs_scr[...]
    a = _dot(xs, w1_ref[...].astype(BF16))
    act = (a * _sigmoid(a)) * _dot(xs, w3_ref[...].astype(BF16))
    o_ref[...] = _dot(act.astype(BF16), w2_ref[...].astype(BF16)).astype(BF16)


def _experts(layer, h2, rank_t, w1, w3, w2):
    wspec = lambda: pl.BlockSpec((None, None, D, FF), lambda e: (layer, e, 0, 0))
    return pl.pallas_call(
        _expert_kernel,
        grid=(N_EXPERTS,),
        in_specs=[pl.BlockSpec((T_ALL, D), lambda e: (0, 0), pipeline_mode=pl.Buffered(1)),
                  pl.BlockSpec((None, 1, T_ALL), lambda e: (e, 0, 0)),
                  wspec(), wspec(), pl.BlockSpec((None, None, FF, D), lambda e: (layer, e, 0, 0))],
        out_specs=pl.BlockSpec((None, SLOTS, D), lambda e: (e, 0, 0)),
        out_shape=jax.ShapeDtypeStruct((N_EXPERTS, SLOTS, D), BF16),
        scratch_shapes=[pltpu.VMEM((SLOTS, D), BF16)],
        compiler_params=_cparams(1),
        name="experts",
    )(h2, rank_t, w1, w3, w2)


def _combine_kernel(cap, final, x_ref, mod_ref, aff_ref, rank_ref, oe_ref, expand_ref, fg_ref, o_ref):
    n = N_EXPERTS * cap
    expand = expand_ref[...]
    rank_x = _sel_dot_r(rank_ref[...], expand, parts=2)
    aff_x = _dot(aff_ref[...].astype(BF16), expand)
    slot = jnp.bitwise_and(lax.broadcasted_iota(jnp.int32, (TM, n), 1), cap - 1).astype(F32)
    gate = jnp.where(rank_x == slot, aff_x, 0.0)
    acc = _dot(gate.astype(BF16), oe_ref[...].reshape(n, D))
    x = x_ref[...] + mod_ref[0, 5:6, :] * acc
    if final:
        ms = jnp.mean(x * x, axis=-1, keepdims=True)
        x = x * lax.rsqrt(ms + NORM_EPS) * fg_ref[...]
    o_ref[...] = x


def _combine(seq_len, n_seq, blk0, cap, slot_blk0, final, x, mod_l, aff, rank, oe, final_g):
    nb = seq_len // TM
    assert cap & (cap - 1) == 0
    tok = lambda c: pl.BlockSpec((TM, c), lambda s, j: (blk0 + s * nb + j, 0))
    expand = jnp.asarray(np.arange(LANE)[:, None] == np.arange(N_EXPERTS * cap)[None, :] // cap, BF16)
    if final:
        out_spec = pl.BlockSpec((TM, D), lambda s, j: (s * nb + j, 0))
        out_shape, aliases = jax.ShapeDtypeStruct((n_seq * seq_len, D), F32), {}
    else:
        out_spec, out_shape, aliases = tok(D), jax.ShapeDtypeStruct((T_ALL, D), F32), {0: 0}
    return pl.pallas_call(
        functools.partial(_combine_kernel, cap, final),
        grid=(n_seq, nb),
        in_specs=[tok(D),
                  pl.BlockSpec((1, N_MOD, D), lambda s, j: (_mod_row(blk0 + s * nb + j), 0, 0)),
                  tok(LANE),
                  pl.BlockSpec((TM, LANE), lambda s, j: (s * nb + j, 0)),
                  pl.BlockSpec((N_EXPERTS, cap, D), lambda s, j: (0, slot_blk0 + s, 0)),
                  pl.BlockSpec((LANE, N_EXPERTS * cap), lambda s, j: (0, 0)),
                  pl.BlockSpec((1, D), lambda s, j: (0, 0))],
        out_specs=out_spec,
        out_shape=out_shape,
        input_output_aliases=aliases,
        compiler_params=_cparams(2),
        name=f"combine_{seq_len}",
    )(x, mod_l, aff, rank, oe, expand, final_g)


def _row_structure(row_len):
    t = np.arange(TM)
    same_row = (t[:, None] // row_len) == (t[None, :] // row_len)
    delta = t[None, :] - t[:, None]
    band = np.stack([same_row & (delta >= -(w // 2)) & (delta < w // 2) for w in POOL_WINDOWS]).astype(np.float32)
    cnt = band.sum(-1)
    invcnt = np.repeat((1.0 / cnt).T, HEAD, axis=1).astype(np.float32)
    shift = np.stack([same_row & (delta == -1), same_row & (delta == 1)]).astype(np.float32)
    return band, invcnt, shift


def _constants():
    lane = np.arange(W)
    ones4 = (lane[:, None] // HEAD == lane[None, :] // HEAD).astype(np.float32)
    eye4 = (np.arange(HEAD)[:, None] == lane[None, :] % HEAD).astype(np.float32)
    structs = [_row_structure(L_CTX), _row_structure(GRID_W)]
    return dict(
        ones4=jnp.asarray(ones4, BF16), eye4=jnp.asarray(eye4, F32),
        band=jnp.asarray(np.stack([s[0] for s in structs]), BF16),
        invcnt=jnp.asarray(np.stack([s[1] for s in structs]), F32),
        shift=jnp.asarray(np.stack([s[2] for s in structs]), BF16))


def _block_diag(blocks):
    n = len(blocks)
    rows = []
    for i, b in enumerate(blocks):
        rows.append(jnp.concatenate([b if j == i else jnp.zeros((b.shape[0], blocks[j].shape[1]), b.dtype)
                                     for j in range(n)], axis=1))
    return jnp.concatenate(rows, axis=0)


def _wide_tiles(s):
    return jnp.moveaxis(s, -3, -2).reshape(s.shape[:-3] + (HEAD, W))


def _head_tiles(s):
    return jnp.moveaxis(s.reshape(s.shape[:-2] + (HEAD, N_HEADS, HEAD)), -2, -3)


def kernel(x_prompt, x_sample, state_rwkv, c, c_ctx, norm1_g, w_mod, b_mod, w_in, w0, w_up, a0, a_up, g_up, k_k, k_a, r_k, lnx_g, lnx_b, w_pool, pool_scale, conv_w, conv_b, sgu_g, w_s, b_s, w_branch, w_out, norm2_g, w_router, w_e1, w_e3, w_e2, final_g):
    const = _constants()
    x = jnp.concatenate([x_prompt.reshape(T_CTX, D), x_sample.reshape(T_LAT, D)], axis=0)
    cond8 = jnp.concatenate([c_ctx[None], c, jnp.zeros((SUBLANE - 1 - N_LAT_SEQ, D), F32)], axis=0)
    mod = _modulation(cond8, w_mod, b_mod).reshape(DEPTH, SUBLANE, N_MOD, D)
    final_g2 = final_g.reshape(1, D)
    ctx_states = []
    y_ctx = y_lat = None
    for l in range(DEPTH):
        mod_l = mod[l]
        g1 = norm1_g[l].reshape(1, D)
        wup_bd = _block_diag([w_up[l, 0], w_up[l, 1]])
        aup_bd = _block_diag([a_up[l, 0], a_up[l, 1]])
        rkv, bcd, wdec, bvec, kin, kk, gate_a, bonus = _in_projection(
            x, mod_l, g1, w_in[l, :, :N_SMALL].astype(BF16), wup_bd, aup_bd, g_up[l], w0[l].reshape(1, 2 * W),
            a0[l].reshape(1, 2 * W), k_k[l].reshape(1, W), k_a[l].reshape(1, W), r_k[l].reshape(1, W),
            const['ones4'])

        s0_ctx = jnp.zeros((N_SCAN_BLK, CTX_SEQ_PER_SCAN_BLK, 2, HEAD, W), F32)
        y_c, s_ctx, y_l = _rwkv_scan(rkv, kk, wdec, bvec, kin, s0_ctx, _wide_tiles(state_rwkv[:, l]),
                                     const['ones4'], const['eye4'])
        ctx_states.append(_head_tiles(s_ctx).reshape(N_CTX_SEQ, 2, N_HEADS, HEAD, HEAD))

        bs_full = jnp.repeat(b_s[l].T, HEAD, axis=1)
        bp = dict(lnx_g=lnx_g[l].reshape(1, W), lnx_b=lnx_b[l].reshape(1, W), band=const['band'],
                  invcnt=const['invcnt'], shift=const['shift'],
                  wpool_bd=_block_diag([w_pool[l, i] for i in range(len(POOL_WINDOWS))]),
                  pool_scale=pool_scale[l].reshape(1, W), conv_w=conv_w[l], conv_b=conv_b[l].reshape(1, W),
                  sgu_g=sgu_g[l].reshape(1, W), w_s=w_s[l], bs_full=bs_full, ones4=const['ones4'])
        br = _branches(y_c, y_l, gate_a, bonus, bcd, bp)
        wr_pad = jnp.concatenate([w_router[l], jnp.zeros((D, LANE - N_EXPERTS), F32)], axis=1)
        x, h2, aff = _merge(x, mod_l, g1, br, w_in[l, :, N_SMALL:].astype(BF16), w_branch[l].astype(BF16),
                            w_out[l].astype(BF16), norm2_g[l].reshape(1, D), wr_pad)
        aff_t_ctx = jnp.swapaxes(aff[:T_CTX, :N_EXPERTS].reshape(N_CTX_SEQ, L_CTX, N_EXPERTS), 1, 2)
        aff_t_lat = jnp.swapaxes(aff[T_CTX:, :N_EXPERTS].reshape(N_LAT_SEQ, L_LAT, N_EXPERTS), 1, 2)
        rank_ctx = _ranks(L_CTX, N_CTX_SEQ, 0, aff, aff_t_ctx)
        rank_lat = _ranks(L_LAT, N_LAT_SEQ, N_CTX_BLK, aff, aff_t_lat)
        rank_t = jnp.concatenate([rank_ctx[:, :N_EXPERTS], rank_lat[:, :N_EXPERTS]], axis=0).T
        oe = _experts(l, h2, rank_t.reshape(N_EXPERTS, 1, T_ALL), w_e1, w_e3, w_e2)
        final = l == DEPTH - 1
        y_ctx = _combine(L_CTX, N_CTX_SEQ, 0, CAP_CTX, 0, final, x, mod_l, aff, rank_ctx, oe, final_g2)
        y_lat = _combine(L_LAT, N_LAT_SEQ, N_CTX_BLK, CAP_LAT, N_CTX_SEQ * CAP_CTX // CAP_LAT, final,
                         x if final else y_ctx, mod_l, aff, rank_lat, oe, final_g2)
        if not final:
            x = y_lat
    y_prompt = y_ctx.reshape(N_CTX_SEQ, L_CTX, D)
    y_sample = y_lat.reshape(N_LAT_SEQ, L_LAT, D)
    new_state = jnp.stack(ctx_states, axis=1)
    return (y_prompt, y_sample, new_state)
```

```python
import functools
import math

import numpy as np
import jax
import jax.numpy as jnp
from jax import lax
from jax.experimental import pallas as pl
from jax.experimental.pallas import tpu as pltpu

F32 = jnp.float32
BF16 = jnp.bfloat16

D = 1024
N_CTX_SEQ, L_CTX = 16, 256
N_LAT_SEQ, L_LAT = 2, 1024
T_CTX = N_CTX_SEQ * L_CTX
T_LAT = N_LAT_SEQ * L_LAT
T_ALL = T_CTX + T_LAT
DEPTH = 2
TM = 256
TM_WIDE = 512
N_BLK = T_ALL // TM
N_CTX_BLK = T_CTX // TM
LAT_BLK_PER_SEQ = L_LAT // TM
GRID_W = 64
W = 256
HEAD = 64
N_HEADS = W // HEAD
LORA = 64
LORA_G = 128
DECAY_SCALE = math.exp(-0.5)
POOL_WINDOWS = (2, 4, 8, 16)
CHUNK = 128
N_EXPERTS = 16
FF = 1024
CAP_CTX = 2 * L_CTX // N_EXPERTS
CAP_LAT = 2 * L_LAT // N_EXPERTS
SLOTS = N_CTX_SEQ * CAP_CTX + N_LAT_SEQ * CAP_LAT
N_MOD = 6
NORM_EPS = 1e-6
GN_EPS = 64e-5
N_SMALL = 3 * W + 2 * LORA + 2 * LORA + LORA_G + 6 * W
LANE = 128
SUBLANE = 8
VMEM_LIMIT = 56 * 1024 * 1024


def _cparams(n_axes):
    return pltpu.CompilerParams(dimension_semantics=("arbitrary",) * n_axes,
                                vmem_limit_bytes=VMEM_LIMIT)


def _split2(x):
    hi = x.astype(BF16)
    lo = (x - hi.astype(F32)).astype(BF16)
    return hi, lo


def _split3(x):
    hi = x.astype(BF16)
    r = x - hi.astype(F32)
    mid = r.astype(BF16)
    lo = (r - mid.astype(F32)).astype(BF16)
    return hi, mid, lo


def _dot(a, b):
    return jnp.dot(a, b, preferred_element_type=F32)


def _sel_dot_l(m_bf16, x, parts=2):
    ps = _split2(x) if parts == 2 else _split3(x)
    acc = _dot(m_bf16, ps[0])
    for p in ps[1:]:
        acc = acc + _dot(m_bf16, p)
    return acc


def _sel_dot_r(x, m_bf16, parts=2):
    ps = _split2(x) if parts == 2 else _split3(x)
    acc = _dot(ps[0], m_bf16)
    for p in ps[1:]:
        acc = acc + _dot(p, m_bf16)
    return acc


def _dot_hl(a, b):
    ah, al = _split2(a)
    bh, bl = _split2(b)
    return _dot(ah, bh) + _dot(al, bh) + _dot(ah, bl)


def _sigmoid(x):
    return 1.0 / (1.0 + jnp.exp(-x))


def _gelu_tanh(x):
    return 0.5 * x * (1.0 + jnp.tanh(math.sqrt(2.0 / math.pi) * (x + 0.044715 * (x * x * x))))


def _norm_mod(x, g, scale, shift):
    ms = jnp.mean(x * x, axis=-1, keepdims=True)
    return (x * lax.rsqrt(ms + NORM_EPS) * g) * (1.0 + scale) + shift


def _mod_row(i, tm=TM):
    return jnp.where(i < T_CTX // tm, 0, (i - T_CTX // tm) // (L_LAT // tm) + 1)


def _mod_kernel(c_ref, w_ref, b_ref, o_ref):
    c = c_ref[...]
    s = c * _sigmoid(c)
    o_ref[...] = _dot(s.astype(BF16), w_ref[...].astype(BF16)) + b_ref[...]


def _modulation(cond8, w_mod, b_mod):
    tn = 1536
    n = N_MOD * D
    return pl.pallas_call(
        _mod_kernel,
        grid=(DEPTH, n // tn),
        in_specs=[pl.BlockSpec((SUBLANE, D), lambda l, j: (0, 0)),
                  pl.BlockSpec((None, D, tn), lambda l, j: (l, 0, j)),
                  pl.BlockSpec((None, 1, tn), lambda l, j: (l, 0, j))],
        out_specs=pl.BlockSpec((None, SUBLANE, tn), lambda l, j: (l, 0, j)),
        out_shape=jax.ShapeDtypeStruct((DEPTH, SUBLANE, n), F32),
        compiler_params=_cparams(2),
        name="modulation",
    )(cond8, w_mod, b_mod.reshape(DEPTH, 1, n))


def _x_specs(x, tm):
    if not isinstance(x, tuple):
        return [pl.BlockSpec((tm, D), lambda i: (i, 0))], (x,)
    n = T_CTX // tm
    return [pl.BlockSpec((tm, D), lambda i: (jnp.minimum(i, n - 1), 0)),
            pl.BlockSpec((tm, D), lambda i: (jnp.maximum(i - n, 0), 0))], x


def _load_x(x_refs):
    if len(x_refs) == 1:
        return x_refs[0][...]
    n = T_CTX // x_refs[0].shape[0]
    return jnp.where(pl.program_id(0) < n, x_refs[0][...], x_refs[1][...])


def _inproj_kernel(n_x, *refs):
    (mod_ref, g_ref, w_ref, wup_ref, aup_ref, gup_ref, w0_ref, a0_ref, kk_k_ref, k_a_ref, r_k_ref, ones4_ref,
     rkv_ref, bcd_ref, wdec_ref, b_ref, kin_ref, kk_ref, gate_ref, bonus_ref, w16_scr) = refs[n_x:]

    @pl.when(pl.program_id(0) == 0)
    def _():
        w16_scr[...] = w_ref[...].astype(BF16)

    h = _norm_mod(_load_x(refs[:n_x]), g_ref[...], mod_ref[0, 1:2, :], mod_ref[0, 0:1, :])
    z = _dot(h.astype(BF16), w16_scr[...])
    rkv_ref[...] = z[:, :3 * W]
    bcd_ref[...] = z[:, 3 * W + 3 * LANE:]
    r = z[:, 0:W]
    k = z[:, W:2 * W]
    v = z[:, 2 * W:3 * W]
    xw = z[:, 3 * W:3 * W + LANE]
    xa = z[:, 3 * W + LANE:3 * W + 2 * LANE]
    xg = z[:, 3 * W + 2 * LANE:3 * W + 3 * LANE]
    ones4 = ones4_ref[...]
    dec = w0_ref[...] + _dot_hl(jnp.tanh(xw), wup_ref[...])
    wdec_ref[...] = jnp.exp(-DECAY_SCALE * _sigmoid(dec))
    a = _sigmoid(a0_ref[...] + _dot_hl(xa, aup_ref[...]))
    gate_ref[...] = _dot_hl(_sigmoid(xg), gup_ref[...])
    kk = k * kk_k_ref[...]
    n2 = _sel_dot_r(kk * kk, ones4, parts=3)
    kk = kk / jnp.maximum(jnp.sqrt(n2), 1e-12)
    kk_ref[...] = kk
    k_a = k_a_ref[...]
    kin0 = k * (1.0 + (a[:, 0:W] - 1.0) * k_a)
    kin1 = k * (1.0 + (a[:, W:2 * W] - 1.0) * k_a)
    kin_ref[:, 0:W] = kin0
    kin_ref[:, W:2 * W] = kin1
    b_ref[:, 0:W] = a[:, 0:W] * kk
    b_ref[:, W:2 * W] = a[:, W:2 * W] * kk
    bonus = _sel_dot_r(r * (kin0 + kin1) * r_k_ref[...], ones4, parts=3)
    bonus_ref[...] = bonus * v


def _in_projection(layer, x, mod_l, g1, w_in, wup_bd, aup_bd, g_up, w0, a0, k_k, k_a, r_k, ones4):
    tok = lambda c: pl.BlockSpec((TM, c), lambda i: (i, 0))
    full = lambda a: pl.BlockSpec(a.shape, lambda i: (0,) * a.ndim)
    params = (wup_bd, aup_bd, g_up, w0, a0, k_k, k_a, r_k, ones4)
    widths = (3 * W, 6 * W, 2 * W, 2 * W, 2 * W, W, W, W)
    x_specs, x = _x_specs(x, TM)
    return pl.pallas_call(
        functools.partial(_inproj_kernel, len(x)),
        grid=(N_BLK,),
        in_specs=x_specs + [pl.BlockSpec((1, N_MOD, D), lambda i: (_mod_row(i), 0, 0)), full(g1),
                            pl.BlockSpec((None, D, N_SMALL), lambda i: (layer, 0, 0), pipeline_mode=pl.Buffered(1))]
                 + [full(a) for a in params],
        out_specs=[tok(c) for c in widths],
        out_shape=[jax.ShapeDtypeStruct((T_ALL, c), F32) for c in widths],
        scratch_shapes=[pltpu.VMEM((D, N_SMALL), BF16)],
        compiler_params=_cparams(1),
        name="in_projection",
    )(*x, mod_l, g1, w_in, *params)


T_STEP = SUBLANE
CHAINS_PER_DOT = 4
CTX_SEQ_PER_SCAN_BLK = 4
N_SCAN_BLK = N_CTX_SEQ // CTX_SEQ_PER_SCAN_BLK
LAT_WIN = L_LAT // N_SCAN_BLK
SCAN_FIELDS = ('r', 'v', 'kk', 'w', 'b', 'k')


def _scan_kernel(*refs):
    n_f = len(SCAN_FIELDS)
    ctx_refs = dict(zip(SCAN_FIELDS, refs[:n_f]))
    lat_refs = {}
    pos = n_f
    for s in range(N_LAT_SEQ):
        for d in range(2):
            lat_refs[(s, d)] = dict(zip(SCAN_FIELDS, refs[pos:pos + n_f]))
            pos += n_f
    s0c_ref, s0l_ref, ones4_ref, eye4_ref, yc_ref, sfin_ref, yl_ref, slat_scr = refs[pos:]
    step_i = pl.program_id(0)
    ones4 = ones4_ref[...]
    eye4 = eye4_ref[...]
    chains = ([(False, b, d) for b in range(CTX_SEQ_PER_SCAN_BLK) for d in range(2)]
              + [(True, s, d) for s in range(N_LAT_SEQ) for d in range(2)])
    n_chain = len(chains)
    groups = [chains[g:g + CHAINS_PER_DOT] for g in range(0, n_chain, CHAINS_PER_DOT)]

    @pl.when(step_i == 0)
    def _():
        yl_ref[...] = jnp.zeros_like(yl_ref)
        for s in range(N_LAT_SEQ):
            for d in range(2):
                slat_scr[s, d] = s0l_ref[s, d]

    yc_ref[...] = jnp.zeros_like(yc_ref)

    eye16 = eye4.astype(BF16)
    lane_in_head = jnp.bitwise_and(lax.broadcasted_iota(jnp.int32, (SUBLANE, W), 1), HEAD - 1)
    row_group = lane_in_head // SUBLANE
    on_diag = jnp.bitwise_and(lane_in_head, SUBLANE - 1) == lax.broadcasted_iota(jnp.int32, (SUBLANE, W), 0)

    def diagonal(x):
        m = x[0:SUBLANE]
        for g in range(1, HEAD // SUBLANE):
            m = jnp.where(row_group == g, x[g * SUBLANE:(g + 1) * SUBLANE], m)
        return jnp.sum(jnp.where(on_diag, m, 0.0), axis=0, keepdims=True)

    def row_sum(blocks):
        return _dot(jnp.concatenate(blocks, axis=0), ones4)

    def body(tb, state):
        state = list(state)
        tiles = []
        for lat, b, d in chains:
            t0 = tb * T_STEP if d == 0 else L_CTX - T_STEP - tb * T_STEP
            if lat:
                rows = pl.ds(pl.multiple_of(t0, T_STEP), T_STEP)
                tiles.append({f: lat_refs[(b, d)][f][rows, :] for f in SCAN_FIELDS})
            else:
                rows = pl.ds(pl.multiple_of(b * L_CTX + t0, T_STEP), T_STEP)
                col = lambda f: slice(d * W, (d + 1) * W) if f in ('w', 'b', 'k') else slice(0, W)
                tiles.append({f: ctx_refs[f][rows, col(f)] for f in SCAN_FIELDS})

        def row(name, c, j):
            jj = j if chains[c][2] == 0 else T_STEP - 1 - j
            return tiles[c][name][jj:jj + 1, :]

        def row16(name, c, j):
            return row(name, c, j).astype(BF16)

        state16 = [s.astype(BF16) for s in state]
        y_rows = [[None] * T_STEP for _ in chains]
        def outputs(j, q):
            y_cols = row_sum(q)
            for c, (lat, b, d) in enumerate(chains):
                jj = j if d == 0 else T_STEP - 1 - j
                y_rows[c][jj] = diagonal(y_cols[c * HEAD:(c + 1) * HEAD])

        def feedback(grp_cs, j):
            return row_sum([state16[c] * row16('kk', c, j) for c in grp_cs]
                           + [eye16 * row16('v', c, j) for c in grp_cs])

        group_cs = [[chains.index(ch) for ch in grp] for grp in groups]
        sums = [feedback(cs, 0) for cs in group_cs]
        for j in range(T_STEP):
            q = [None] * n_chain
            for g, cs in enumerate(group_cs):
                n = len(cs)
                for i, c in enumerate(cs):
                    s = (state[c] * row('w', c, j) - sums[g][i * HEAD:(i + 1) * HEAD] * row('b', c, j)
                         + sums[g][(n + i) * HEAD:(n + i + 1) * HEAD] * row('k', c, j))
                    state[c] = s
                    state16[c] = s.astype(BF16)
                    q[c] = state16[c] * row16('r', c, j)
                if j + 1 < T_STEP:
                    sums[g] = feedback(cs, j + 1)
            outputs(j, q)
        for c, (lat, b, d) in enumerate(chains):
            t0 = tb * T_STEP if d == 0 else L_CTX - T_STEP - tb * T_STEP
            y_tile = jnp.concatenate(y_rows[c], axis=0)
            if lat:
                win0 = step_i * LAT_WIN if d == 0 else L_LAT - LAT_WIN - step_i * LAT_WIN
                rows = pl.ds(pl.multiple_of(b * L_LAT + win0 + t0, T_STEP), T_STEP)
                yl_ref[rows, :] += y_tile
            else:
                rows = pl.ds(pl.multiple_of(b * L_CTX + t0, T_STEP), T_STEP)
                yc_ref[rows, :] += y_tile
        return tuple(state)

    init = tuple(slat_scr[b, d] if lat else s0c_ref[b, d] for (lat, b, d) in chains)
    fin = lax.fori_loop(0, L_CTX // T_STEP, body, init)
    for c, (lat, b, d) in enumerate(chains):
        if lat:
            slat_scr[b, d] = fin[c]
        else:
            sfin_ref[b, d] = fin[c]


def _rwkv_scan(rkv, kk, wdec, bvec, kin, s0_ctx, s0_lat, ones4, eye4):
    assert LAT_WIN == L_CTX
    arrays = dict(r=(rkv, 0), v=(rkv, 2), kk=(kk, 0), w=(wdec, None), b=(bvec, None), k=(kin, None))
    rows_c = CTX_SEQ_PER_SCAN_BLK * L_CTX
    operands, in_specs = [], []
    for f in SCAN_FIELDS:
        a, col = arrays[f]
        operands.append(a)
        in_specs.append(pl.BlockSpec((rows_c, W if col is not None else 2 * W),
                                     lambda i, col=col: (i, col if col is not None else 0)))
    win_blk0 = T_CTX // LAT_WIN
    for s in range(N_LAT_SEQ):
        for d in range(2):
            for f in SCAN_FIELDS:
                a, col = arrays[f]
                operands.append(a)
                in_specs.append(pl.BlockSpec(
                    (LAT_WIN, W),
                    lambda i, s=s, d=d, col=col: (win_blk0 + s * N_SCAN_BLK + (i if d == 0 else N_SCAN_BLK - 1 - i),
                                                  col if col is not None else d)))
    st = pl.BlockSpec((None, CTX_SEQ_PER_SCAN_BLK, 2, HEAD, W), lambda i: (i, 0, 0, 0, 0))
    full = lambda a: pl.BlockSpec(a.shape, lambda i: (0,) * a.ndim)
    operands += [s0_ctx, s0_lat, ones4, eye4]
    in_specs += [st, full(s0_lat), full(ones4), full(eye4)]
    return pl.pallas_call(
        _scan_kernel,
        grid=(N_SCAN_BLK,),
        in_specs=in_specs,
        out_specs=[pl.BlockSpec((rows_c, W), lambda i: (i, 0)), st, pl.BlockSpec((T_LAT, W), lambda i: (0, 0))],
        out_shape=[jax.ShapeDtypeStruct((T_CTX, W), F32),
                   jax.ShapeDtypeStruct((N_SCAN_BLK, CTX_SEQ_PER_SCAN_BLK, 2, HEAD, W), F32),
                   jax.ShapeDtypeStruct((T_LAT, W), F32)],
        scratch_shapes=[pltpu.VMEM((N_LAT_SEQ, 2, HEAD, W), F32)],
        compiler_params=_cparams(1),
        name="rwkv_scan",
    )(*operands)


def _branch_kernel(yc_ref, yl_ref, g_ref, bonus_ref, bcd_ref, lnx_g_ref, lnx_b_ref, band_ref, invcnt_ref, shift_ref,
                   wpool_ref, pscale_ref, conv_w_ref, conv_b_ref, sgu_g_ref, ws_ref, bs_ref, ones4_ref, br_ref):
    ones4 = ones4_ref[...]
    group = lax.broadcasted_iota(jnp.int32, (TM, W), 1) // HEAD

    def seg_mean(x):
        return _sel_dot_r(x, ones4, parts=2) * (1.0 / HEAD)

    y = jnp.where(pl.program_id(0) < N_CTX_BLK, yc_ref[...], yl_ref[...])
    mu = seg_mean(y)
    yc = y - mu
    var = seg_mean(yc * yc)
    ya = yc * lax.rsqrt(var + GN_EPS) * lnx_g_ref[...] + lnx_b_ref[...] + bonus_ref[...]
    br_ref[:, 0:W] = (ya * g_ref[...]).astype(BF16)

    u = bcd_ref[:, 0:W]
    u_parts = _split2(u)
    win = jnp.zeros((TM, W), F32)
    for gi in range(len(POOL_WINDOWS)):
        band = band_ref[0, gi]
        s = _dot(band, u_parts[0]) + _dot(band, u_parts[1])
        win = jnp.where(group == gi, s, win)
    pooled = win * invcnt_ref[0] - u
    br_ref[:, W:2 * W] = (_dot_hl(pooled, wpool_ref[...]) * pscale_ref[...]).astype(BF16)

    cin = bcd_ref[:, W:2 * W]
    cb = bcd_ref[:, 2 * W:3 * W]
    cc = bcd_ref[:, 3 * W:4 * W]
    t = cc * cin
    t_prev = _sel_dot_l(shift_ref[0, 0], t, parts=2)
    t_next = _sel_dot_l(shift_ref[0, 1], t, parts=2)
    conv = conv_b_ref[...] + t_prev * conv_w_ref[0:1, :]
    conv = conv + t * conv_w_ref[1:2, :]
    conv = conv + t_next * conv_w_ref[2:3, :]
    br_ref[:, 2 * W:3 * W] = (cb * conv).astype(BF16)

    du = _gelu_tanh(bcd_ref[:, 4 * W:5 * W])
    dv = _gelu_tanh(bcd_ref[:, 5 * W:6 * W])
    mu = seg_mean(dv)
    dc = dv - mu
    var = seg_mean(dc * dc)
    vn = dc * lax.rsqrt(var + NORM_EPS) * sgu_g_ref[...]
    group_c = lax.broadcasted_iota(jnp.int32, (CHUNK, W), 1) // HEAD
    for ch in range(TM // CHUNK):
        rows = slice(ch * CHUNK, (ch + 1) * CHUNK)
        vh, vl = _split2(vn[rows, :])
        s = jnp.zeros((CHUNK, W), F32)
        for gi in range(N_HEADS):
            wh, wl = _split2(ws_ref[gi])
            sg = _dot(wh, vh) + _dot(wl, vh) + _dot(wh, vl)
            s = jnp.where(group_c == gi, sg, s)
        br_ref[rows, 3 * W:4 * W] = (du[rows, :] * (s + bs_ref[...])).astype(BF16)


def _branches(y_ctx, y_lat, g, bonus, bcd, p):
    tok = lambda c: pl.BlockSpec((TM, c), lambda i: (i, 0))
    y_specs = [pl.BlockSpec((TM, W), lambda i: (jnp.minimum(i, N_CTX_BLK - 1), 0)),
               pl.BlockSpec((TM, W), lambda i: (jnp.maximum(i - N_CTX_BLK, 0), 0))]
    full = lambda a: pl.BlockSpec(a.shape, lambda i: (0,) * a.ndim)
    lay = lambda a: pl.BlockSpec((1,) + a.shape[1:], lambda i: (jnp.where(i < N_CTX_BLK, 0, 1),) + (0,) * (a.ndim - 1))
    return pl.pallas_call(
        _branch_kernel,
        grid=(N_BLK,),
        in_specs=y_specs + [tok(W), tok(W), tok(6 * W), full(p['lnx_g']), full(p['lnx_b']),
                  lay(p['band']), lay(p['invcnt']), lay(p['shift']),
                  full(p['wpool_bd']), full(p['pool_scale']), full(p['conv_w']), full(p['conv_b']),
                  full(p['sgu_g']), full(p['w_s']), full(p['bs_full']), full(p['ones4'])],
        out_specs=tok(4 * W),
        out_shape=jax.ShapeDtypeStruct((T_ALL, 4 * W), BF16),
        compiler_params=_cparams(1),
        name="branches",
    )(y_ctx, y_lat, g, bonus, bcd, p['lnx_g'], p['lnx_b'], p['band'], p['invcnt'], p['shift'], p['wpool_bd'],
      p['pool_scale'], p['conv_w'], p['conv_b'], p['sgu_g'], p['w_s'], p['bs_full'], p['ones4'])


def _merge_kernel(n_x, *refs):
    mod_ref, g1_ref, br_ref, wgl_ref, wbr_ref, wout_ref, g2_ref, wr_ref, o_ref, h_ref, aff_ref = refs[n_x:]
    x = _load_x(refs[:n_x])
    h = _norm_mod(x, g1_ref[...], mod_ref[0, 1:2, :], mod_ref[0, 0:1, :]).astype(BF16)
    merged = jnp.zeros(x.shape, F32)
    for i in range(4):
        gl = _dot(h, wgl_ref[:, i * D:(i + 1) * D])
        proj = _dot(br_ref[:, i * W:(i + 1) * W], wbr_ref[i])
        merged = merged + _sigmoid(gl) * proj
    mix = _dot(merged.astype(BF16), wout_ref[...])
    x = x + mod_ref[0, 2:3, :] * mix
    o_ref[...] = x

    h2 = _norm_mod(x, g2_ref[...], mod_ref[0, 4:5, :], mod_ref[0, 3:4, :])
    h_ref[...] = h2.astype(BF16)
    logits = _dot_hl(h2, wr_ref[...])
    lane = lax.broadcasted_iota(jnp.int32, logits.shape, 1)
    logits = jnp.where(lane < N_EXPERTS, logits, -1e30)
    m = jnp.max(logits, axis=-1, keepdims=True)
    e = jnp.where(lane < N_EXPERTS, jnp.exp(logits - m), 0.0)
    aff_ref[...] = e / jnp.sum(e, axis=-1, keepdims=True)


def _merge(x, mod_l, g1, br, w_gl, w_branch, w_out, g2, w_router_pad):
    full = lambda a: pl.BlockSpec(a.shape, lambda i: (0,) * a.ndim)
    tm = TM_WIDE
    tok = lambda c: pl.BlockSpec((tm, c), lambda i: (i, 0))
    x_specs, x = _x_specs(x, tm)
    return pl.pallas_call(
        functools.partial(_merge_kernel, len(x)),
        grid=(T_ALL // tm,),
        in_specs=x_specs + [pl.BlockSpec((1, N_MOD, D), lambda i: (_mod_row(i, tm), 0, 0)), full(g1), tok(4 * W),
                            full(w_gl), full(w_branch), full(w_out), full(g2), full(w_router_pad)],
        out_specs=[tok(D), tok(D), tok(LANE)],
        out_shape=[jax.ShapeDtypeStruct((T_ALL, D), F32), jax.ShapeDtypeStruct((T_ALL, D), BF16),
                   jax.ShapeDtypeStruct((T_ALL, LANE), F32)],
        compiler_params=_cparams(1),
        name="merge",
    )(*x, mod_l, g1, br, w_gl, w_branch, w_out, g2, w_router_pad)


def _rank_kernel(seq_len, aff_ref, afft_ref, ones_ref, rank_ref):
    nb = seq_len // TM
    earlier = jnp.where(lax.broadcasted_iota(jnp.int32, (TM, TM), 1) < lax.broadcasted_iota(jnp.int32, (TM, TM), 0),
                        1.0, 0.0)
    lane = lax.broadcasted_iota(jnp.int32, (TM, LANE), 1)
    for j in range(nb):
        aff = aff_ref[j * TM:(j + 1) * TM, :]
        rank = jnp.zeros((TM, LANE), F32)
        for e in range(N_EXPERTS):
            mine = aff[:, e:e+1]
            tiles = []
            for c in range(nb):
                other = afft_ref[e:e+1, c * TM:(c + 1) * TM]
                if c < j:
                    tiles.append(jnp.where(other >= mine, 1.0, 0.0))
                elif c > j:
                    tiles.append(jnp.where(other > mine, 1.0, 0.0))
                else:
                    tiles.append(jnp.where(other > mine, 1.0, jnp.where(other == mine, earlier, 0.0)))
            cnt = _dot(jnp.concatenate(tiles, axis=1).astype(BF16), ones_ref[...])
            rank = jnp.where(lane == e, cnt, rank)
        rank_ref[j * TM:(j + 1) * TM, :] = rank


def _ranks(seq_len, n_seq, blk0, aff, aff_t):
    nb = seq_len // TM
    ones = jnp.ones((seq_len, LANE), BF16)
    return pl.pallas_call(
        functools.partial(_rank_kernel, seq_len),
        grid=(n_seq,),
        in_specs=[pl.BlockSpec((seq_len, LANE), lambda s: (blk0 // nb + s, 0)),
                  pl.BlockSpec((None, N_EXPERTS, seq_len), lambda s: (s, 0, 0)),
                  pl.BlockSpec((seq_len, LANE), lambda s: (0, 0))],
        out_specs=pl.BlockSpec((seq_len, LANE), lambda s: (s, 0)),
        out_shape=jax.ShapeDtypeStruct((n_seq * seq_len, LANE), F32),
        compiler_params=_cparams(1),
        name=f"ranks_{seq_len}",
    )(aff, aff_t, ones)


def _expert_kernel(h_ref, rank_ref, w1_ref, w3_ref, w2_ref, o_ref, xs_scr):
    def gather(seq_len, cap, tok0, slot0):
        slot = lax.broadcasted_iota(jnp.int32, (cap, seq_len), 0).astype(F32)
        rk = rank_ref[:, tok0:tok0 + seq_len]
        onehot = jnp.where(rk == slot, 1.0, 0.0).astype(BF16)
        xs_scr[slot0:slot0 + cap, :] = _dot(onehot, h_ref[tok0:tok0 + seq_len, :]).astype(BF16)

    for s in range(N_CTX_SEQ):
        gather(L_CTX, CAP_CTX, s * L_CTX, s * CAP_CTX)
    for s in range(N_LAT_SEQ):
        gather(L_LAT, CAP_LAT, T_CTX + s * L_LAT, N_CTX_SEQ * CAP_CTX + s * CAP_LAT)
    xs = xs_scr[...]
    a = _dot(xs, w1_ref[...].astype(BF16))
    act = (a * _sigmoid(a)) * _dot(xs, w3_ref[...].astype(BF16))
    o_ref[...] = _dot(act.astype(BF16), w2_ref[...].astype(BF16)).astype(BF16)


def _experts(layer, h2, rank_t, w1, w3, w2):
    wspec = lambda: pl.BlockSpec((None, None, D, FF), lambda e: (layer, e, 0, 0))
    return pl.pallas_call(
        _expert_kernel,
        grid=(N_EXPERTS,),
        in_specs=[pl.BlockSpec((T_ALL, D), lambda e: (0, 0), pipeline_mode=pl.Buffered(1)),
                  pl.BlockSpec((None, 1, T_ALL), lambda e: (e, 0, 0)),
                  wspec(), wspec(), pl.BlockSpec((None, None, FF, D), lambda e: (layer, e, 0, 0))],
        out_specs=pl.BlockSpec((None, SLOTS, D), lambda e: (e, 0, 0)),
        out_shape=jax.ShapeDtypeStruct((N_EXPERTS, SLOTS, D), BF16),
        scratch_shapes=[pltpu.VMEM((SLOTS, D), BF16)],
        compiler_params=_cparams(1),
        name="experts",
    )(h2, rank_t, w1, w3, w2)


def _combine_kernel(cap, final, x_ref, mod_ref, aff_ref, rank_ref, oe_ref, expand_ref, fg_ref, o_ref):
    n = N_EXPERTS * cap
    expand = expand_ref[...]
    rank_x = _sel_dot_r(rank_ref[...], expand, parts=2)
    aff_x = _dot(aff_ref[...].astype(BF16), expand)
    slot = jnp.bitwise_and(lax.broadcasted_iota(jnp.int32, (TM, n), 1), cap - 1).astype(F32)
    gate = jnp.where(rank_x == slot, aff_x, 0.0)
    acc = _dot(gate.astype(BF16), oe_ref[...].reshape(n, D))
    x = x_ref[...] + mod_ref[0, 5:6, :] * acc
    if final:
        ms = jnp.mean(x * x, axis=-1, keepdims=True)
        x = x * lax.rsqrt(ms + NORM_EPS) * fg_ref[...]
    o_ref[...] = x


def _combine(seq_len, n_seq, blk0, cap, slot_blk0, final, x, mod_l, aff, rank, oe, final_g):
    nb = seq_len // TM
    assert cap & (cap - 1) == 0
    tok = lambda c: pl.BlockSpec((TM, c), lambda s, j: (blk0 + s * nb + j, 0))
    expand = jnp.asarray(np.arange(LANE)[:, None] == np.arange(N_EXPERTS * cap)[None, :] // cap, BF16)
    if final:
        out_spec = pl.BlockSpec((TM, D), lambda s, j: (s * nb + j, 0))
        out_shape, aliases = jax.ShapeDtypeStruct((n_seq * seq_len, D), F32), {}
    else:
        out_spec, out_shape, aliases = tok(D), jax.ShapeDtypeStruct((T_ALL, D), F32), {0: 0}
    return pl.pallas_call(
        functools.partial(_combine_kernel, cap, final),
        grid=(n_seq, nb),
        in_specs=[tok(D),
                  pl.BlockSpec((1, N_MOD, D), lambda s, j: (_mod_row(blk0 + s * nb + j), 0, 0)),
                  tok(LANE),
                  pl.BlockSpec((TM, LANE), lambda s, j: (s * nb + j, 0)),
                  pl.BlockSpec((N_EXPERTS, cap, D), lambda s, j: (0, slot_blk0 + s, 0)),
                  pl.BlockSpec((LANE, N_EXPERTS * cap), lambda s, j: (0, 0)),
                  pl.BlockSpec((1, D), lambda s, j: (0, 0))],
        out_specs=out_spec,
        out_shape=out_shape,
        input_output_aliases=aliases,
        compiler_params=_cparams(2),
        name=f"combine_{seq_len}",
    )(x, mod_l, aff, rank, oe, expand, final_g)


def _row_structure(row_len):
    t = np.arange(TM)
    same_row = (t[:, None] // row_len) == (t[None, :] // row_len)
    delta = t[None, :] - t[:, None]
    band = np.stack([same_row & (delta >= -(w // 2)) & (delta < w // 2) for w in POOL_WINDOWS]).astype(np.float32)
    cnt = band.sum(-1)
    invcnt = np.repeat((1.0 / cnt).T, HEAD, axis=1).astype(np.float32)
    shift = np.stack([same_row & (delta == -1), same_row & (delta == 1)]).astype(np.float32)
    return band, invcnt, shift


def _constants():
    lane = np.arange(W)
    ones4 = (lane[:, None] // HEAD == lane[None, :] // HEAD).astype(np.float32)
    eye4 = (np.arange(HEAD)[:, None] == lane[None, :] % HEAD).astype(np.float32)
    structs = [_row_structure(L_CTX), _row_structure(GRID_W)]
    return dict(
        ones4=jnp.asarray(ones4, BF16), eye4=jnp.asarray(eye4, F32),
        band=jnp.asarray(np.stack([s[0] for s in structs]), BF16),
        invcnt=jnp.asarray(np.stack([s[1] for s in structs]), F32),
        shift=jnp.asarray(np.stack([s[2] for s in structs]), BF16))


def _block_diag(blocks):
    n = len(blocks)
    rows = []
    for i, b in enumerate(blocks):
        rows.append(jnp.concatenate([b if j == i else jnp.zeros((b.shape[0], blocks[j].shape[1]), b.dtype)
                                     for j in range(n)], axis=1))
    return jnp.concatenate(rows, axis=0)


def _wide_tiles(s):
    return jnp.moveaxis(s, -3, -2).reshape(s.shape[:-3] + (HEAD, W))


def _head_tiles(s):
    return jnp.moveaxis(s.reshape(s.shape[:-2] + (HEAD, N_HEADS, HEAD)), -2, -3)


def kernel(x_prompt, x_sample, state_rwkv, c, c_ctx, norm1_g, w_mod, b_mod, w_in, w0, w_up, a0, a_up, g_up, k_k, k_a, r_k, lnx_g, lnx_b, w_pool, pool_scale, conv_w, conv_b, sgu_g, w_s, b_s, w_branch, w_out, norm2_g, w_router, w_e1, w_e3, w_e2, final_g):
    const = _constants()
    x = (x_prompt.reshape(T_CTX, D), x_sample.reshape(T_LAT, D))
    cond8 = jnp.concatenate([c_ctx[None], c, jnp.zeros((SUBLANE - 1 - N_LAT_SEQ, D), F32)], axis=0)
    mod = _modulation(cond8, w_mod, b_mod).reshape(DEPTH, SUBLANE, N_MOD, D)
    final_g2 = final_g.reshape(1, D)
    ctx_states = []
    y_ctx = y_lat = None
    for l in range(DEPTH):
        mod_l = mod[l]
        g1 = norm1_g[l].reshape(1, D)
        wup_bd = _block_diag([w_up[l, 0], w_up[l, 1]])
        aup_bd = _block_diag([a_up[l, 0], a_up[l, 1]])
        rkv, bcd, wdec, bvec, kin, kk, gate_a, bonus = _in_projection(
            l, x, mod_l, g1, w_in, wup_bd, aup_bd, g_up[l], w0[l].reshape(1, 2 * W),
            a0[l].reshape(1, 2 * W), k_k[l].reshape(1, W), k_a[l].reshape(1, W), r_k[l].reshape(1, W),
            const['ones4'])

        s0_ctx = jnp.zeros((N_SCAN_BLK, CTX_SEQ_PER_SCAN_BLK, 2, HEAD, W), F32)
        y_c, s_ctx, y_l = _rwkv_scan(rkv, kk, wdec, bvec, kin, s0_ctx, _wide_tiles(state_rwkv[:, l]),
                                     const['ones4'], const['eye4'])
        ctx_states.append(_head_tiles(s_ctx).reshape(N_CTX_SEQ, 2, N_HEADS, HEAD, HEAD))

        bs_full = jnp.repeat(b_s[l].T, HEAD, axis=1)
        bp = dict(lnx_g=lnx_g[l].reshape(1, W), lnx_b=lnx_b[l].reshape(1, W), band=const['band'],
                  invcnt=const['invcnt'], shift=const['shift'],
                  wpool_bd=_block_diag([w_pool[l, i] for i in range(len(POOL_WINDOWS))]),
                  pool_scale=pool_scale[l].reshape(1, W), conv_w=conv_w[l], conv_b=conv_b[l].reshape(1, W),
                  sgu_g=sgu_g[l].reshape(1, W), w_s=w_s[l], bs_full=bs_full, ones4=const['ones4'])
        br = _branches(y_c, y_l, gate_a, bonus, bcd, bp)
        wr_pad = jnp.concatenate([w_router[l], jnp.zeros((D, LANE - N_EXPERTS), F32)], axis=1)
        x, h2, aff = _merge(x, mod_l, g1, br, w_in[l, :, N_SMALL:].astype(BF16), w_branch[l].astype(BF16),
                            w_out[l].astype(BF16), norm2_g[l].reshape(1, D), wr_pad)
        aff_t_ctx = jnp.swapaxes(aff[:T_CTX, :N_EXPERTS].reshape(N_CTX_SEQ, L_CTX, N_EXPERTS), 1, 2)
        aff_t_lat = jnp.swapaxes(aff[T_CTX:, :N_EXPERTS].reshape(N_LAT_SEQ, L_LAT, N_EXPERTS), 1, 2)
        rank_ctx = _ranks(L_CTX, N_CTX_SEQ, 0, aff, aff_t_ctx)
        rank_lat = _ranks(L_LAT, N_LAT_SEQ, N_CTX_BLK, aff, aff_t_lat)
        rank_t = jnp.concatenate([rank_ctx[:, :N_EXPERTS], rank_lat[:, :N_EXPERTS]], axis=0).T
        oe = _experts(l, h2, rank_t.reshape(N_EXPERTS, 1, T_ALL), w_e1, w_e3, w_e2)
        final = l == DEPTH - 1
        y_ctx = _combine(L_CTX, N_CTX_SEQ, 0, CAP_CTX, 0, final, x, mod_l, aff, rank_ctx, oe, final_g2)
        y_lat = _combine(L_LAT, N_LAT_SEQ, N_CTX_BLK, CAP_LAT, N_CTX_SEQ * CAP_CTX // CAP_LAT, final,
                         x if final else y_ctx, mod_l, aff, rank_lat, oe, final_g2)
        if not final:
            x = y_lat
    y_prompt = y_ctx.reshape(N_CTX_SEQ, L_CTX, D)
    y_sample = y_lat.reshape(N_LAT_SEQ, L_LAT, D)
    new_state = jnp.stack(ctx_states, axis=1)
    return (y_prompt, y_sample, new_state)
```

```python
import functools
import math

import numpy as np
import jax
import jax.numpy as jnp
from jax import lax
from jax.experimental import pallas as pl
from jax.experimental.pallas import tpu as pltpu

F32 = jnp.float32
BF16 = jnp.bfloat16

D = 1024
N_CTX_SEQ, L_CTX = 16, 256
N_LAT_SEQ, L_LAT = 2, 1024
T_CTX = N_CTX_SEQ * L_CTX
T_LAT = N_LAT_SEQ * L_LAT
T_ALL = T_CTX + T_LAT
DEPTH = 2
TM = 256
TM_WIDE = 512
N_BLK = T_ALL // TM
N_CTX_BLK = T_CTX // TM
LAT_BLK_PER_SEQ = L_LAT // TM
GRID_W = 64
W = 256
HEAD = 64
N_HEADS = W // HEAD
LORA = 64
LORA_G = 128
DECAY_SCALE = math.exp(-0.5)
POOL_WINDOWS = (2, 4, 8, 16)
CHUNK = 128
N_EXPERTS = 16
FF = 1024
CAP_CTX = 2 * L_CTX // N_EXPERTS
CAP_LAT = 2 * L_LAT // N_EXPERTS
SLOTS = N_CTX_SEQ * CAP_CTX + N_LAT_SEQ * CAP_LAT
N_MOD = 6
NORM_EPS = 1e-6
GN_EPS = 64e-5
N_SMALL = 3 * W + 2 * LORA + 2 * LORA + LORA_G + 6 * W
LANE = 128
SUBLANE = 8
VMEM_LIMIT = 56 * 1024 * 1024


def _cparams(n_axes):
    return pltpu.CompilerParams(dimension_semantics=("arbitrary",) * n_axes,
                                vmem_limit_bytes=VMEM_LIMIT)


def _split2(x):
    hi = x.astype(BF16)
    lo = (x - hi.astype(F32)).astype(BF16)
    return hi, lo


def _split3(x):
    hi = x.astype(BF16)
    r = x - hi.astype(F32)
    mid = r.astype(BF16)
    lo = (r - mid.astype(F32)).astype(BF16)
    return hi, mid, lo


def _dot(a, b):
    return jnp.dot(a, b, preferred_element_type=F32)


def _sel_dot_l(m_bf16, x, parts=2):
    ps = _split2(x) if parts == 2 else _split3(x)
    acc = _dot(m_bf16, ps[0])
    for p in ps[1:]:
        acc = acc + _dot(m_bf16, p)
    return acc


def _sel_dot_r(x, m_bf16, parts=2):
    ps = _split2(x) if parts == 2 else _split3(x)
    acc = _dot(ps[0], m_bf16)
    for p in ps[1:]:
        acc = acc + _dot(p, m_bf16)
    return acc


def _dot_hl(a, b):
    ah, al = _split2(a)
    bh, bl = _split2(b)
    return _dot(ah, bh) + _dot(al, bh) + _dot(ah, bl)


def _sigmoid(x):
    return 1.0 / (1.0 + jnp.exp(-x))


def _gelu_tanh(x):
    return 0.5 * x * (1.0 + jnp.tanh(math.sqrt(2.0 / math.pi) * (x + 0.044715 * (x * x * x))))


def _norm_mod(x, g, scale, shift):
    ms = jnp.mean(x * x, axis=-1, keepdims=True)
    return (x * lax.rsqrt(ms + NORM_EPS) * g) * (1.0 + scale) + shift


def _mod_row(i, tm=TM):
    return jnp.where(i < T_CTX // tm, 0, (i - T_CTX // tm) // (L_LAT // tm) + 1)


def _mod_kernel(c_ref, w_ref, b_ref, o_ref):
    c = c_ref[...]
    s = c * _sigmoid(c)
    o_ref[...] = _dot(s.astype(BF16), w_ref[...].astype(BF16)) + b_ref[...]


def _modulation(cond8, w_mod, b_mod):
    tn = 1536
    n = N_MOD * D
    return pl.pallas_call(
        _mod_kernel,
        grid=(DEPTH, n // tn),
        in_specs=[pl.BlockSpec((SUBLANE, D), lambda l, j: (0, 0)),
                  pl.BlockSpec((None, D, tn), lambda l, j: (l, 0, j)),
                  pl.BlockSpec((None, 1, tn), lambda l, j: (l, 0, j))],
        out_specs=pl.BlockSpec((None, SUBLANE, tn), lambda l, j: (l, 0, j)),
        out_shape=jax.ShapeDtypeStruct((DEPTH, SUBLANE, n), F32),
        compiler_params=_cparams(2),
        name="modulation",
    )(cond8, w_mod, b_mod.reshape(DEPTH, 1, n))


def _x_specs(x, tm):
    if not isinstance(x, tuple):
        return [pl.BlockSpec((tm, D), lambda i: (i, 0))], (x,)
    n = T_CTX // tm
    return [pl.BlockSpec((tm, D), lambda i: (jnp.minimum(i, n - 1), 0)),
            pl.BlockSpec((tm, D), lambda i: (jnp.maximum(i - n, 0), 0))], x


def _load_x(x_refs):
    if len(x_refs) == 1:
        return x_refs[0][...]
    n = T_CTX // x_refs[0].shape[0]
    return jnp.where(pl.program_id(0) < n, x_refs[0][...], x_refs[1][...])


def _inproj_kernel(n_x, *refs):
    (mod_ref, g_ref, w_ref, wup_ref, aup_ref, gup_ref, w0_ref, a0_ref, kk_k_ref, k_a_ref, r_k_ref, ones4_ref,
     rkv_ref, bcd_ref, wdec_ref, b_ref, kin_ref, kk_ref, gate_ref, bonus_ref, w16_scr) = refs[n_x:]

    @pl.when(pl.program_id(0) == 0)
    def _():
        w16_scr[...] = w_ref[...].astype(BF16)

    h = _norm_mod(_load_x(refs[:n_x]), g_ref[...], mod_ref[0, 1:2, :], mod_ref[0, 0:1, :])
    z = _dot(h.astype(BF16), w16_scr[...])
    rkv_ref[...] = z[:, :3 * W]
    bcd_ref[...] = z[:, 3 * W + 3 * LANE:]
    r = z[:, 0:W]
    k = z[:, W:2 * W]
    v = z[:, 2 * W:3 * W]
    xw = z[:, 3 * W:3 * W + LANE]
    xa = z[:, 3 * W + LANE:3 * W + 2 * LANE]
    xg = z[:, 3 * W + 2 * LANE:3 * W + 3 * LANE]
    ones4 = ones4_ref[...]
    dec = w0_ref[...] + _dot_hl(jnp.tanh(xw), wup_ref[...])
    wdec_ref[...] = jnp.exp(-DECAY_SCALE * _sigmoid(dec))
    a = _sigmoid(a0_ref[...] + _dot_hl(xa, aup_ref[...]))
    gate_ref[...] = _dot_hl(_sigmoid(xg), gup_ref[...])
    kk = k * kk_k_ref[...]
    n2 = _sel_dot_r(kk * kk, ones4, parts=3)
    kk = kk / jnp.maximum(jnp.sqrt(n2), 1e-12)
    kk_ref[...] = kk
    k_a = k_a_ref[...]
    kin0 = k * (1.0 + (a[:, 0:W] - 1.0) * k_a)
    kin1 = k * (1.0 + (a[:, W:2 * W] - 1.0) * k_a)
    kin_ref[:, 0:W] = kin0
    kin_ref[:, W:2 * W] = kin1
    b_ref[:, 0:W] = a[:, 0:W] * kk
    b_ref[:, W:2 * W] = a[:, W:2 * W] * kk
    bonus = _sel_dot_r(r * (kin0 + kin1) * r_k_ref[...], ones4, parts=3)
    bonus_ref[...] = bonus * v


def _in_projection(layer, x, mod_l, g1, w_in, wup_bd, aup_bd, g_up, w0, a0, k_k, k_a, r_k, ones4):
    tm = TM_WIDE
    tok = lambda c: pl.BlockSpec((tm, c), lambda i: (i, 0))
    full = lambda a: pl.BlockSpec(a.shape, lambda i: (0,) * a.ndim)
    params = (wup_bd, aup_bd, g_up, w0, a0, k_k, k_a, r_k, ones4)
    widths = (3 * W, 6 * W, 2 * W, 2 * W, 2 * W, W, W, W)
    x_specs, x = _x_specs(x, tm)
    return pl.pallas_call(
        functools.partial(_inproj_kernel, len(x)),
        grid=(T_ALL // tm,),
        in_specs=x_specs + [pl.BlockSpec((1, N_MOD, D), lambda i: (_mod_row(i, tm), 0, 0)), full(g1),
                            pl.BlockSpec((None, D, N_SMALL), lambda i: (layer, 0, 0), pipeline_mode=pl.Buffered(1))]
                 + [full(a) for a in params],
        out_specs=[tok(c) for c in widths],
        out_shape=[jax.ShapeDtypeStruct((T_ALL, c), F32) for c in widths],
        scratch_shapes=[pltpu.VMEM((D, N_SMALL), BF16)],
        compiler_params=_cparams(1),
        name="in_projection",
    )(*x, mod_l, g1, w_in, *params)


T_STEP = SUBLANE
CHAINS_PER_DOT = 4
CTX_SEQ_PER_SCAN_BLK = 4
N_SCAN_BLK = N_CTX_SEQ // CTX_SEQ_PER_SCAN_BLK
LAT_WIN = L_LAT // N_SCAN_BLK
SCAN_FIELDS = ('r', 'v', 'kk', 'w', 'b', 'k')


def _scan_kernel(*refs):
    n_f = len(SCAN_FIELDS)
    ctx_refs = dict(zip(SCAN_FIELDS, refs[:n_f]))
    lat_refs = {}
    pos = n_f
    for s in range(N_LAT_SEQ):
        for d in range(2):
            lat_refs[(s, d)] = dict(zip(SCAN_FIELDS, refs[pos:pos + n_f]))
            pos += n_f
    s0c_ref, s0l_ref, ones4_ref, eye4_ref, yc_ref, sfin_ref, yl_ref, slat_scr = refs[pos:]
    step_i = pl.program_id(0)
    ones4 = ones4_ref[...]
    eye4 = eye4_ref[...]
    chains = ([(False, b, d) for b in range(CTX_SEQ_PER_SCAN_BLK) for d in range(2)]
              + [(True, s, d) for s in range(N_LAT_SEQ) for d in range(2)])
    n_chain = len(chains)
    groups = [chains[g:g + CHAINS_PER_DOT] for g in range(0, n_chain, CHAINS_PER_DOT)]

    @pl.when(step_i == 0)
    def _():
        yl_ref[...] = jnp.zeros_like(yl_ref)
        for s in range(N_LAT_SEQ):
            for d in range(2):
                slat_scr[s, d] = s0l_ref[s, d]

    yc_ref[...] = jnp.zeros_like(yc_ref)

    eye16 = eye4.astype(BF16)
    lane_in_head = jnp.bitwise_and(lax.broadcasted_iota(jnp.int32, (SUBLANE, W), 1), HEAD - 1)
    row_group = lane_in_head // SUBLANE
    on_diag = jnp.bitwise_and(lane_in_head, SUBLANE - 1) == lax.broadcasted_iota(jnp.int32, (SUBLANE, W), 0)

    def diagonal(x):
        m = x[0:SUBLANE]
        for g in range(1, HEAD // SUBLANE):
            m = jnp.where(row_group == g, x[g * SUBLANE:(g + 1) * SUBLANE], m)
        return jnp.sum(jnp.where(on_diag, m, 0.0), axis=0, keepdims=True)

    def row_sum(blocks):
        return _dot(jnp.concatenate(blocks, axis=0), ones4)

    def body(tb, state):
        state = list(state)
        tiles = []
        for lat, b, d in chains:
            t0 = tb * T_STEP if d == 0 else L_CTX - T_STEP - tb * T_STEP
            if lat:
                rows = pl.ds(pl.multiple_of(t0, T_STEP), T_STEP)
                tiles.append({f: lat_refs[(b, d)][f][rows, :] for f in SCAN_FIELDS})
            else:
                rows = pl.ds(pl.multiple_of(b * L_CTX + t0, T_STEP), T_STEP)
                col = lambda f: slice(d * W, (d + 1) * W) if f in ('w', 'b', 'k') else slice(0, W)
                tiles.append({f: ctx_refs[f][rows, col(f)] for f in SCAN_FIELDS})

        def row(name, c, j):
            jj = j if chains[c][2] == 0 else T_STEP - 1 - j
            return tiles[c][name][jj:jj + 1, :]

        def row16(name, c, j):
            return row(name, c, j).astype(BF16)

        state16 = [s.astype(BF16) for s in state]
        y_rows = [[None] * T_STEP for _ in chains]
        def outputs(j, q):
            y_cols = row_sum(q)
            for c, (lat, b, d) in enumerate(chains):
                jj = j if d == 0 else T_STEP - 1 - j
                y_rows[c][jj] = diagonal(y_cols[c * HEAD:(c + 1) * HEAD])

        def feedback(grp_cs, j):
            return row_sum([state16[c] * row16('kk', c, j) for c in grp_cs]
                           + [eye16 * row16('v', c, j) for c in grp_cs])

        group_cs = [[chains.index(ch) for ch in grp] for grp in groups]
        sums = [feedback(cs, 0) for cs in group_cs]
        for j in range(T_STEP):
            q = [None] * n_chain
            for g, cs in enumerate(group_cs):
                n = len(cs)
                for i, c in enumerate(cs):
                    s = (state[c] * row('w', c, j) - sums[g][i * HEAD:(i + 1) * HEAD] * row('b', c, j)
                         + sums[g][(n + i) * HEAD:(n + i + 1) * HEAD] * row('k', c, j))
                    state[c] = s
                    state16[c] = s.astype(BF16)
                    q[c] = state16[c] * row16('r', c, j)
                if j + 1 < T_STEP:
                    sums[g] = feedback(cs, j + 1)
            outputs(j, q)
        for c, (lat, b, d) in enumerate(chains):
            t0 = tb * T_STEP if d == 0 else L_CTX - T_STEP - tb * T_STEP
            y_tile = jnp.concatenate(y_rows[c], axis=0)
            if lat:
                win0 = step_i * LAT_WIN if d == 0 else L_LAT - LAT_WIN - step_i * LAT_WIN
                rows = pl.ds(pl.multiple_of(b * L_LAT + win0 + t0, T_STEP), T_STEP)
                yl_ref[rows, :] += y_tile
            else:
                rows = pl.ds(pl.multiple_of(b * L_CTX + t0, T_STEP), T_STEP)
                yc_ref[rows, :] += y_tile
        return tuple(state)

    init = tuple(slat_scr[b, d] if lat else s0c_ref[b, d] for (lat, b, d) in chains)
    fin = lax.fori_loop(0, L_CTX // T_STEP, body, init)
    for c, (lat, b, d) in enumerate(chains):
        if lat:
            slat_scr[b, d] = fin[c]
        else:
            sfin_ref[b, d] = fin[c]


def _rwkv_scan(rkv, kk, wdec, bvec, kin, s0_ctx, s0_lat, ones4, eye4):
    assert LAT_WIN == L_CTX
    arrays = dict(r=(rkv, 0), v=(rkv, 2), kk=(kk, 0), w=(wdec, None), b=(bvec, None), k=(kin, None))
    rows_c = CTX_SEQ_PER_SCAN_BLK * L_CTX
    operands, in_specs = [], []
    for f in SCAN_FIELDS:
        a, col = arrays[f]
        operands.append(a)
        in_specs.append(pl.BlockSpec((rows_c, W if col is not None else 2 * W),
                                     lambda i, col=col: (i, col if col is not None else 0)))
    win_blk0 = T_CTX // LAT_WIN
    for s in range(N_LAT_SEQ):
        for d in range(2):
            for f in SCAN_FIELDS:
                a, col = arrays[f]
                operands.append(a)
                in_specs.append(pl.BlockSpec(
                    (LAT_WIN, W),
                    lambda i, s=s, d=d, col=col: (win_blk0 + s * N_SCAN_BLK + (i if d == 0 else N_SCAN_BLK - 1 - i),
                                                  col if col is not None else d)))
    st = pl.BlockSpec((None, CTX_SEQ_PER_SCAN_BLK, 2, HEAD, W), lambda i: (i, 0, 0, 0, 0))
    full = lambda a: pl.BlockSpec(a.shape, lambda i: (0,) * a.ndim)
    operands += [s0_ctx, s0_lat, ones4, eye4]
    in_specs += [st, full(s0_lat), full(ones4), full(eye4)]
    return pl.pallas_call(
        _scan_kernel,
        grid=(N_SCAN_BLK,),
        in_specs=in_specs,
        out_specs=[pl.BlockSpec((rows_c, W), lambda i: (i, 0)), st, pl.BlockSpec((T_LAT, W), lambda i: (0, 0))],
        out_shape=[jax.ShapeDtypeStruct((T_CTX, W), F32),
                   jax.ShapeDtypeStruct((N_SCAN_BLK, CTX_SEQ_PER_SCAN_BLK, 2, HEAD, W), F32),
                   jax.ShapeDtypeStruct((T_LAT, W), F32)],
        scratch_shapes=[pltpu.VMEM((N_LAT_SEQ, 2, HEAD, W), F32)],
        compiler_params=_cparams(1),
        name="rwkv_scan",
    )(*operands)


def _branch_kernel(yc_ref, yl_ref, g_ref, bonus_ref, bcd_ref, lnx_g_ref, lnx_b_ref, band_ref, invcnt_ref, shift_ref,
                   wpool_ref, pscale_ref, conv_w_ref, conv_b_ref, sgu_g_ref, ws_ref, bs_ref, ones4_ref, br_ref):
    ones4 = ones4_ref[...]
    group = lax.broadcasted_iota(jnp.int32, (TM, W), 1) // HEAD

    def seg_mean(x):
        return _sel_dot_r(x, ones4, parts=2) * (1.0 / HEAD)

    y = jnp.where(pl.program_id(0) < N_CTX_BLK, yc_ref[...], yl_ref[...])
    mu = seg_mean(y)
    yc = y - mu
    var = seg_mean(yc * yc)
    ya = yc * lax.rsqrt(var + GN_EPS) * lnx_g_ref[...] + lnx_b_ref[...] + bonus_ref[...]
    br_ref[:, 0:W] = (ya * g_ref[...]).astype(BF16)

    u = bcd_ref[:, 0:W]
    u_parts = _split2(u)
    win = jnp.zeros((TM, W), F32)
    for gi in range(len(POOL_WINDOWS)):
        band = band_ref[0, gi]
        s = _dot(band, u_parts[0]) + _dot(band, u_parts[1])
        win = jnp.where(group == gi, s, win)
    pooled = win * invcnt_ref[0] - u
    br_ref[:, W:2 * W] = (_dot_hl(pooled, wpool_ref[...]) * pscale_ref[...]).astype(BF16)

    cin = bcd_ref[:, W:2 * W]
    cb = bcd_ref[:, 2 * W:3 * W]
    cc = bcd_ref[:, 3 * W:4 * W]
    t = cc * cin
    t_prev = _sel_dot_l(shift_ref[0, 0], t, parts=2)
    t_next = _sel_dot_l(shift_ref[0, 1], t, parts=2)
    conv = conv_b_ref[...] + t_prev * conv_w_ref[0:1, :]
    conv = conv + t * conv_w_ref[1:2, :]
    conv = conv + t_next * conv_w_ref[2:3, :]
    br_ref[:, 2 * W:3 * W] = (cb * conv).astype(BF16)

    du = _gelu_tanh(bcd_ref[:, 4 * W:5 * W])
    dv = _gelu_tanh(bcd_ref[:, 5 * W:6 * W])
    mu = seg_mean(dv)
    dc = dv - mu
    var = seg_mean(dc * dc)
    vn = dc * lax.rsqrt(var + NORM_EPS) * sgu_g_ref[...]
    group_c = lax.broadcasted_iota(jnp.int32, (CHUNK, W), 1) // HEAD
    for ch in range(TM // CHUNK):
        rows = slice(ch * CHUNK, (ch + 1) * CHUNK)
        vh, vl = _split2(vn[rows, :])
        s = jnp.zeros((CHUNK, W), F32)
        for gi in range(N_HEADS):
            wh, wl = _split2(ws_ref[gi])
            sg = _dot(wh, vh) + _dot(wl, vh) + _dot(wh, vl)
            s = jnp.where(group_c == gi, sg, s)
        br_ref[rows, 3 * W:4 * W] = (du[rows, :] * (s + bs_ref[...])).astype(BF16)


def _branches(y_ctx, y_lat, g, bonus, bcd, p):
    tok = lambda c: pl.BlockSpec((TM, c), lambda i: (i, 0))
    y_specs = [pl.BlockSpec((TM, W), lambda i: (jnp.minimum(i, N_CTX_BLK - 1), 0)),
               pl.BlockSpec((TM, W), lambda i: (jnp.maximum(i - N_CTX_BLK, 0), 0))]
    full = lambda a: pl.BlockSpec(a.shape, lambda i: (0,) * a.ndim)
    lay = lambda a: pl.BlockSpec((1,) + a.shape[1:], lambda i: (jnp.where(i < N_CTX_BLK, 0, 1),) + (0,) * (a.ndim - 1))
    return pl.pallas_call(
        _branch_kernel,
        grid=(N_BLK,),
        in_specs=y_specs + [tok(W), tok(W), tok(6 * W), full(p['lnx_g']), full(p['lnx_b']),
                  lay(p['band']), lay(p['invcnt']), lay(p['shift']),
                  full(p['wpool_bd']), full(p['pool_scale']), full(p['conv_w']), full(p['conv_b']),
                  full(p['sgu_g']), full(p['w_s']), full(p['bs_full']), full(p['ones4'])],
        out_specs=tok(4 * W),
        out_shape=jax.ShapeDtypeStruct((T_ALL, 4 * W), BF16),
        compiler_params=_cparams(1),
        name="branches",
    )(y_ctx, y_lat, g, bonus, bcd, p['lnx_g'], p['lnx_b'], p['band'], p['invcnt'], p['shift'], p['wpool_bd'],
      p['pool_scale'], p['conv_w'], p['conv_b'], p['sgu_g'], p['w_s'], p['bs_full'], p['ones4'])


def _merge_kernel(n_x, *refs):
    mod_ref, g1_ref, br_ref, wgl_ref, wbr_ref, wout_ref, g2_ref, wr_ref, o_ref, h_ref, aff_ref = refs[n_x:]
    x = _load_x(refs[:n_x])
    h = _norm_mod(x, g1_ref[...], mod_ref[0, 1:2, :], mod_ref[0, 0:1, :]).astype(BF16)
    merged = jnp.zeros(x.shape, F32)
    for i in range(4):
        gl = _dot(h, wgl_ref[:, i * D:(i + 1) * D])
        proj = _dot(br_ref[:, i * W:(i + 1) * W], wbr_ref[i])
        merged = merged + _sigmoid(gl) * proj
    mix = _dot(merged.astype(BF16), wout_ref[...])
    x = x + mod_ref[0, 2:3, :] * mix
    o_ref[...] = x

    h2 = _norm_mod(x, g2_ref[...], mod_ref[0, 4:5, :], mod_ref[0, 3:4, :])
    h_ref[...] = h2.astype(BF16)
    logits = _dot_hl(h2, wr_ref[...])
    lane = lax.broadcasted_iota(jnp.int32, logits.shape, 1)
    logits = jnp.where(lane < N_EXPERTS, logits, -1e30)
    m = jnp.max(logits, axis=-1, keepdims=True)
    e = jnp.where(lane < N_EXPERTS, jnp.exp(logits - m), 0.0)
    aff_ref[...] = e / jnp.sum(e, axis=-1, keepdims=True)


def _merge(x, mod_l, g1, br, w_gl, w_branch, w_out, g2, w_router_pad):
    full = lambda a: pl.BlockSpec(a.shape, lambda i: (0,) * a.ndim)
    tm = TM_WIDE
    tok = lambda c: pl.BlockSpec((tm, c), lambda i: (i, 0))
    x_specs, x = _x_specs(x, tm)
    return pl.pallas_call(
        functools.partial(_merge_kernel, len(x)),
        grid=(T_ALL // tm,),
        in_specs=x_specs + [pl.BlockSpec((1, N_MOD, D), lambda i: (_mod_row(i, tm), 0, 0)), full(g1), tok(4 * W),
                            full(w_gl), full(w_branch), full(w_out), full(g2), full(w_router_pad)],
        out_specs=[tok(D), tok(D), tok(LANE)],
        out_shape=[jax.ShapeDtypeStruct((T_ALL, D), F32), jax.ShapeDtypeStruct((T_ALL, D), BF16),
                   jax.ShapeDtypeStruct((T_ALL, LANE), F32)],
        compiler_params=_cparams(1),
        name="merge",
    )(*x, mod_l, g1, br, w_gl, w_branch, w_out, g2, w_router_pad)


def _rank_kernel(seq_len, aff_ref, afft_ref, ones_ref, rank_ref):
    nb = seq_len // TM
    earlier = jnp.where(lax.broadcasted_iota(jnp.int32, (TM, TM), 1) < lax.broadcasted_iota(jnp.int32, (TM, TM), 0),
                        1.0, 0.0)
    lane = lax.broadcasted_iota(jnp.int32, (TM, LANE), 1)
    for j in range(nb):
        aff = aff_ref[j * TM:(j + 1) * TM, :]
        rank = jnp.zeros((TM, LANE), F32)
        for e in range(N_EXPERTS):
            mine = aff[:, e:e+1]
            tiles = []
            for c in range(nb):
                other = afft_ref[e:e+1, c * TM:(c + 1) * TM]
                if c < j:
                    tiles.append(jnp.where(other >= mine, 1.0, 0.0))
                elif c > j:
                    tiles.append(jnp.where(other > mine, 1.0, 0.0))
                else:
                    tiles.append(jnp.where(other > mine, 1.0, jnp.where(other == mine, earlier, 0.0)))
            cnt = _dot(jnp.concatenate(tiles, axis=1).astype(BF16), ones_ref[...])
            rank = jnp.where(lane == e, cnt, rank)
        rank_ref[j * TM:(j + 1) * TM, :] = rank


def _ranks(seq_len, n_seq, blk0, aff, aff_t):
    nb = seq_len // TM
    ones = jnp.ones((seq_len, LANE), BF16)
    return pl.pallas_call(
        functools.partial(_rank_kernel, seq_len),
        grid=(n_seq,),
        in_specs=[pl.BlockSpec((seq_len, LANE), lambda s: (blk0 // nb + s, 0)),
                  pl.BlockSpec((None, N_EXPERTS, seq_len), lambda s: (s, 0, 0)),
                  pl.BlockSpec((seq_len, LANE), lambda s: (0, 0))],
        out_specs=pl.BlockSpec((seq_len, LANE), lambda s: (s, 0)),
        out_shape=jax.ShapeDtypeStruct((n_seq * seq_len, LANE), F32),
        compiler_params=_cparams(1),
        name=f"ranks_{seq_len}",
    )(aff, aff_t, ones)


def _expert_kernel(h_ref, rank_ref, w1_ref, w3_ref, w2_ref, o_ref, xs_scr):
    def gather(seq_len, cap, tok0, slot0):
        slot = lax.broadcasted_iota(jnp.int32, (cap, seq_len), 0).astype(F32)
        rk = rank_ref[:, tok0:tok0 + seq_len]
        onehot = jnp.where(rk == slot, 1.0, 0.0).astype(BF16)
        xs_scr[slot0:slot0 + cap, :] = _dot(onehot, h_ref[tok0:tok0 + seq_len, :]).astype(BF16)

    for s in range(N_CTX_SEQ):
        gather(L_CTX, CAP_CTX, s * L_CTX, s * CAP_CTX)
    for s in range(N_LAT_SEQ):
        gather(L_LAT, CAP_LAT, T_CTX + s * L_LAT, N_CTX_SEQ * CAP_CTX + s * CAP_LAT)
    xs = xs_scr[...]
    a = _dot(xs, w1_ref[...].astype(BF16))
    act = (a * _sigmoid(a)) * _dot(xs, w3_ref[...].astype(BF16))
    o_ref[...] = _dot(act.astype(BF16), w2_ref[...].astype(BF16)).astype(BF16)


def _experts(layer, h2, rank_t, w1, w3, w2):
    wspec = lambda: pl.BlockSpec((None, None, D, FF), lambda e: (layer, e, 0, 0))
    return pl.pallas_call(
        _expert_kernel,
        grid=(N_EXPERTS,),
        in_specs=[pl.BlockSpec((T_ALL, D), lambda e: (0, 0), pipeline_mode=pl.Buffered(1)),
                  pl.BlockSpec((None, 1, T_ALL), lambda e: (e, 0, 0)),
                  wspec(), wspec(), pl.BlockSpec((None, None, FF, D), lambda e: (layer, e, 0, 0))],
        out_specs=pl.BlockSpec((None, SLOTS, D), lambda e: (e, 0, 0)),
        out_shape=jax.ShapeDtypeStruct((N_EXPERTS, SLOTS, D), BF16),
        scratch_shapes=[pltpu.VMEM((SLOTS, D), BF16)],
        compiler_params=_cparams(1),
        name="experts",
    )(h2, rank_t, w1, w3, w2)


def _combine_kernel(cap, final, x_ref, mod_ref, aff_ref, rank_ref, oe_ref, expand_ref, fg_ref, o_ref):
    n = N_EXPERTS * cap
    expand = expand_ref[...]
    rank_x = _sel_dot_r(rank_ref[...], expand, parts=2)
    aff_x = _dot(aff_ref[...].astype(BF16), expand)
    slot = jnp.bitwise_and(lax.broadcasted_iota(jnp.int32, (TM, n), 1), cap - 1).astype(F32)
    gate = jnp.where(rank_x == slot, aff_x, 0.0)
    acc = _dot(gate.astype(BF16), oe_ref[...].reshape(n, D))
    x = x_ref[...] + mod_ref[0, 5:6, :] * acc
    if final:
        ms = jnp.mean(x * x, axis=-1, keepdims=True)
        x = x * lax.rsqrt(ms + NORM_EPS) * fg_ref[...]
    o_ref[...] = x


def _combine(seq_len, n_seq, blk0, cap, slot_blk0, final, x, mod_l, aff, rank, oe, final_g):
    nb = seq_len // TM
    assert cap & (cap - 1) == 0
    tok = lambda c: pl.BlockSpec((TM, c), lambda s, j: (blk0 + s * nb + j, 0))
    expand = jnp.asarray(np.arange(LANE)[:, None] == np.arange(N_EXPERTS * cap)[None, :] // cap, BF16)
    if final:
        out_spec = pl.BlockSpec((TM, D), lambda s, j: (s * nb + j, 0))
        out_shape, aliases = jax.ShapeDtypeStruct((n_seq * seq_len, D), F32), {}
    else:
        out_spec, out_shape, aliases = tok(D), jax.ShapeDtypeStruct((T_ALL, D), F32), {0: 0}
    return pl.pallas_call(
        functools.partial(_combine_kernel, cap, final),
        grid=(n_seq, nb),
        in_specs=[tok(D),
                  pl.BlockSpec((1, N_MOD, D), lambda s, j: (_mod_row(blk0 + s * nb + j), 0, 0)),
                  tok(LANE),
                  pl.BlockSpec((TM, LANE), lambda s, j: (s * nb + j, 0)),
                  pl.BlockSpec((N_EXPERTS, cap, D), lambda s, j: (0, slot_blk0 + s, 0)),
                  pl.BlockSpec((LANE, N_EXPERTS * cap), lambda s, j: (0, 0)),
                  pl.BlockSpec((1, D), lambda s, j: (0, 0))],
        out_specs=out_spec,
        out_shape=out_shape,
        input_output_aliases=aliases,
        compiler_params=_cparams(2),
        name=f"combine_{seq_len}",
    )(x, mod_l, aff, rank, oe, expand, final_g)


def _row_structure(row_len):
    t = np.arange(TM)
    same_row = (t[:, None] // row_len) == (t[None, :] // row_len)
    delta = t[None, :] - t[:, None]
    band = np.stack([same_row & (delta >= -(w // 2)) & (delta < w // 2) for w in POOL_WINDOWS]).astype(np.float32)
    cnt = band.sum(-1)
    invcnt = np.repeat((1.0 / cnt).T, HEAD, axis=1).astype(np.float32)
    shift = np.stack([same_row & (delta == -1), same_row & (delta == 1)]).astype(np.float32)
    return band, invcnt, shift


def _constants():
    lane = np.arange(W)
    ones4 = (lane[:, None] // HEAD == lane[None, :] // HEAD).astype(np.float32)
    eye4 = (np.arange(HEAD)[:, None] == lane[None, :] % HEAD).astype(np.float32)
    structs = [_row_structure(L_CTX), _row_structure(GRID_W)]
    return dict(
        ones4=jnp.asarray(ones4, BF16), eye4=jnp.asarray(eye4, F32),
        band=jnp.asarray(np.stack([s[0] for s in structs]), BF16),
        invcnt=jnp.asarray(np.stack([s[1] for s in structs]), F32),
        shift=jnp.asarray(np.stack([s[2] for s in structs]), BF16))


def _block_diag(blocks):
    n = len(blocks)
    rows = []
    for i, b in enumerate(blocks):
        rows.append(jnp.concatenate([b if j == i else jnp.zeros((b.shape[0], blocks[j].shape[1]), b.dtype)
                                     for j in range(n)], axis=1))
    return jnp.concatenate(rows, axis=0)


def _wide_tiles(s):
    return jnp.moveaxis(s, -3, -2).reshape(s.shape[:-3] + (HEAD, W))


def _head_tiles(s):
    return jnp.moveaxis(s.reshape(s.shape[:-2] + (HEAD, N_HEADS, HEAD)), -2, -3)


def kernel(x_prompt, x_sample, state_rwkv, c, c_ctx, norm1_g, w_mod, b_mod, w_in, w0, w_up, a0, a_up, g_up, k_k, k_a, r_k, lnx_g, lnx_b, w_pool, pool_scale, conv_w, conv_b, sgu_g, w_s, b_s, w_branch, w_out, norm2_g, w_router, w_e1, w_e3, w_e2, final_g):
    const = _constants()
    x = (x_prompt.reshape(T_CTX, D), x_sample.reshape(T_LAT, D))
    cond8 = jnp.concatenate([c_ctx[None], c, jnp.zeros((SUBLANE - 1 - N_LAT_SEQ, D), F32)], axis=0)
    mod = _modulation(cond8, w_mod, b_mod).reshape(DEPTH, SUBLANE, N_MOD, D)
    final_g2 = final_g.reshape(1, D)
    ctx_states = []
    y_ctx = y_lat = None
    for l in range(DEPTH):
        mod_l = mod[l]
        g1 = norm1_g[l].reshape(1, D)
        wup_bd = _block_diag([w_up[l, 0], w_up[l, 1]])
        aup_bd = _block_diag([a_up[l, 0], a_up[l, 1]])
        rkv, bcd, wdec, bvec, kin, kk, gate_a, bonus = _in_projection(
            l, x, mod_l, g1, w_in, wup_bd, aup_bd, g_up[l], w0[l].reshape(1, 2 * W),
            a0[l].reshape(1, 2 * W), k_k[l].reshape(1, W), k_a[l].reshape(1, W), r_k[l].reshape(1, W),
            const['ones4'])

        s0_ctx = jnp.zeros((N_SCAN_BLK, CTX_SEQ_PER_SCAN_BLK, 2, HEAD, W), F32)
        y_c, s_ctx, y_l = _rwkv_scan(rkv, kk, wdec, bvec, kin, s0_ctx, _wide_tiles(state_rwkv[:, l]),
                                     const['ones4'], const['eye4'])
        ctx_states.append(_head_tiles(s_ctx).reshape(N_CTX_SEQ, 2, N_HEADS, HEAD, HEAD))

        bs_full = jnp.repeat(b_s[l].T, HEAD, axis=1)
        bp = dict(lnx_g=lnx_g[l].reshape(1, W), lnx_b=lnx_b[l].reshape(1, W), band=const['band'],
                  invcnt=const['invcnt'], shift=const['shift'],
                  wpool_bd=_block_diag([w_pool[l, i] for i in range(len(POOL_WINDOWS))]),
                  pool_scale=pool_scale[l].reshape(1, W), conv_w=conv_w[l], conv_b=conv_b[l].reshape(1, W),
                  sgu_g=sgu_g[l].reshape(1, W), w_s=w_s[l], bs_full=bs_full, ones4=const['ones4'])
        br = _branches(y_c, y_l, gate_a, bonus, bcd, bp)
        wr_pad = jnp.concatenate([w_router[l], jnp.zeros((D, LANE - N_EXPERTS), F32)], axis=1)
        x, h2, aff = _merge(x, mod_l, g1, br, w_in[l, :, N_SMALL:].astype(BF16), w_branch[l].astype(BF16),
                            w_out[l].astype(BF16), norm2_g[l].reshape(1, D), wr_pad)
        aff_t_ctx = jnp.swapaxes(aff[:T_CTX, :N_EXPERTS].reshape(N_CTX_SEQ, L_CTX, N_EXPERTS), 1, 2)
        aff_t_lat = jnp.swapaxes(aff[T_CTX:, :N_EXPERTS].reshape(N_LAT_SEQ, L_LAT, N_EXPERTS), 1, 2)
        rank_ctx = _ranks(L_CTX, N_CTX_SEQ, 0, aff, aff_t_ctx)
        rank_lat = _ranks(L_LAT, N_LAT_SEQ, N_CTX_BLK, aff, aff_t_lat)
        rank_t = jnp.concatenate([rank_ctx[:, :N_EXPERTS], rank_lat[:, :N_EXPERTS]], axis=0).T
        oe = _experts(l, h2, rank_t.reshape(N_EXPERTS, 1, T_ALL), w_e1, w_e3, w_e2)
        final = l == DEPTH - 1
        y_ctx = _combine(L_CTX, N_CTX_SEQ, 0, CAP_CTX, 0, final, x, mod_l, aff, rank_ctx, oe, final_g2)
        y_lat = _combine(L_LAT, N_LAT_SEQ, N_CTX_BLK, CAP_LAT, N_CTX_SEQ * CAP_CTX // CAP_LAT, final,
                         x if final else y_ctx, mod_l, aff, rank_lat, oe, final_g2)
        if not final:
            x = y_lat
    y_prompt = y_ctx.reshape(N_CTX_SEQ, L_CTX, D)
    y_sample = y_lat.reshape(N_LAT_SEQ, L_LAT, D)
    new_state = jnp.stack(ctx_states, axis=1)
    return (y_prompt, y_sample, new_state)
```

```python
import functools
import math

import numpy as np
import jax
import jax.numpy as jnp
from jax import lax
from jax.experimental import pallas as pl
from jax.experimental.pallas import tpu as pltpu

F32 = jnp.float32
BF16 = jnp.bfloat16

D = 1024
N_CTX_SEQ, L_CTX = 16, 256
N_LAT_SEQ, L_LAT = 2, 1024
T_CTX = N_CTX_SEQ * L_CTX
T_LAT = N_LAT_SEQ * L_LAT
T_ALL = T_CTX + T_LAT
DEPTH = 2
TM = 256
TM_WIDE = 512
N_BLK = T_ALL // TM
N_CTX_BLK = T_CTX // TM
LAT_BLK_PER_SEQ = L_LAT // TM
GRID_W = 64
W = 256
HEAD = 64
N_HEADS = W // HEAD
LORA = 64
LORA_G = 128
DECAY_SCALE = math.exp(-0.5)
POOL_WINDOWS = (2, 4, 8, 16)
CHUNK = 128
N_EXPERTS = 16
FF = 1024
CAP_CTX = 2 * L_CTX // N_EXPERTS
CAP_LAT = 2 * L_LAT // N_EXPERTS
SLOTS = N_CTX_SEQ * CAP_CTX + N_LAT_SEQ * CAP_LAT
N_MOD = 6
NORM_EPS = 1e-6
GN_EPS = 64e-5
N_SMALL = 3 * W + 2 * LORA + 2 * LORA + LORA_G + 6 * W
LANE = 128
SUBLANE = 8
VMEM_LIMIT = 56 * 1024 * 1024


def _cparams(n_axes):
    return pltpu.CompilerParams(dimension_semantics=("arbitrary",) * n_axes,
                                vmem_limit_bytes=VMEM_LIMIT)


def _split2(x):
    hi = x.astype(BF16)
    lo = (x - hi.astype(F32)).astype(BF16)
    return hi, lo


def _split3(x):
    hi = x.astype(BF16)
    r = x - hi.astype(F32)
    mid = r.astype(BF16)
    lo = (r - mid.astype(F32)).astype(BF16)
    return hi, mid, lo


def _dot(a, b):
    return jnp.dot(a, b, preferred_element_type=F32)


def _sel_dot_l(m_bf16, x, parts=2):
    ps = _split2(x) if parts == 2 else _split3(x)
    acc = _dot(m_bf16, ps[0])
    for p in ps[1:]:
        acc = acc + _dot(m_bf16, p)
    return acc


def _sel_dot_r(x, m_bf16, parts=2):
    ps = _split2(x) if parts == 2 else _split3(x)
    acc = _dot(ps[0], m_bf16)
    for p in ps[1:]:
        acc = acc + _dot(p, m_bf16)
    return acc


def _dot_hl(a, b):
    ah, al = _split2(a)
    bh, bl = _split2(b)
    return _dot(ah, bh) + _dot(al, bh) + _dot(ah, bl)


def _sigmoid(x):
    return 1.0 / (1.0 + jnp.exp(-x))


def _gelu_tanh(x):
    return 0.5 * x * (1.0 + jnp.tanh(math.sqrt(2.0 / math.pi) * (x + 0.044715 * (x * x * x))))


def _norm_mod(x, g, scale, shift):
    ms = jnp.mean(x * x, axis=-1, keepdims=True)
    return (x * lax.rsqrt(ms + NORM_EPS) * g) * (1.0 + scale) + shift


def _mod_row(i, tm=TM):
    return jnp.where(i < T_CTX // tm, 0, (i - T_CTX // tm) // (L_LAT // tm) + 1)


def _mod_kernel(c_ref, w_ref, b_ref, o_ref):
    c = c_ref[...]
    s = c * _sigmoid(c)
    o_ref[...] = _dot(s.astype(BF16), w_ref[...].astype(BF16)) + b_ref[...]


def _modulation(cond8, w_mod, b_mod):
    tn = 1536
    n = N_MOD * D
    return pl.pallas_call(
        _mod_kernel,
        grid=(DEPTH, n // tn),
        in_specs=[pl.BlockSpec((SUBLANE, D), lambda l, j: (0, 0)),
                  pl.BlockSpec((None, D, tn), lambda l, j: (l, 0, j)),
                  pl.BlockSpec((None, 1, tn), lambda l, j: (l, 0, j))],
        out_specs=pl.BlockSpec((None, SUBLANE, tn), lambda l, j: (l, 0, j)),
        out_shape=jax.ShapeDtypeStruct((DEPTH, SUBLANE, n), F32),
        compiler_params=_cparams(2),
        name="modulation",
    )(cond8, w_mod, b_mod.reshape(DEPTH, 1, n))


def _x_specs(x, tm):
    if not isinstance(x, tuple):
        return [pl.BlockSpec((tm, D), lambda i: (i, 0))], (x,)
    n = T_CTX // tm
    return [pl.BlockSpec((tm, D), lambda i: (jnp.minimum(i, n - 1), 0)),
            pl.BlockSpec((tm, D), lambda i: (jnp.maximum(i - n, 0), 0))], x


def _load_x(x_refs):
    if len(x_refs) == 1:
        return x_refs[0][...]
    n = T_CTX // x_refs[0].shape[0]
    return jnp.where(pl.program_id(0) < n, x_refs[0][...], x_refs[1][...])


def _inproj_kernel(n_x, *refs):
    (mod_ref, g_ref, w_ref, wup_ref, aup_ref, gup_ref, w0_ref, a0_ref, kk_k_ref, k_a_ref, r_k_ref, ones4_ref,
     rkv_ref, bcd_ref, wdec_ref, b_ref, kin_ref, kk_ref, gate_ref, bonus_ref, w16_scr) = refs[n_x:]

    @pl.when(pl.program_id(0) == 0)
    def _():
        w16_scr[...] = w_ref[...].astype(BF16)

    h = _norm_mod(_load_x(refs[:n_x]), g_ref[...], mod_ref[0, 1:2, :], mod_ref[0, 0:1, :])
    z = _dot(h.astype(BF16), w16_scr[...])
    rkv_ref[...] = z[:, :3 * W]
    bcd_ref[...] = z[:, 3 * W + 3 * LANE:]
    r = z[:, 0:W]
    k = z[:, W:2 * W]
    v = z[:, 2 * W:3 * W]
    xw = z[:, 3 * W:3 * W + LANE]
    xa = z[:, 3 * W + LANE:3 * W + 2 * LANE]
    xg = z[:, 3 * W + 2 * LANE:3 * W + 3 * LANE]
    ones4 = ones4_ref[...]
    dec = w0_ref[...] + _dot_hl(jnp.tanh(xw), wup_ref[...])
    wdec_ref[...] = jnp.exp(-DECAY_SCALE * _sigmoid(dec))
    a = _sigmoid(a0_ref[...] + _dot_hl(xa, aup_ref[...]))
    gate_ref[...] = _dot_hl(_sigmoid(xg), gup_ref[...])
    kk = k * kk_k_ref[...]
    n2 = _sel_dot_r(kk * kk, ones4, parts=3)
    kk = kk / jnp.maximum(jnp.sqrt(n2), 1e-12)
    kk_ref[...] = kk
    k_a = k_a_ref[...]
    kin0 = k * (1.0 + (a[:, 0:W] - 1.0) * k_a)
    kin1 = k * (1.0 + (a[:, W:2 * W] - 1.0) * k_a)
    kin_ref[:, 0:W] = kin0
    kin_ref[:, W:2 * W] = kin1
    b_ref[:, 0:W] = a[:, 0:W] * kk
    b_ref[:, W:2 * W] = a[:, W:2 * W] * kk
    bonus = _sel_dot_r(r * (kin0 + kin1) * r_k_ref[...], ones4, parts=3)
    bonus_ref[...] = bonus * v


def _in_projection(layer, x, mod_l, g1, w_in, wup_bd, aup_bd, g_up, w0, a0, k_k, k_a, r_k, ones4):
    tm = TM_WIDE
    tok = lambda c: pl.BlockSpec((tm, c), lambda i: (i, 0))
    full = lambda a: pl.BlockSpec(a.shape, lambda i: (0,) * a.ndim)
    params = (wup_bd, aup_bd, g_up, w0, a0, k_k, k_a, r_k, ones4)
    widths = (3 * W, 6 * W, 2 * W, 2 * W, 2 * W, W, W, W)
    x_specs, x = _x_specs(x, tm)
    return pl.pallas_call(
        functools.partial(_inproj_kernel, len(x)),
        grid=(T_ALL // tm,),
        in_specs=x_specs + [pl.BlockSpec((1, N_MOD, D), lambda i: (_mod_row(i, tm), 0, 0)), full(g1),
                            pl.BlockSpec((None, D, N_SMALL), lambda i: (layer, 0, 0), pipeline_mode=pl.Buffered(1))]
                 + [full(a) for a in params],
        out_specs=[tok(c) for c in widths],
        out_shape=[jax.ShapeDtypeStruct((T_ALL, c), F32) for c in widths],
        scratch_shapes=[pltpu.VMEM((D, N_SMALL), BF16)],
        compiler_params=_cparams(1),
        name="in_projection",
    )(*x, mod_l, g1, w_in, *params)


T_STEP = 4 * SUBLANE
CHAINS_PER_DOT = 4
CTX_SEQ_PER_SCAN_BLK = 4
N_SCAN_BLK = N_CTX_SEQ // CTX_SEQ_PER_SCAN_BLK
LAT_WIN = L_LAT // N_SCAN_BLK
SCAN_FIELDS = ('r', 'v', 'kk', 'w', 'b', 'k')


def _scan_kernel(*refs):
    n_f = len(SCAN_FIELDS)
    ctx_refs = dict(zip(SCAN_FIELDS, refs[:n_f]))
    lat_refs = {}
    pos = n_f
    for s in range(N_LAT_SEQ):
        for d in range(2):
            lat_refs[(s, d)] = dict(zip(SCAN_FIELDS, refs[pos:pos + n_f]))
            pos += n_f
    s0c_ref, s0l_ref, ones4_ref, eye4_ref, yc_ref, sfin_ref, yl_ref, slat_scr = refs[pos:]
    step_i = pl.program_id(0)
    ones4 = ones4_ref[...]
    eye4 = eye4_ref[...]
    chains = ([(False, b, d) for b in range(CTX_SEQ_PER_SCAN_BLK) for d in range(2)]
              + [(True, s, d) for s in range(N_LAT_SEQ) for d in range(2)])
    n_chain = len(chains)
    groups = [chains[g:g + CHAINS_PER_DOT] for g in range(0, n_chain, CHAINS_PER_DOT)]

    @pl.when(step_i == 0)
    def _():
        yl_ref[...] = jnp.zeros_like(yl_ref)
        for s in range(N_LAT_SEQ):
            for d in range(2):
                slat_scr[s, d] = s0l_ref[s, d]

    yc_ref[...] = jnp.zeros_like(yc_ref)

    eye16 = eye4.astype(BF16)
    lane_in_head = jnp.bitwise_and(lax.broadcasted_iota(jnp.int32, (SUBLANE, W), 1), HEAD - 1)
    row_group = lane_in_head // SUBLANE
    on_diag = jnp.bitwise_and(lane_in_head, SUBLANE - 1) == lax.broadcasted_iota(jnp.int32, (SUBLANE, W), 0)

    def diagonal(x):
        m = x[0:SUBLANE]
        for g in range(1, HEAD // SUBLANE):
            m = jnp.where(row_group == g, x[g * SUBLANE:(g + 1) * SUBLANE], m)
        return jnp.sum(jnp.where(on_diag, m, 0.0), axis=0, keepdims=True)

    def row_sum(blocks):
        return _dot(jnp.concatenate(blocks, axis=0), ones4)

    def body(tb, state):
        state = list(state)
        tiles = []
        for lat, b, d in chains:
            t0 = tb * T_STEP if d == 0 else L_CTX - T_STEP - tb * T_STEP
            if lat:
                rows = pl.ds(pl.multiple_of(t0, T_STEP), T_STEP)
                tiles.append({f: lat_refs[(b, d)][f][rows, :] for f in SCAN_FIELDS})
            else:
                rows = pl.ds(pl.multiple_of(b * L_CTX + t0, T_STEP), T_STEP)
                col = lambda f: slice(d * W, (d + 1) * W) if f in ('w', 'b', 'k') else slice(0, W)
                tiles.append({f: ctx_refs[f][rows, col(f)] for f in SCAN_FIELDS})

        def row(name, c, j):
            jj = j if chains[c][2] == 0 else T_STEP - 1 - j
            return tiles[c][name][jj:jj + 1, :]

        def row16(name, c, j):
            return row(name, c, j).astype(BF16)

        state16 = [s.astype(BF16) for s in state]
        y_rows = [[None] * T_STEP for _ in chains]
        def outputs(j, q):
            y_cols = row_sum(q)
            for c, (lat, b, d) in enumerate(chains):
                jj = j if d == 0 else T_STEP - 1 - j
                y_rows[c][jj] = diagonal(y_cols[c * HEAD:(c + 1) * HEAD])

        def feedback(grp_cs, j):
            return row_sum([state16[c] * row16('kk', c, j) for c in grp_cs]
                           + [eye16 * row16('v', c, j) for c in grp_cs])

        group_cs = [[chains.index(ch) for ch in grp] for grp in groups]
        sums = [feedback(cs, 0) for cs in group_cs]
        for j in range(T_STEP):
            q = [None] * n_chain
            for g, cs in enumerate(group_cs):
                n = len(cs)
                for i, c in enumerate(cs):
                    s = (state[c] * row('w', c, j) - sums[g][i * HEAD:(i + 1) * HEAD] * row('b', c, j)
                         + sums[g][(n + i) * HEAD:(n + i + 1) * HEAD] * row('k', c, j))
                    state[c] = s
                    state16[c] = s.astype(BF16)
                    q[c] = state16[c] * row16('r', c, j)
                if j + 1 < T_STEP:
                    sums[g] = feedback(cs, j + 1)
            outputs(j, q)
        for c, (lat, b, d) in enumerate(chains):
            t0 = tb * T_STEP if d == 0 else L_CTX - T_STEP - tb * T_STEP
            y_tile = jnp.concatenate(y_rows[c], axis=0)
            if lat:
                win0 = step_i * LAT_WIN if d == 0 else L_LAT - LAT_WIN - step_i * LAT_WIN
                rows = pl.ds(pl.multiple_of(b * L_LAT + win0 + t0, T_STEP), T_STEP)
                yl_ref[rows, :] += y_tile
            else:
                rows = pl.ds(pl.multiple_of(b * L_CTX + t0, T_STEP), T_STEP)
                yc_ref[rows, :] += y_tile
        return tuple(state)

    init = tuple(slat_scr[b, d] if lat else s0c_ref[b, d] for (lat, b, d) in chains)
    fin = lax.fori_loop(0, L_CTX // T_STEP, body, init)
    for c, (lat, b, d) in enumerate(chains):
        if lat:
            slat_scr[b, d] = fin[c]
        else:
            sfin_ref[b, d] = fin[c]


def _rwkv_scan(rkv, kk, wdec, bvec, kin, s0_ctx, s0_lat, ones4, eye4):
    assert LAT_WIN == L_CTX
    arrays = dict(r=(rkv, 0), v=(rkv, 2), kk=(kk, 0), w=(wdec, None), b=(bvec, None), k=(kin, None))
    rows_c = CTX_SEQ_PER_SCAN_BLK * L_CTX
    operands, in_specs = [], []
    for f in SCAN_FIELDS:
        a, col = arrays[f]
        operands.append(a)
        in_specs.append(pl.BlockSpec((rows_c, W if col is not None else 2 * W),
                                     lambda i, col=col: (i, col if col is not None else 0)))
    win_blk0 = T_CTX // LAT_WIN
    for s in range(N_LAT_SEQ):
        for d in range(2):
            for f in SCAN_FIELDS:
                a, col = arrays[f]
                operands.append(a)
                in_specs.append(pl.BlockSpec(
                    (LAT_WIN, W),
                    lambda i, s=s, d=d, col=col: (win_blk0 + s * N_SCAN_BLK + (i if d == 0 else N_SCAN_BLK - 1 - i),
                                                  col if col is not None else d)))
    st = pl.BlockSpec((None, CTX_SEQ_PER_SCAN_BLK, 2, HEAD, W), lambda i: (i, 0, 0, 0, 0))
    full = lambda a: pl.BlockSpec(a.shape, lambda i: (0,) * a.ndim)
    operands += [s0_ctx, s0_lat, ones4, eye4]
    in_specs += [st, full(s0_lat), full(ones4), full(eye4)]
    return pl.pallas_call(
        _scan_kernel,
        grid=(N_SCAN_BLK,),
        in_specs=in_specs,
        out_specs=[pl.BlockSpec((rows_c, W), lambda i: (i, 0)), st, pl.BlockSpec((T_LAT, W), lambda i: (0, 0))],
        out_shape=[jax.ShapeDtypeStruct((T_CTX, W), F32),
                   jax.ShapeDtypeStruct((N_SCAN_BLK, CTX_SEQ_PER_SCAN_BLK, 2, HEAD, W), F32),
                   jax.ShapeDtypeStruct((T_LAT, W), F32)],
        scratch_shapes=[pltpu.VMEM((N_LAT_SEQ, 2, HEAD, W), F32)],
        compiler_params=_cparams(1),
        name="rwkv_scan",
    )(*operands)


def _branch_kernel(yc_ref, yl_ref, g_ref, bonus_ref, bcd_ref, lnx_g_ref, lnx_b_ref, band_ref, invcnt_ref, shift_ref,
                   wpool_ref, pscale_ref, conv_w_ref, conv_b_ref, sgu_g_ref, ws_ref, bs_ref, ones4_ref, br_ref):
    ones4 = ones4_ref[...]
    group = lax.broadcasted_iota(jnp.int32, (TM, W), 1) // HEAD

    def seg_mean(x):
        return _sel_dot_r(x, ones4, parts=2) * (1.0 / HEAD)

    y = jnp.where(pl.program_id(0) < N_CTX_BLK, yc_ref[...], yl_ref[...])
    mu = seg_mean(y)
    yc = y - mu
    var = seg_mean(yc * yc)
    ya = yc * lax.rsqrt(var + GN_EPS) * lnx_g_ref[...] + lnx_b_ref[...] + bonus_ref[...]
    br_ref[:, 0:W] = (ya * g_ref[...]).astype(BF16)

    u = bcd_ref[:, 0:W]
    u_parts = _split2(u)
    win = jnp.zeros((TM, W), F32)
    for gi in range(len(POOL_WINDOWS)):
        band = band_ref[0, gi]
        s = _dot(band, u_parts[0]) + _dot(band, u_parts[1])
        win = jnp.where(group == gi, s, win)
    pooled = win * invcnt_ref[0] - u
    br_ref[:, W:2 * W] = (_dot_hl(pooled, wpool_ref[...]) * pscale_ref[...]).astype(BF16)

    cin = bcd_ref[:, W:2 * W]
    cb = bcd_ref[:, 2 * W:3 * W]
    cc = bcd_ref[:, 3 * W:4 * W]
    t = cc * cin
    t_prev = _sel_dot_l(shift_ref[0, 0], t, parts=2)
    t_next = _sel_dot_l(shift_ref[0, 1], t, parts=2)
    conv = conv_b_ref[...] + t_prev * conv_w_ref[0:1, :]
    conv = conv + t * conv_w_ref[1:2, :]
    conv = conv + t_next * conv_w_ref[2:3, :]
    br_ref[:, 2 * W:3 * W] = (cb * conv).astype(BF16)

    du = _gelu_tanh(bcd_ref[:, 4 * W:5 * W])
    dv = _gelu_tanh(bcd_ref[:, 5 * W:6 * W])
    mu = seg_mean(dv)
    dc = dv - mu
    var = seg_mean(dc * dc)
    vn = dc * lax.rsqrt(var + NORM_EPS) * sgu_g_ref[...]
    group_c = lax.broadcasted_iota(jnp.int32, (CHUNK, W), 1) // HEAD
    for ch in range(TM // CHUNK):
        rows = slice(ch * CHUNK, (ch + 1) * CHUNK)
        vh, vl = _split2(vn[rows, :])
        s = jnp.zeros((CHUNK, W), F32)
        for gi in range(N_HEADS):
            wh, wl = _split2(ws_ref[gi])
            sg = _dot(wh, vh) + _dot(wl, vh) + _dot(wh, vl)
            s = jnp.where(group_c == gi, sg, s)
        br_ref[rows, 3 * W:4 * W] = (du[rows, :] * (s + bs_ref[...])).astype(BF16)


def _branches(y_ctx, y_lat, g, bonus, bcd, p):
    tok = lambda c: pl.BlockSpec((TM, c), lambda i: (i, 0))
    y_specs = [pl.BlockSpec((TM, W), lambda i: (jnp.minimum(i, N_CTX_BLK - 1), 0)),
               pl.BlockSpec((TM, W), lambda i: (jnp.maximum(i - N_CTX_BLK, 0), 0))]
    full = lambda a: pl.BlockSpec(a.shape, lambda i: (0,) * a.ndim)
    lay = lambda a: pl.BlockSpec((1,) + a.shape[1:], lambda i: (jnp.where(i < N_CTX_BLK, 0, 1),) + (0,) * (a.ndim - 1))
    return pl.pallas_call(
        _branch_kernel,
        grid=(N_BLK,),
        in_specs=y_specs + [tok(W), tok(W), tok(6 * W), full(p['lnx_g']), full(p['lnx_b']),
                  lay(p['band']), lay(p['invcnt']), lay(p['shift']),
                  full(p['wpool_bd']), full(p['pool_scale']), full(p['conv_w']), full(p['conv_b']),
                  full(p['sgu_g']), full(p['w_s']), full(p['bs_full']), full(p['ones4'])],
        out_specs=tok(4 * W),
        out_shape=jax.ShapeDtypeStruct((T_ALL, 4 * W), BF16),
        compiler_params=_cparams(1),
        name="branches",
    )(y_ctx, y_lat, g, bonus, bcd, p['lnx_g'], p['lnx_b'], p['band'], p['invcnt'], p['shift'], p['wpool_bd'],
      p['pool_scale'], p['conv_w'], p['conv_b'], p['sgu_g'], p['w_s'], p['bs_full'], p['ones4'])


def _merge_kernel(n_x, *refs):
    mod_ref, g1_ref, br_ref, wgl_ref, wbr_ref, wout_ref, g2_ref, wr_ref, o_ref, h_ref, aff_ref = refs[n_x:]
    x = _load_x(refs[:n_x])
    h = _norm_mod(x, g1_ref[...], mod_ref[0, 1:2, :], mod_ref[0, 0:1, :]).astype(BF16)
    merged = jnp.zeros(x.shape, F32)
    for i in range(4):
        gl = _dot(h, wgl_ref[:, i * D:(i + 1) * D])
        proj = _dot(br_ref[:, i * W:(i + 1) * W], wbr_ref[i])
        merged = merged + _sigmoid(gl) * proj
    mix = _dot(merged.astype(BF16), wout_ref[...])
    x = x + mod_ref[0, 2:3, :] * mix
    o_ref[...] = x

    h2 = _norm_mod(x, g2_ref[...], mod_ref[0, 4:5, :], mod_ref[0, 3:4, :])
    h_ref[...] = h2.astype(BF16)
    logits = _dot_hl(h2, wr_ref[...])
    lane = lax.broadcasted_iota(jnp.int32, logits.shape, 1)
    logits = jnp.where(lane < N_EXPERTS, logits, -1e30)
    m = jnp.max(logits, axis=-1, keepdims=True)
    e = jnp.where(lane < N_EXPERTS, jnp.exp(logits - m), 0.0)
    aff_ref[...] = e / jnp.sum(e, axis=-1, keepdims=True)


def _merge(x, mod_l, g1, br, w_gl, w_branch, w_out, g2, w_router_pad):
    full = lambda a: pl.BlockSpec(a.shape, lambda i: (0,) * a.ndim)
    tm = TM_WIDE
    tok = lambda c: pl.BlockSpec((tm, c), lambda i: (i, 0))
    x_specs, x = _x_specs(x, tm)
    return pl.pallas_call(
        functools.partial(_merge_kernel, len(x)),
        grid=(T_ALL // tm,),
        in_specs=x_specs + [pl.BlockSpec((1, N_MOD, D), lambda i: (_mod_row(i, tm), 0, 0)), full(g1), tok(4 * W),
                            full(w_gl), full(w_branch), full(w_out), full(g2), full(w_router_pad)],
        out_specs=[tok(D), tok(D), tok(LANE)],
        out_shape=[jax.ShapeDtypeStruct((T_ALL, D), F32), jax.ShapeDtypeStruct((T_ALL, D), BF16),
                   jax.ShapeDtypeStruct((T_ALL, LANE), F32)],
        compiler_params=_cparams(1),
        name="merge",
    )(*x, mod_l, g1, br, w_gl, w_branch, w_out, g2, w_router_pad)


def _rank_kernel(seq_len, aff_ref, afft_ref, ones_ref, rank_ref):
    nb = seq_len // TM
    earlier = jnp.where(lax.broadcasted_iota(jnp.int32, (TM, TM), 1) < lax.broadcasted_iota(jnp.int32, (TM, TM), 0),
                        1.0, 0.0)
    lane = lax.broadcasted_iota(jnp.int32, (TM, LANE), 1)
    for j in range(nb):
        aff = aff_ref[j * TM:(j + 1) * TM, :]
        rank = jnp.zeros((TM, LANE), F32)
        for e in range(N_EXPERTS):
            mine = aff[:, e:e+1]
            tiles = []
            for c in range(nb):
                other = afft_ref[e:e+1, c * TM:(c + 1) * TM]
                if c < j:
                    tiles.append(jnp.where(other >= mine, 1.0, 0.0))
                elif c > j:
                    tiles.append(jnp.where(other > mine, 1.0, 0.0))
                else:
                    tiles.append(jnp.where(other > mine, 1.0, jnp.where(other == mine, earlier, 0.0)))
            cnt = _dot(jnp.concatenate(tiles, axis=1).astype(BF16), ones_ref[...])
            rank = jnp.where(lane == e, cnt, rank)
        rank_ref[j * TM:(j + 1) * TM, :] = rank


def _ranks(seq_len, n_seq, blk0, aff, aff_t):
    nb = seq_len // TM
    ones = jnp.ones((seq_len, LANE), BF16)
    return pl.pallas_call(
        functools.partial(_rank_kernel, seq_len),
        grid=(n_seq,),
        in_specs=[pl.BlockSpec((seq_len, LANE), lambda s: (blk0 // nb + s, 0)),
                  pl.BlockSpec((None, N_EXPERTS, seq_len), lambda s: (s, 0, 0)),
                  pl.BlockSpec((seq_len, LANE), lambda s: (0, 0))],
        out_specs=pl.BlockSpec((seq_len, LANE), lambda s: (s, 0)),
        out_shape=jax.ShapeDtypeStruct((n_seq * seq_len, LANE), F32),
        compiler_params=_cparams(1),
        name=f"ranks_{seq_len}",
    )(aff, aff_t, ones)


def _expert_kernel(h_ref, rank_ref, w1_ref, w3_ref, w2_ref, o_ref, xs_scr):
    def gather(seq_len, cap, tok0, slot0):
        slot = lax.broadcasted_iota(jnp.int32, (cap, seq_len), 0).astype(F32)
        rk = rank_ref[:, tok0:tok0 + seq_len]
        onehot = jnp.where(rk == slot, 1.0, 0.0).astype(BF16)
        xs_scr[slot0:slot0 + cap, :] = _dot(onehot, h_ref[tok0:tok0 + seq_len, :]).astype(BF16)

    for s in range(N_CTX_SEQ):
        gather(L_CTX, CAP_CTX, s * L_CTX, s * CAP_CTX)
    for s in range(N_LAT_SEQ):
        gather(L_LAT, CAP_LAT, T_CTX + s * L_LAT, N_CTX_SEQ * CAP_CTX + s * CAP_LAT)
    xs = xs_scr[...]
    a = _dot(xs, w1_ref[...].astype(BF16))
    act = (a * _sigmoid(a)) * _dot(xs, w3_ref[...].astype(BF16))
    o_ref[...] = _dot(act.astype(BF16), w2_ref[...].astype(BF16)).astype(BF16)


def _experts(layer, h2, rank_t, w1, w3, w2):
    wspec = lambda: pl.BlockSpec((None, None, D, FF), lambda e: (layer, e, 0, 0))
    return pl.pallas_call(
        _expert_kernel,
        grid=(N_EXPERTS,),
        in_specs=[pl.BlockSpec((T_ALL, D), lambda e: (0, 0), pipeline_mode=pl.Buffered(1)),
                  pl.BlockSpec((None, 1, T_ALL), lambda e: (e, 0, 0)),
                  wspec(), wspec(), pl.BlockSpec((None, None, FF, D), lambda e: (layer, e, 0, 0))],
        out_specs=pl.BlockSpec((None, SLOTS, D), lambda e: (e, 0, 0)),
        out_shape=jax.ShapeDtypeStruct((N_EXPERTS, SLOTS, D), BF16),
        scratch_shapes=[pltpu.VMEM((SLOTS, D), BF16)],
        compiler_params=_cparams(1),
        name="experts",
    )(h2, rank_t, w1, w3, w2)


def _combine_kernel(cap, final, x_ref, mod_ref, aff_ref, rank_ref, oe_ref, expand_ref, fg_ref, o_ref):
    n = N_EXPERTS * cap
    expand = expand_ref[...]
    rank_x = _sel_dot_r(rank_ref[...], expand, parts=2)
    aff_x = _dot(aff_ref[...].astype(BF16), expand)
    slot = jnp.bitwise_and(lax.broadcasted_iota(jnp.int32, (TM, n), 1), cap - 1).astype(F32)
    gate = jnp.where(rank_x == slot, aff_x, 0.0)
    acc = _dot(gate.astype(BF16), oe_ref[...].reshape(n, D))
    x = x_ref[...] + mod_ref[0, 5:6, :] * acc
    if final:
        ms = jnp.mean(x * x, axis=-1, keepdims=True)
        x = x * lax.rsqrt(ms + NORM_EPS) * fg_ref[...]
    o_ref[...] = x


def _combine(seq_len, n_seq, blk0, cap, slot_blk0, final, x, mod_l, aff, rank, oe, final_g):
    nb = seq_len // TM
    assert cap & (cap - 1) == 0
    tok = lambda c: pl.BlockSpec((TM, c), lambda s, j: (blk0 + s * nb + j, 0))
    expand = jnp.asarray(np.arange(LANE)[:, None] == np.arange(N_EXPERTS * cap)[None, :] // cap, BF16)
    if final:
        out_spec = pl.BlockSpec((TM, D), lambda s, j: (s * nb + j, 0))
        out_shape, aliases = jax.ShapeDtypeStruct((n_seq * seq_len, D), F32), {}
    else:
        out_spec, out_shape, aliases = tok(D), jax.ShapeDtypeStruct((T_ALL, D), F32), {0: 0}
    return pl.pallas_call(
        functools.partial(_combine_kernel, cap, final),
        grid=(n_seq, nb),
        in_specs=[tok(D),
                  pl.BlockSpec((1, N_MOD, D), lambda s, j: (_mod_row(blk0 + s * nb + j), 0, 0)),
                  tok(LANE),
                  pl.BlockSpec((TM, LANE), lambda s, j: (s * nb + j, 0)),
                  pl.BlockSpec((N_EXPERTS, cap, D), lambda s, j: (0, slot_blk0 + s, 0)),
                  pl.BlockSpec((LANE, N_EXPERTS * cap), lambda s, j: (0, 0)),
                  pl.BlockSpec((1, D), lambda s, j: (0, 0))],
        out_specs=out_spec,
        out_shape=out_shape,
        input_output_aliases=aliases,
        compiler_params=_cparams(2),
        name=f"combine_{seq_len}",
    )(x, mod_l, aff, rank, oe, expand, final_g)


def _row_structure(row_len):
    t = np.arange(TM)
    same_row = (t[:, None] // row_len) == (t[None, :] // row_len)
    delta = t[None, :] - t[:, None]
    band = np.stack([same_row & (delta >= -(w // 2)) & (delta < w // 2) for w in POOL_WINDOWS]).astype(np.float32)
    cnt = band.sum(-1)
    invcnt = np.repeat((1.0 / cnt).T, HEAD, axis=1).astype(np.float32)
    shift = np.stack([same_row & (delta == -1), same_row & (delta == 1)]).astype(np.float32)
    return band, invcnt, shift


def _constants():
    lane = np.arange(W)
    ones4 = (lane[:, None] // HEAD == lane[None, :] // HEAD).astype(np.float32)
    eye4 = (np.arange(HEAD)[:, None] == lane[None, :] % HEAD).astype(np.float32)
    structs = [_row_structure(L_CTX), _row_structure(GRID_W)]
    return dict(
        ones4=jnp.asarray(ones4, BF16), eye4=jnp.asarray(eye4, F32),
        band=jnp.asarray(np.stack([s[0] for s in structs]), BF16),
        invcnt=jnp.asarray(np.stack([s[1] for s in structs]), F32),
        shift=jnp.asarray(np.stack([s[2] for s in structs]), BF16))


def _block_diag(blocks):
    n = len(blocks)
    rows = []
    for i, b in enumerate(blocks):
        rows.append(jnp.concatenate([b if j == i else jnp.zeros((b.shape[0], blocks[j].shape[1]), b.dtype)
                                     for j in range(n)], axis=1))
    return jnp.concatenate(rows, axis=0)


def _wide_tiles(s):
    return jnp.moveaxis(s, -3, -2).reshape(s.shape[:-3] + (HEAD, W))


def _head_tiles(s):
    return jnp.moveaxis(s.reshape(s.shape[:-2] + (HEAD, N_HEADS, HEAD)), -2, -3)


def kernel(x_prompt, x_sample, state_rwkv, c, c_ctx, norm1_g, w_mod, b_mod, w_in, w0, w_up, a0, a_up, g_up, k_k, k_a, r_k, lnx_g, lnx_b, w_pool, pool_scale, conv_w, conv_b, sgu_g, w_s, b_s, w_branch, w_out, norm2_g, w_router, w_e1, w_e3, w_e2, final_g):
    const = _constants()
    x = (x_prompt.reshape(T_CTX, D), x_sample.reshape(T_LAT, D))
    cond8 = jnp.concatenate([c_ctx[None], c, jnp.zeros((SUBLANE - 1 - N_LAT_SEQ, D), F32)], axis=0)
    mod = _modulation(cond8, w_mod, b_mod).reshape(DEPTH, SUBLANE, N_MOD, D)
    final_g2 = final_g.reshape(1, D)
    ctx_states = []
    y_ctx = y_lat = None
    for l in range(DEPTH):
        mod_l = mod[l]
        g1 = norm1_g[l].reshape(1, D)
        wup_bd = _block_diag([w_up[l, 0], w_up[l, 1]])
        aup_bd = _block_diag([a_up[l, 0], a_up[l, 1]])
        rkv, bcd, wdec, bvec, kin, kk, gate_a, bonus = _in_projection(
            l, x, mod_l, g1, w_in, wup_bd, aup_bd, g_up[l], w0[l].reshape(1, 2 * W),
            a0[l].reshape(1, 2 * W), k_k[l].reshape(1, W), k_a[l].reshape(1, W), r_k[l].reshape(1, W),
            const['ones4'])

        s0_ctx = jnp.zeros((N_SCAN_BLK, CTX_SEQ_PER_SCAN_BLK, 2, HEAD, W), F32)
        y_c, s_ctx, y_l = _rwkv_scan(rkv, kk, wdec, bvec, kin, s0_ctx, _wide_tiles(state_rwkv[:, l]),
                                     const['ones4'], const['eye4'])
        ctx_states.append(_head_tiles(s_ctx).reshape(N_CTX_SEQ, 2, N_HEADS, HEAD, HEAD))

        bs_full = jnp.repeat(b_s[l].T, HEAD, axis=1)
        bp = dict(lnx_g=lnx_g[l].reshape(1, W), lnx_b=lnx_b[l].reshape(1, W), band=const['band'],
                  invcnt=const['invcnt'], shift=const['shift'],
                  wpool_bd=_block_diag([w_pool[l, i] for i in range(len(POOL_WINDOWS))]),
                  pool_scale=pool_scale[l].reshape(1, W), conv_w=conv_w[l], conv_b=conv_b[l].reshape(1, W),
                  sgu_g=sgu_g[l].reshape(1, W), w_s=w_s[l], bs_full=bs_full, ones4=const['ones4'])
        br = _branches(y_c, y_l, gate_a, bonus, bcd, bp)
        wr_pad = jnp.concatenate([w_router[l], jnp.zeros((D, LANE - N_EXPERTS), F32)], axis=1)
        x, h2, aff = _merge(x, mod_l, g1, br, w_in[l, :, N_SMALL:].astype(BF16), w_branch[l].astype(BF16),
                            w_out[l].astype(BF16), norm2_g[l].reshape(1, D), wr_pad)
        aff_t_ctx = jnp.swapaxes(aff[:T_CTX, :N_EXPERTS].reshape(N_CTX_SEQ, L_CTX, N_EXPERTS), 1, 2)
        aff_t_lat = jnp.swapaxes(aff[T_CTX:, :N_EXPERTS].reshape(N_LAT_SEQ, L_LAT, N_EXPERTS), 1, 2)
        rank_ctx = _ranks(L_CTX, N_CTX_SEQ, 0, aff, aff_t_ctx)
        rank_lat = _ranks(L_LAT, N_LAT_SEQ, N_CTX_BLK, aff, aff_t_lat)
        rank_t = jnp.concatenate([rank_ctx[:, :N_EXPERTS], rank_lat[:, :N_EXPERTS]], axis=0).T
        oe = _experts(l, h2, rank_t.reshape(N_EXPERTS, 1, T_ALL), w_e1, w_e3, w_e2)
        final = l == DEPTH - 1
        y_ctx = _combine(L_CTX, N_CTX_SEQ, 0, CAP_CTX, 0, final, x, mod_l, aff, rank_ctx, oe, final_g2)
        y_lat = _combine(L_LAT, N_LAT_SEQ, N_CTX_BLK, CAP_LAT, N_CTX_SEQ * CAP_CTX // CAP_LAT, final,
                         x if final else y_ctx, mod_l, aff, rank_lat, oe, final_g2)
        if not final:
            x = y_lat
    y_prompt = y_ctx.reshape(N_CTX_SEQ, L_CTX, D)
    y_sample = y_lat.reshape(N_LAT_SEQ, L_LAT, D)
    new_state = jnp.stack(ctx_states, axis=1)
    return (y_prompt, y_sample, new_state)
```

```python
import functools
import math

import numpy as np
import jax
import jax.numpy as jnp
from jax import lax
from jax.experimental import pallas as pl
from jax.experimental.pallas import tpu as pltpu

F32 = jnp.float32
BF16 = jnp.bfloat16

D = 1024
N_CTX_SEQ, L_CTX = 16, 256
N_LAT_SEQ, L_LAT = 2, 1024
T_CTX = N_CTX_SEQ * L_CTX
T_LAT = N_LAT_SEQ * L_LAT
T_ALL = T_CTX + T_LAT
DEPTH = 2
TM = 256
TM_WIDE = 512
N_BLK = T_ALL // TM
N_CTX_BLK = T_CTX // TM
LAT_BLK_PER_SEQ = L_LAT // TM
GRID_W = 64
W = 256
HEAD = 64
N_HEADS = W // HEAD
LORA = 64
LORA_G = 128
DECAY_SCALE = math.exp(-0.5)
POOL_WINDOWS = (2, 4, 8, 16)
CHUNK = 128
N_EXPERTS = 16
FF = 1024
CAP_CTX = 2 * L_CTX // N_EXPERTS
CAP_LAT = 2 * L_LAT // N_EXPERTS
SLOTS = N_CTX_SEQ * CAP_CTX + N_LAT_SEQ * CAP_LAT
N_MOD = 6
NORM_EPS = 1e-6
GN_EPS = 64e-5
N_SMALL = 3 * W + 2 * LORA + 2 * LORA + LORA_G + 6 * W
LANE = 128
SUBLANE = 8
VMEM_LIMIT = 56 * 1024 * 1024


def _cparams(n_axes):
    return pltpu.CompilerParams(dimension_semantics=("arbitrary",) * n_axes,
                                vmem_limit_bytes=VMEM_LIMIT)


def _split2(x):
    hi = x.astype(BF16)
    lo = (x - hi.astype(F32)).astype(BF16)
    return hi, lo


def _split3(x):
    hi = x.astype(BF16)
    r = x - hi.astype(F32)
    mid = r.astype(BF16)
    lo = (r - mid.astype(F32)).astype(BF16)
    return hi, mid, lo


def _dot(a, b):
    return jnp.dot(a, b, preferred_element_type=F32)


def _sel_dot_l(m_bf16, x, parts=2):
    ps = _split2(x) if parts == 2 else _split3(x)
    acc = _dot(m_bf16, ps[0])
    for p in ps[1:]:
        acc = acc + _dot(m_bf16, p)
    return acc


def _sel_dot_r(x, m_bf16, parts=2):
    ps = _split2(x) if parts == 2 else _split3(x)
    acc = _dot(ps[0], m_bf16)
    for p in ps[1:]:
        acc = acc + _dot(p, m_bf16)
    return acc


def _dot_hl(a, b):
    ah, al = _split2(a)
    bh, bl = _split2(b)
    return _dot(ah, bh) + _dot(al, bh) + _dot(ah, bl)


def _sigmoid(x):
    return 1.0 / (1.0 + jnp.exp(-x))


def _gelu_tanh(x):
    return 0.5 * x * (1.0 + jnp.tanh(math.sqrt(2.0 / math.pi) * (x + 0.044715 * (x * x * x))))


def _norm_mod(x, g, scale, shift):
    ms = jnp.mean(x * x, axis=-1, keepdims=True)
    return (x * lax.rsqrt(ms + NORM_EPS) * g) * (1.0 + scale) + shift


def _mod_row(i, tm=TM):
    return jnp.where(i < T_CTX // tm, 0, (i - T_CTX // tm) // (L_LAT // tm) + 1)


def _mod_kernel(c_ref, w_ref, b_ref, o_ref):
    c = c_ref[...]
    s = c * _sigmoid(c)
    o_ref[...] = _dot(s.astype(BF16), w_ref[...].astype(BF16)) + b_ref[...]


def _modulation(cond8, w_mod, b_mod):
    tn = 1536
    n = N_MOD * D
    return pl.pallas_call(
        _mod_kernel,
        grid=(DEPTH, n // tn),
        in_specs=[pl.BlockSpec((SUBLANE, D), lambda l, j: (0, 0)),
                  pl.BlockSpec((None, D, tn), lambda l, j: (l, 0, j)),
                  pl.BlockSpec((None, 1, tn), lambda l, j: (l, 0, j))],
        out_specs=pl.BlockSpec((None, SUBLANE, tn), lambda l, j: (l, 0, j)),
        out_shape=jax.ShapeDtypeStruct((DEPTH, SUBLANE, n), F32),
        compiler_params=_cparams(2),
        name="modulation",
    )(cond8, w_mod, b_mod.reshape(DEPTH, 1, n))


def _x_specs(x, tm):
    if not isinstance(x, tuple):
        return [pl.BlockSpec((tm, D), lambda i: (i, 0))], (x,)
    n = T_CTX // tm
    return [pl.BlockSpec((tm, D), lambda i: (jnp.minimum(i, n - 1), 0)),
            pl.BlockSpec((tm, D), lambda i: (jnp.maximum(i - n, 0), 0))], x


def _load_x(x_refs):
    if len(x_refs) == 1:
        return x_refs[0][...]
    n = T_CTX // x_refs[0].shape[0]
    return jnp.where(pl.program_id(0) < n, x_refs[0][...], x_refs[1][...])


def _inproj_kernel(n_x, *refs):
    (mod_ref, g_ref, w_ref, wup_ref, aup_ref, gup_ref, w0_ref, a0_ref, kk_k_ref, k_a_ref, r_k_ref, ones4_ref,
     rkv_ref, bcd_ref, wdec_ref, b_ref, kin_ref, kk_ref, gate_ref, bonus_ref, w16_scr) = refs[n_x:]

    @pl.when(pl.program_id(0) == 0)
    def _():
        w16_scr[...] = w_ref[...].astype(BF16)

    h = _norm_mod(_load_x(refs[:n_x]), g_ref[...], mod_ref[0, 1:2, :], mod_ref[0, 0:1, :])
    z = _dot(h.astype(BF16), w16_scr[...])
    rkv_ref[...] = z[:, :3 * W]
    bcd_ref[...] = z[:, 3 * W + 3 * LANE:]
    r = z[:, 0:W]
    k = z[:, W:2 * W]
    v = z[:, 2 * W:3 * W]
    xw = z[:, 3 * W:3 * W + LANE]
    xa = z[:, 3 * W + LANE:3 * W + 2 * LANE]
    xg = z[:, 3 * W + 2 * LANE:3 * W + 3 * LANE]
    ones4 = ones4_ref[...]
    dec = w0_ref[...] + _dot_hl(jnp.tanh(xw), wup_ref[...])
    wdec_ref[...] = jnp.exp(-DECAY_SCALE * _sigmoid(dec))
    a = _sigmoid(a0_ref[...] + _dot_hl(xa, aup_ref[...]))
    gate_ref[...] = _dot_hl(_sigmoid(xg), gup_ref[...])
    kk = k * kk_k_ref[...]
    n2 = _sel_dot_r(kk * kk, ones4, parts=3)
    kk = kk / jnp.maximum(jnp.sqrt(n2), 1e-12)
    kk_ref[...] = kk
    k_a = k_a_ref[...]
    kin0 = k * (1.0 + (a[:, 0:W] - 1.0) * k_a)
    kin1 = k * (1.0 + (a[:, W:2 * W] - 1.0) * k_a)
    kin_ref[:, 0:W] = kin0
    kin_ref[:, W:2 * W] = kin1
    b_ref[:, 0:W] = a[:, 0:W] * kk
    b_ref[:, W:2 * W] = a[:, W:2 * W] * kk
    bonus = _sel_dot_r(r * (kin0 + kin1) * r_k_ref[...], ones4, parts=3)
    bonus_ref[...] = bonus * v


def _in_projection(layer, x, mod_l, g1, w_in, wup_bd, aup_bd, g_up, w0, a0, k_k, k_a, r_k, ones4):
    tm = TM_WIDE
    tok = lambda c: pl.BlockSpec((tm, c), lambda i: (i, 0))
    full = lambda a: pl.BlockSpec(a.shape, lambda i: (0,) * a.ndim)
    params = (wup_bd, aup_bd, g_up, w0, a0, k_k, k_a, r_k, ones4)
    widths = (3 * W, 6 * W, 2 * W, 2 * W, 2 * W, W, W, W)
    x_specs, x = _x_specs(x, tm)
    return pl.pallas_call(
        functools.partial(_inproj_kernel, len(x)),
        grid=(T_ALL // tm,),
        in_specs=x_specs + [pl.BlockSpec((1, N_MOD, D), lambda i: (_mod_row(i, tm), 0, 0)), full(g1),
                            pl.BlockSpec((None, D, N_SMALL), lambda i: (layer, 0, 0), pipeline_mode=pl.Buffered(1))]
                 + [full(a) for a in params],
        out_specs=[tok(c) for c in widths],
        out_shape=[jax.ShapeDtypeStruct((T_ALL, c), F32) for c in widths],
        scratch_shapes=[pltpu.VMEM((D, N_SMALL), BF16)],
        compiler_params=_cparams(1),
        name="in_projection",
    )(*x, mod_l, g1, w_in, *params)


T_STEP = 2 * SUBLANE
CHAINS_PER_DOT = 4
CTX_SEQ_PER_SCAN_BLK = 4
N_SCAN_BLK = N_CTX_SEQ // CTX_SEQ_PER_SCAN_BLK
LAT_WIN = L_LAT // N_SCAN_BLK
SCAN_FIELDS = ('r', 'v', 'kk', 'w', 'b', 'k')


def _scan_kernel(*refs):
    n_f = len(SCAN_FIELDS)
    ctx_refs = dict(zip(SCAN_FIELDS, refs[:n_f]))
    lat_refs = {}
    pos = n_f
    for s in range(N_LAT_SEQ):
        for d in range(2):
            lat_refs[(s, d)] = dict(zip(SCAN_FIELDS, refs[pos:pos + n_f]))
            pos += n_f
    s0c_ref, s0l_ref, ones4_ref, eye4_ref, yc_ref, sfin_ref, yl_ref, slat_scr = refs[pos:]
    step_i = pl.program_id(0)
    ones4 = ones4_ref[...]
    eye4 = eye4_ref[...]
    chains = ([(False, b, d) for b in range(CTX_SEQ_PER_SCAN_BLK) for d in range(2)]
              + [(True, s, d) for s in range(N_LAT_SEQ) for d in range(2)])
    n_chain = len(chains)
    groups = [chains[g:g + CHAINS_PER_DOT] for g in range(0, n_chain, CHAINS_PER_DOT)]

    @pl.when(step_i == 0)
    def _():
        yl_ref[...] = jnp.zeros_like(yl_ref)
        for s in range(N_LAT_SEQ):
            for d in range(2):
                slat_scr[s, d] = s0l_ref[s, d]

    yc_ref[...] = jnp.zeros_like(yc_ref)

    eye16 = eye4.astype(BF16)
    lane_in_head = jnp.bitwise_and(lax.broadcasted_iota(jnp.int32, (SUBLANE, W), 1), HEAD - 1)
    row_group = lane_in_head // SUBLANE
    on_diag = jnp.bitwise_and(lane_in_head, SUBLANE - 1) == lax.broadcasted_iota(jnp.int32, (SUBLANE, W), 0)

    def diagonal(x):
        m = x[0:SUBLANE]
        for g in range(1, HEAD // SUBLANE):
            m = jnp.where(row_group == g, x[g * SUBLANE:(g + 1) * SUBLANE], m)
        return jnp.sum(jnp.where(on_diag, m, 0.0), axis=0, keepdims=True)

    def row_sum(blocks):
        return _dot(jnp.concatenate(blocks, axis=0), ones4)

    def body(tb, state):
        state = list(state)
        tiles = []
        for lat, b, d in chains:
            t0 = tb * T_STEP if d == 0 else L_CTX - T_STEP - tb * T_STEP
            if lat:
                rows = pl.ds(pl.multiple_of(t0, T_STEP), T_STEP)
                tiles.append({f: lat_refs[(b, d)][f][rows, :] for f in SCAN_FIELDS})
            else:
                rows = pl.ds(pl.multiple_of(b * L_CTX + t0, T_STEP), T_STEP)
                col = lambda f: slice(d * W, (d + 1) * W) if f in ('w', 'b', 'k') else slice(0, W)
                tiles.append({f: ctx_refs[f][rows, col(f)] for f in SCAN_FIELDS})

        def row(name, c, j):
            jj = j if chains[c][2] == 0 else T_STEP - 1 - j
            return tiles[c][name][jj:jj + 1, :]

        def row16(name, c, j):
            return row(name, c, j).astype(BF16)

        state16 = [s.astype(BF16) for s in state]
        y_rows = [[None] * T_STEP for _ in chains]
        def outputs(j, q):
            y_cols = row_sum(q)
            for c, (lat, b, d) in enumerate(chains):
                jj = j if d == 0 else T_STEP - 1 - j
                y_rows[c][jj] = diagonal(y_cols[c * HEAD:(c + 1) * HEAD])

        def feedback(grp_cs, j):
            return row_sum([state16[c] * row16('kk', c, j) for c in grp_cs]
                           + [eye16 * row16('v', c, j) for c in grp_cs])

        group_cs = [[chains.index(ch) for ch in grp] for grp in groups]
        sums = [feedback(cs, 0) for cs in group_cs]
        for j in range(T_STEP):
            q = [None] * n_chain
            for g, cs in enumerate(group_cs):
                n = len(cs)
                for i, c in enumerate(cs):
                    s = (state[c] * row('w', c, j) - sums[g][i * HEAD:(i + 1) * HEAD] * row('b', c, j)
                         + sums[g][(n + i) * HEAD:(n + i + 1) * HEAD] * row('k', c, j))
                    state[c] = s
                    state16[c] = s.astype(BF16)
                    q[c] = state16[c] * row16('r', c, j)
                if j + 1 < T_STEP:
                    sums[g] = feedback(cs, j + 1)
            outputs(j, q)
        for c, (lat, b, d) in enumerate(chains):
            t0 = tb * T_STEP if d == 0 else L_CTX - T_STEP - tb * T_STEP
            y_tile = jnp.concatenate(y_rows[c], axis=0)
            if lat:
                win0 = step_i * LAT_WIN if d == 0 else L_LAT - LAT_WIN - step_i * LAT_WIN
                rows = pl.ds(pl.multiple_of(b * L_LAT + win0 + t0, T_STEP), T_STEP)
                yl_ref[rows, :] += y_tile
            else:
                rows = pl.ds(pl.multiple_of(b * L_CTX + t0, T_STEP), T_STEP)
                yc_ref[rows, :] += y_tile
        return tuple(state)

    init = tuple(slat_scr[b, d] if lat else s0c_ref[b, d] for (lat, b, d) in chains)
    fin = lax.fori_loop(0, L_CTX // T_STEP, body, init)
    for c, (lat, b, d) in enumerate(chains):
        if lat:
            slat_scr[b, d] = fin[c]
        else:
            sfin_ref[b, d] = fin[c]


def _rwkv_scan(rkv, kk, wdec, bvec, kin, s0_ctx, s0_lat, ones4, eye4):
    assert LAT_WIN == L_CTX
    arrays = dict(r=(rkv, 0), v=(rkv, 2), kk=(kk, 0), w=(wdec, None), b=(bvec, None), k=(kin, None))
    rows_c = CTX_SEQ_PER_SCAN_BLK * L_CTX
    operands, in_specs = [], []
    for f in SCAN_FIELDS:
        a, col = arrays[f]
        operands.append(a)
        in_specs.append(pl.BlockSpec((rows_c, W if col is not None else 2 * W),
                                     lambda i, col=col: (i, col if col is not None else 0)))
    win_blk0 = T_CTX // LAT_WIN
    for s in range(N_LAT_SEQ):
        for d in range(2):
            for f in SCAN_FIELDS:
                a, col = arrays[f]
                operands.append(a)
                in_specs.append(pl.BlockSpec(
                    (LAT_WIN, W),
                    lambda i, s=s, d=d, col=col: (win_blk0 + s * N_SCAN_BLK + (i if d == 0 else N_SCAN_BLK - 1 - i),
                                                  col if col is not None else d)))
    st = pl.BlockSpec((None, CTX_SEQ_PER_SCAN_BLK, 2, HEAD, W), lambda i: (i, 0, 0, 0, 0))
    full = lambda a: pl.BlockSpec(a.shape, lambda i: (0,) * a.ndim)
    operands += [s0_ctx, s0_lat, ones4, eye4]
    in_specs += [st, full(s0_lat), full(ones4), full(eye4)]
    return pl.pallas_call(
        _scan_kernel,
        grid=(N_SCAN_BLK,),
        in_specs=in_specs,
        out_specs=[pl.BlockSpec((rows_c, W), lambda i: (i, 0)), st, pl.BlockSpec((T_LAT, W), lambda i: (0, 0))],
        out_shape=[jax.ShapeDtypeStruct((T_CTX, W), F32),
                   jax.ShapeDtypeStruct((N_SCAN_BLK, CTX_SEQ_PER_SCAN_BLK, 2, HEAD, W), F32),
                   jax.ShapeDtypeStruct((T_LAT, W), F32)],
        scratch_shapes=[pltpu.VMEM((N_LAT_SEQ, 2, HEAD, W), F32)],
        compiler_params=_cparams(1),
        name="rwkv_scan",
    )(*operands)


def _branch_kernel(yc_ref, yl_ref, g_ref, bonus_ref, bcd_ref, lnx_g_ref, lnx_b_ref, band_ref, invcnt_ref, shift_ref,
                   wpool_ref, pscale_ref, conv_w_ref, conv_b_ref, sgu_g_ref, ws_ref, bs_ref, ones4_ref, br_ref):
    ones4 = ones4_ref[...]
    group = lax.broadcasted_iota(jnp.int32, (TM, W), 1) // HEAD

    def seg_mean(x):
        return _sel_dot_r(x, ones4, parts=2) * (1.0 / HEAD)

    y = jnp.where(pl.program_id(0) < N_CTX_BLK, yc_ref[...], yl_ref[...])
    mu = seg_mean(y)
    yc = y - mu
    var = seg_mean(yc * yc)
    ya = yc * lax.rsqrt(var + GN_EPS) * lnx_g_ref[...] + lnx_b_ref[...] + bonus_ref[...]
    br_ref[:, 0:W] = (ya * g_ref[...]).astype(BF16)

    u = bcd_ref[:, 0:W]
    u_parts = _split2(u)
    win = jnp.zeros((TM, W), F32)
    for gi in range(len(POOL_WINDOWS)):
        band = band_ref[0, gi]
        s = _dot(band, u_parts[0]) + _dot(band, u_parts[1])
        win = jnp.where(group == gi, s, win)
    pooled = win * invcnt_ref[0] - u
    br_ref[:, W:2 * W] = (_dot_hl(pooled, wpool_ref[...]) * pscale_ref[...]).astype(BF16)

    cin = bcd_ref[:, W:2 * W]
    cb = bcd_ref[:, 2 * W:3 * W]
    cc = bcd_ref[:, 3 * W:4 * W]
    t = cc * cin
    t_prev = _sel_dot_l(shift_ref[0, 0], t, parts=2)
    t_next = _sel_dot_l(shift_ref[0, 1], t, parts=2)
    conv = conv_b_ref[...] + t_prev * conv_w_ref[0:1, :]
    conv = conv + t * conv_w_ref[1:2, :]
    conv = conv + t_next * conv_w_ref[2:3, :]
    br_ref[:, 2 * W:3 * W] = (cb * conv).astype(BF16)

    du = _gelu_tanh(bcd_ref[:, 4 * W:5 * W])
    dv = _gelu_tanh(bcd_ref[:, 5 * W:6 * W])
    mu = seg_mean(dv)
    dc = dv - mu
    var = seg_mean(dc * dc)
    vn = dc * lax.rsqrt(var + NORM_EPS) * sgu_g_ref[...]
    group_c = lax.broadcasted_iota(jnp.int32, (CHUNK, W), 1) // HEAD
    for ch in range(TM // CHUNK):
        rows = slice(ch * CHUNK, (ch + 1) * CHUNK)
        vh, vl = _split2(vn[rows, :])
        s = jnp.zeros((CHUNK, W), F32)
        for gi in range(N_HEADS):
            wh, wl = _split2(ws_ref[gi])
            sg = _dot(wh, vh) + _dot(wl, vh) + _dot(wh, vl)
            s = jnp.where(group_c == gi, sg, s)
        br_ref[rows, 3 * W:4 * W] = (du[rows, :] * (s + bs_ref[...])).astype(BF16)


def _branches(y_ctx, y_lat, g, bonus, bcd, p):
    tok = lambda c: pl.BlockSpec((TM, c), lambda i: (i, 0))
    y_specs = [pl.BlockSpec((TM, W), lambda i: (jnp.minimum(i, N_CTX_BLK - 1), 0)),
               pl.BlockSpec((TM, W), lambda i: (jnp.maximum(i - N_CTX_BLK, 0), 0))]
    full = lambda a: pl.BlockSpec(a.shape, lambda i: (0,) * a.ndim)
    lay = lambda a: pl.BlockSpec((1,) + a.shape[1:], lambda i: (jnp.where(i < N_CTX_BLK, 0, 1),) + (0,) * (a.ndim - 1))
    return pl.pallas_call(
        _branch_kernel,
        grid=(N_BLK,),
        in_specs=y_specs + [tok(W), tok(W), tok(6 * W), full(p['lnx_g']), full(p['lnx_b']),
                  lay(p['band']), lay(p['invcnt']), lay(p['shift']),
                  full(p['wpool_bd']), full(p['pool_scale']), full(p['conv_w']), full(p['conv_b']),
                  full(p['sgu_g']), full(p['w_s']), full(p['bs_full']), full(p['ones4'])],
        out_specs=tok(4 * W),
        out_shape=jax.ShapeDtypeStruct((T_ALL, 4 * W), BF16),
        compiler_params=_cparams(1),
        name="branches",
    )(y_ctx, y_lat, g, bonus, bcd, p['lnx_g'], p['lnx_b'], p['band'], p['invcnt'], p['shift'], p['wpool_bd'],
      p['pool_scale'], p['conv_w'], p['conv_b'], p['sgu_g'], p['w_s'], p['bs_full'], p['ones4'])


def _merge_kernel(n_x, *refs):
    mod_ref, g1_ref, br_ref, wgl_ref, wbr_ref, wout_ref, g2_ref, wr_ref, o_ref, h_ref, aff_ref = refs[n_x:]
    x = _load_x(refs[:n_x])
    h = _norm_mod(x, g1_ref[...], mod_ref[0, 1:2, :], mod_ref[0, 0:1, :]).astype(BF16)
    merged = jnp.zeros(x.shape, F32)
    for i in range(4):
        gl = _dot(h, wgl_ref[:, i * D:(i + 1) * D])
        proj = _dot(br_ref[:, i * W:(i + 1) * W], wbr_ref[i])
        merged = merged + _sigmoid(gl) * proj
    mix = _dot(merged.astype(BF16), wout_ref[...])
    x = x + mod_ref[0, 2:3, :] * mix
    o_ref[...] = x

    h2 = _norm_mod(x, g2_ref[...], mod_ref[0, 4:5, :], mod_ref[0, 3:4, :])
    h_ref[...] = h2.astype(BF16)
    logits = _dot_hl(h2, wr_ref[...])
    lane = lax.broadcasted_iota(jnp.int32, logits.shape, 1)
    logits = jnp.where(lane < N_EXPERTS, logits, -1e30)
    m = jnp.max(logits, axis=-1, keepdims=True)
    e = jnp.where(lane < N_EXPERTS, jnp.exp(logits - m), 0.0)
    aff_ref[...] = e / jnp.sum(e, axis=-1, keepdims=True)


def _merge(x, mod_l, g1, br, w_gl, w_branch, w_out, g2, w_router_pad):
    full = lambda a: pl.BlockSpec(a.shape, lambda i: (0,) * a.ndim)
    tm = TM_WIDE
    tok = lambda c: pl.BlockSpec((tm, c), lambda i: (i, 0))
    x_specs, x = _x_specs(x, tm)
    return pl.pallas_call(
        functools.partial(_merge_kernel, len(x)),
        grid=(T_ALL // tm,),
        in_specs=x_specs + [pl.BlockSpec((1, N_MOD, D), lambda i: (_mod_row(i, tm), 0, 0)), full(g1), tok(4 * W),
                            full(w_gl), full(w_branch), full(w_out), full(g2), full(w_router_pad)],
        out_specs=[tok(D), tok(D), tok(LANE)],
        out_shape=[jax.ShapeDtypeStruct((T_ALL, D), F32), jax.ShapeDtypeStruct((T_ALL, D), BF16),
                   jax.ShapeDtypeStruct((T_ALL, LANE), F32)],
        compiler_params=_cparams(1),
        name="merge",
    )(*x, mod_l, g1, br, w_gl, w_branch, w_out, g2, w_router_pad)


def _rank_kernel(seq_len, aff_ref, afft_ref, ones_ref, rank_ref):
    nb = seq_len // TM
    earlier = jnp.where(lax.broadcasted_iota(jnp.int32, (TM, TM), 1) < lax.broadcasted_iota(jnp.int32, (TM, TM), 0),
                        1.0, 0.0)
    lane = lax.broadcasted_iota(jnp.int32, (TM, LANE), 1)
    for j in range(nb):
        aff = aff_ref[j * TM:(j + 1) * TM, :]
        rank = jnp.zeros((TM, LANE), F32)
        for e in range(N_EXPERTS):
            mine = aff[:, e:e+1]
            tiles = []
            for c in range(nb):
                other = afft_ref[e:e+1, c * TM:(c + 1) * TM]
                if c < j:
                    tiles.append(jnp.where(other >= mine, 1.0, 0.0))
                elif c > j:
                    tiles.append(jnp.where(other > mine, 1.0, 0.0))
                else:
                    tiles.append(jnp.where(other > mine, 1.0, jnp.where(other == mine, earlier, 0.0)))
            cnt = _dot(jnp.concatenate(tiles, axis=1).astype(BF16), ones_ref[...])
            rank = jnp.where(lane == e, cnt, rank)
        rank_ref[j * TM:(j + 1) * TM, :] = rank


def _ranks(seq_len, n_seq, blk0, aff, aff_t):
    nb = seq_len // TM
    ones = jnp.ones((seq_len, LANE), BF16)
    return pl.pallas_call(
        functools.partial(_rank_kernel, seq_len),
        grid=(n_seq,),
        in_specs=[pl.BlockSpec((seq_len, LANE), lambda s: (blk0 // nb + s, 0)),
                  pl.BlockSpec((None, N_EXPERTS, seq_len), lambda s: (s, 0, 0)),
                  pl.BlockSpec((seq_len, LANE), lambda s: (0, 0))],
        out_specs=pl.BlockSpec((seq_len, LANE), lambda s: (s, 0)),
        out_shape=jax.ShapeDtypeStruct((n_seq * seq_len, LANE), F32),
        compiler_params=_cparams(1),
        name=f"ranks_{seq_len}",
    )(aff, aff_t, ones)


def _expert_kernel(h_ref, rank_ref, w1_ref, w3_ref, w2_ref, o_ref, xs_scr):
    def gather(seq_len, cap, tok0, slot0):
        slot = lax.broadcasted_iota(jnp.int32, (cap, seq_len), 0).astype(F32)
        rk = rank_ref[:, tok0:tok0 + seq_len]
        onehot = jnp.where(rk == slot, 1.0, 0.0).astype(BF16)
        xs_scr[slot0:slot0 + cap, :] = _dot(onehot, h_ref[tok0:tok0 + seq_len, :]).astype(BF16)

    for s in range(N_CTX_SEQ):
        gather(L_CTX, CAP_CTX, s * L_CTX, s * CAP_CTX)
    for s in range(N_LAT_SEQ):
        gather(L_LAT, CAP_LAT, T_CTX + s * L_LAT, N_CTX_SEQ * CAP_CTX + s * CAP_LAT)
    xs = xs_scr[...]
    a = _dot(xs, w1_ref[...].astype(BF16))
    act = (a * _sigmoid(a)) * _dot(xs, w3_ref[...].astype(BF16))
    o_ref[...] = _dot(act.astype(BF16), w2_ref[...].astype(BF16)).astype(BF16)


def _experts(layer, h2, rank_t, w1, w3, w2):
    wspec = lambda: pl.BlockSpec((None, None, D, FF), lambda e: (layer, e, 0, 0))
    return pl.pallas_call(
        _expert_kernel,
        grid=(N_EXPERTS,),
        in_specs=[pl.BlockSpec((T_ALL, D), lambda e: (0, 0), pipeline_mode=pl.Buffered(1)),
                  pl.BlockSpec((None, 1, T_ALL), lambda e: (e, 0, 0)),
                  wspec(), wspec(), pl.BlockSpec((None, None, FF, D), lambda e: (layer, e, 0, 0))],
        out_specs=pl.BlockSpec((None, SLOTS, D), lambda e: (e, 0, 0)),
        out_shape=jax.ShapeDtypeStruct((N_EXPERTS, SLOTS, D), BF16),
        scratch_shapes=[pltpu.VMEM((SLOTS, D), BF16)],
        compiler_params=_cparams(1),
        name="experts",
    )(h2, rank_t, w1, w3, w2)


def _combine_kernel(cap, final, x_ref, mod_ref, aff_ref, rank_ref, oe_ref, expand_ref, fg_ref, o_ref):
    n = N_EXPERTS * cap
    expand = expand_ref[...]
    rank_x = _sel_dot_r(rank_ref[...], expand, parts=2)
    aff_x = _dot(aff_ref[...].astype(BF16), expand)
    slot = jnp.bitwise_and(lax.broadcasted_iota(jnp.int32, (TM, n), 1), cap - 1).astype(F32)
    gate = jnp.where(rank_x == slot, aff_x, 0.0)
    acc = _dot(gate.astype(BF16), oe_ref[...].reshape(n, D))
    x = x_ref[...] + mod_ref[0, 5:6, :] * acc
    if final:
        ms = jnp.mean(x * x, axis=-1, keepdims=True)
        x = x * lax.rsqrt(ms + NORM_EPS) * fg_ref[...]
    o_ref[...] = x


def _combine(seq_len, n_seq, blk0, cap, slot_blk0, final, x, mod_l, aff, rank, oe, final_g):
    nb = seq_len // TM
    assert cap & (cap - 1) == 0
    tok = lambda c: pl.BlockSpec((TM, c), lambda s, j: (blk0 + s * nb + j, 0))
    expand = jnp.asarray(np.arange(LANE)[:, None] == np.arange(N_EXPERTS * cap)[None, :] // cap, BF16)
    if final:
        out_spec = pl.BlockSpec((TM, D), lambda s, j: (s * nb + j, 0))
        out_shape, aliases = jax.ShapeDtypeStruct((n_seq * seq_len, D), F32), {}
    else:
        out_spec, out_shape, aliases = tok(D), jax.ShapeDtypeStruct((T_ALL, D), F32), {0: 0}
    return pl.pallas_call(
        functools.partial(_combine_kernel, cap, final),
        grid=(n_seq, nb),
        in_specs=[tok(D),
                  pl.BlockSpec((1, N_MOD, D), lambda s, j: (_mod_row(blk0 + s * nb + j), 0, 0)),
                  tok(LANE),
                  pl.BlockSpec((TM, LANE), lambda s, j: (s * nb + j, 0)),
                  pl.BlockSpec((N_EXPERTS, cap, D), lambda s, j: (0, slot_blk0 + s, 0)),
                  pl.BlockSpec((LANE, N_EXPERTS * cap), lambda s, j: (0, 0)),
                  pl.BlockSpec((1, D), lambda s, j: (0, 0))],
        out_specs=out_spec,
        out_shape=out_shape,
        input_output_aliases=aliases,
        compiler_params=_cparams(2),
        name=f"combine_{seq_len}",
    )(x, mod_l, aff, rank, oe, expand, final_g)


def _row_structure(row_len):
    t = np.arange(TM)
    same_row = (t[:, None] // row_len) == (t[None, :] // row_len)
    delta = t[None, :] - t[:, None]
    band = np.stack([same_row & (delta >= -(w // 2)) & (delta < w // 2) for w in POOL_WINDOWS]).astype(np.float32)
    cnt = band.sum(-1)
    invcnt = np.repeat((1.0 / cnt).T, HEAD, axis=1).astype(np.float32)
    shift = np.stack([same_row & (delta == -1), same_row & (delta == 1)]).astype(np.float32)
    return band, invcnt, shift


def _constants():
    lane = np.arange(W)
    ones4 = (lane[:, None] // HEAD == lane[None, :] // HEAD).astype(np.float32)
    eye4 = (np.arange(HEAD)[:, None] == lane[None, :] % HEAD).astype(np.float32)
    structs = [_row_structure(L_CTX), _row_structure(GRID_W)]
    return dict(
        ones4=jnp.asarray(ones4, BF16), eye4=jnp.asarray(eye4, F32),
        band=jnp.asarray(np.stack([s[0] for s in structs]), BF16),
        invcnt=jnp.asarray(np.stack([s[1] for s in structs]), F32),
        shift=jnp.asarray(np.stack([s[2] for s in structs]), BF16))


def _block_diag(blocks):
    n = len(blocks)
    rows = []
    for i, b in enumerate(blocks):
        rows.append(jnp.concatenate([b if j == i else jnp.zeros((b.shape[0], blocks[j].shape[1]), b.dtype)
                                     for j in range(n)], axis=1))
    return jnp.concatenate(rows, axis=0)


def _wide_tiles(s):
    return jnp.moveaxis(s, -3, -2).reshape(s.shape[:-3] + (HEAD, W))


def _head_tiles(s):
    return jnp.moveaxis(s.reshape(s.shape[:-2] + (HEAD, N_HEADS, HEAD)), -2, -3)


def kernel(x_prompt, x_sample, state_rwkv, c, c_ctx, norm1_g, w_mod, b_mod, w_in, w0, w_up, a0, a_up, g_up, k_k, k_a, r_k, lnx_g, lnx_b, w_pool, pool_scale, conv_w, conv_b, sgu_g, w_s, b_s, w_branch, w_out, norm2_g, w_router, w_e1, w_e3, w_e2, final_g):
    const = _constants()
    x = (x_prompt.reshape(T_CTX, D), x_sample.reshape(T_LAT, D))
    cond8 = jnp.concatenate([c_ctx[None], c, jnp.zeros((SUBLANE - 1 - N_LAT_SEQ, D), F32)], axis=0)
    mod = _modulation(cond8, w_mod, b_mod).reshape(DEPTH, SUBLANE, N_MOD, D)
    final_g2 = final_g.reshape(1, D)
    ctx_states = []
    y_ctx = y_lat = None
    for l in range(DEPTH):
        mod_l = mod[l]
        g1 = norm1_g[l].reshape(1, D)
        wup_bd = _block_diag([w_up[l, 0], w_up[l, 1]])
        aup_bd = _block_diag([a_up[l, 0], a_up[l, 1]])
        rkv, bcd, wdec, bvec, kin, kk, gate_a, bonus = _in_projection(
            l, x, mod_l, g1, w_in, wup_bd, aup_bd, g_up[l], w0[l].reshape(1, 2 * W),
            a0[l].reshape(1, 2 * W), k_k[l].reshape(1, W), k_a[l].reshape(1, W), r_k[l].reshape(1, W),
            const['ones4'])

        s0_ctx = jnp.zeros((N_SCAN_BLK, CTX_SEQ_PER_SCAN_BLK, 2, HEAD, W), F32)
        y_c, s_ctx, y_l = _rwkv_scan(rkv, kk, wdec, bvec, kin, s0_ctx, _wide_tiles(state_rwkv[:, l]),
                                     const['ones4'], const['eye4'])
        ctx_states.append(_head_tiles(s_ctx).reshape(N_CTX_SEQ, 2, N_HEADS, HEAD, HEAD))

        bs_full = jnp.repeat(b_s[l].T, HEAD, axis=1)
        bp = dict(lnx_g=lnx_g[l].reshape(1, W), lnx_b=lnx_b[l].reshape(1, W), band=const['band'],
                  invcnt=const['invcnt'], shift=const['shift'],
                  wpool_bd=_block_diag([w_pool[l, i] for i in range(len(POOL_WINDOWS))]),
                  pool_scale=pool_scale[l].reshape(1, W), conv_w=conv_w[l], conv_b=conv_b[l].reshape(1, W),
                  sgu_g=sgu_g[l].reshape(1, W), w_s=w_s[l], bs_full=bs_full, ones4=const['ones4'])
        br = _branches(y_c, y_l, gate_a, bonus, bcd, bp)
        wr_pad = jnp.concatenate([w_router[l], jnp.zeros((D, LANE - N_EXPERTS), F32)], axis=1)
        x, h2, aff = _merge(x, mod_l, g1, br, w_in[l, :, N_SMALL:].astype(BF16), w_branch[l].astype(BF16),
                            w_out[l].astype(BF16), norm2_g[l].reshape(1, D), wr_pad)
        aff_t_ctx = jnp.swapaxes(aff[:T_CTX, :N_EXPERTS].reshape(N_CTX_SEQ, L_CTX, N_EXPERTS), 1, 2)
        aff_t_lat = jnp.swapaxes(aff[T_CTX:, :N_EXPERTS].reshape(N_LAT_SEQ, L_LAT, N_EXPERTS), 1, 2)
        rank_ctx = _ranks(L_CTX, N_CTX_SEQ, 0, aff, aff_t_ctx)
        rank_lat = _ranks(L_LAT, N_LAT_SEQ, N_CTX_BLK, aff, aff_t_lat)
        rank_t = jnp.concatenate([rank_ctx[:, :N_EXPERTS], rank_lat[:, :N_EXPERTS]], axis=0).T
        oe = _experts(l, h2, rank_t.reshape(N_EXPERTS, 1, T_ALL), w_e1, w_e3, w_e2)
        final = l == DEPTH - 1
        y_ctx = _combine(L_CTX, N_CTX_SEQ, 0, CAP_CTX, 0, final, x, mod_l, aff, rank_ctx, oe, final_g2)
        y_lat = _combine(L_LAT, N_LAT_SEQ, N_CTX_BLK, CAP_LAT, N_CTX_SEQ * CAP_CTX // CAP_LAT, final,
                         x if final else y_ctx, mod_l, aff, rank_lat, oe, final_g2)
        if not final:
            x = y_lat
    y_prompt = y_ctx.reshape(N_CTX_SEQ, L_CTX, D)
    y_sample = y_lat.reshape(N_LAT_SEQ, L_LAT, D)
    new_state = jnp.stack(ctx_states, axis=1)
    return (y_prompt, y_sample, new_state)
```
